```python
import math
import jax
import jax.numpy as jnp
from jax import lax
import numpy as np

D_MODEL = 2048
BATCH = 4
SEQ = 2048
DEPTH = 4
DEC_BATCH = 128
DEC_SEQ = 8
PAST_LEN = 16384
PAGE_SIZE = 128

N_MIXERS = 3
N_META = 16
EPS = 1e-6
D_FF = 5632

M_D_INNER = 2 * D_MODEL
M_HEADDIM = 64
M_HEADS = M_D_INNER // M_HEADDIM
M_GROUPS = 8
M_D_STATE = 128
M_CONV = 4
M_CHUNK = 64
M_CONV_DIM = M_D_INNER + 2 * M_GROUPS * M_D_STATE
M_PROJ = M_D_INNER + M_CONV_DIM + M_HEADS
M_NORM_EPS = 1e-5

R_HEADSIZE = 64
R_HEADS = D_MODEL // R_HEADSIZE
R_DECAY_LORA = 96
R_A_LORA = 96
R_GATE_LORA = 256
R_LN_EPS = 64e-5

H_HEADS = 16
H_EXPAND = 128
H_FDIM = H_HEADS * H_EXPAND
H_VDIM = D_MODEL // H_HEADS
H_CHUNK = 16

N_A = (DEPTH + N_MIXERS - 1) // N_MIXERS
N_B = (DEPTH + N_MIXERS - 2) // N_MIXERS
N_C = (DEPTH + N_MIXERS - 3) // N_MIXERS

kernel_name = "hybrid_ssd_rwkv7_hgrn2_macaron_step"

F32 = jnp.float32


def rmsnorm(x, g, eps=EPS):
    xf = x.astype(F32)
    y = xf * lax.rsqrt(jnp.mean(xf * xf, axis=-1, keepdims=True) + eps)
    return (y * g.astype(F32)).astype(x.dtype)


def swiglu(x, w_gate, w_up, w_down):
    return (jax.nn.silu(x @ w_gate) * (x @ w_up)) @ w_down


def causal_depthwise_conv(x, buf, w, b):
    L = x.shape[1]
    xp = jnp.concatenate([buf.astype(x.dtype), x], axis=1)
    y = b + sum(xp[:, j:j + L] * w[j] for j in range(M_CONV))
    return y, xp[:, -(M_CONV - 1):]


def run_chunks(step, state, seqs, lead, chunk):
    T = seqs[0].shape[1]
    lead = lead + (T - lead) % chunk
    outs = []
    if lead > 0:
        state, y = step(state, tuple(s[:, :lead] for s in seqs))
        outs.append(y)
    n_blocks = (T - lead) // chunk
    if n_blocks > 0:
        blocks = tuple(jnp.moveaxis(s[:, lead:].reshape((s.shape[0], n_blocks, chunk) + s.shape[2:]), 1, 0) for s in seqs)
        state, y = lax.scan(step, state, blocks)
        y = jnp.moveaxis(y, 0, 1)
        outs.append(y.reshape((y.shape[0], n_blocks * chunk) + y.shape[3:]))
    return state, jnp.concatenate(outs, axis=1)


def ssd_chunk(S, inp, a):
    x, dt, bm, cm = inp
    Bsz, L = x.shape[:2]
    R = M_HEADS // M_GROUPS
    x = x.reshape(Bsz, L, M_GROUPS, R, M_HEADDIM)
    dt = dt.reshape(Bsz, L, M_GROUPS, R)
    cum = jnp.cumsum(dt * a.reshape(M_GROUPS, R), axis=1)
    causal = jnp.tril(jnp.ones((L, L), dtype=bool))[None, :, :, None, None]
    decay = jnp.exp(jnp.where(causal, cum[:, :, None] - cum[:, None, :], -jnp.inf))
    cb = jnp.einsum('btgn,bsgn->btsg', cm, bm)
    scores = cb[..., None] * decay * dt[:, None]
    y = jnp.einsum('btsgr,bsgrp->btgrp', scores, x)
    Sg = S.reshape(Bsz, M_GROUPS, R, M_HEADDIM, M_D_STATE)
    y = y + jnp.einsum('btgn,bgrpn->btgrp', cm, Sg) * jnp.exp(cum)[..., None]
    tail = jnp.exp(cum[:, -1:] - cum) * dt
    Sg = jnp.exp(cum[:, -1])[..., None, None] * Sg + jnp.einsum('bsgrp,bsgn->bgrpn', tail[..., None] * x, bm)
    return Sg.reshape(S.shape), y.reshape(Bsz, L, M_HEADS, M_HEADDIM)


def mamba2_mixer(u, ssm, conv_buf, lead, p):
    Bsz, T, _ = u.shape
    proj = u @ p['w_in']
    z = proj[..., :M_D_INNER]
    xbc = proj[..., M_D_INNER:M_D_INNER + M_CONV_DIM]
    dt = proj[..., M_D_INNER + M_CONV_DIM:]
    xbc, conv_new = causal_depthwise_conv(xbc, conv_buf, p['conv_w'], p['conv_b'])
    xbc = jax.nn.silu(xbc).astype(F32)
    gn = M_GROUPS * M_D_STATE
    xs = xbc[..., :M_D_INNER].reshape(Bsz, T, M_HEADS, M_HEADDIM)
    bm = xbc[..., M_D_INNER:M_D_INNER + gn].reshape(Bsz, T, M_GROUPS, M_D_STATE)
    cm = xbc[..., M_D_INNER + gn:].reshape(Bsz, T, M_GROUPS, M_D_STATE)
    dt = jax.nn.softplus((dt + p['dt_bias']).astype(F32))
    a = -jnp.exp(p['a_log'].astype(F32))

    def step(S, inp):
        return ssd_chunk(S, inp, a)

    S, y = run_chunks(step, ssm.astype(F32), (xs, dt, bm, cm), lead, M_CHUNK)
    y = y + p['d'].astype(F32)[:, None] * xs
    y = y.reshape(Bsz, T, M_D_INNER) * jax.nn.silu(z.astype(F32))
    yg = y.reshape(Bsz, T, M_GROUPS, M_D_INNER // M_GROUPS)
    yg = yg * lax.rsqrt(jnp.mean(yg * yg, axis=-1, keepdims=True) + M_NORM_EPS)
    y = (yg.reshape(Bsz, T, M_D_INNER) * p['norm_g'].astype(F32)).astype(u.dtype)
    return y @ p['w_out'], S.astype(u.dtype), conv_new


def rwkv7_mixer(u, wkv, shift, p):
    Bsz, T, _ = u.shape
    prev = jnp.concatenate([shift[:, None].astype(u.dtype), u[:, :-1]], axis=1)
    dx = prev - u
    mu = p['mu']
    xr, xw, xk, xv, xa, xg = [u + dx * mu[i] for i in range(6)]
    r = xr @ p['w_r']
    k = xk @ p['w_k']
    v = xv @ p['w_v']
    w = -jax.nn.softplus(-(p['w0'] + jnp.tanh(xw @ p['w1']) @ p['w2']).astype(F32)) - 0.5
    decay = jnp.exp(-jnp.exp(w))
    a = jax.nn.sigmoid((p['a0'] + (xa @ p['a1']) @ p['a2']).astype(F32))
    g = jax.nn.sigmoid(xg @ p['g1']) @ p['g2']

    def heads(t):
        return t.astype(F32).reshape(Bsz, T, R_HEADS, R_HEADSIZE)

    kk = heads(k * p['k_k'])
    kk = kk / jnp.maximum(jnp.sqrt(jnp.sum(kk * kk, axis=-1, keepdims=True)), 1e-12)
    k = k.astype(F32) * (1.0 + (a - 1.0) * p['k_a'].astype(F32))
    r_h, k_h, v_h, a_h, w_h = heads(r), heads(k), heads(v), heads(a), heads(decay)

    def step(S, inp):
        r_t, w_t, k_t, v_t, kk_t, a_t = inp
        sa = jnp.einsum('bhvk,bhk->bhv', S, kk_t)
        S = S * w_t[:, :, None, :] - sa[..., None] * (kk_t * a_t)[:, :, None, :] + v_t[..., None] * k_t[:, :, None, :]
        return S, jnp.einsum('bhvk,bhk->bhv', S, r_t)

    seq = tuple(jnp.moveaxis(t, 1, 0) for t in (r_h, w_h, k_h, v_h, kk, a_h))
    S, y = lax.scan(step, wkv.astype(F32), seq)
    y = jnp.moveaxis(y, 0, 1)
    mean = jnp.mean(y, axis=-1, keepdims=True)
    var = jnp.mean(jnp.square(y - mean), axis=-1, keepdims=True)
    y = ((y - mean) * lax.rsqrt(var + R_LN_EPS)).reshape(Bsz, T, D_MODEL)
    y = y * p['ln_g'].astype(F32) + p['ln_b'].astype(F32)
    bonus = jnp.sum(r_h * k_h * p['r_k'].astype(F32), axis=-1, keepdims=True) * v_h
    y = (y + bonus.reshape(Bsz, T, D_MODEL)).astype(u.dtype)
    return (y * g) @ p['w_o'], S.astype(u.dtype), u[:, -1]


def gla_chunk(S, inp):
    q, logf, k, v = inp
    L = q.shape[1]
    G = jnp.cumsum(logf, axis=1)
    causal = jnp.tril(jnp.ones((L, L), dtype=bool))[None, :, :, None, None]
    dec = jnp.exp(jnp.where(causal, G[:, :, None] - G[:, None, :], -jnp.inf))
    att = jnp.einsum('btshk,bshk->bths', q[:, :, None] * dec, k)
    y = jnp.einsum('bths,bshv->bthv', att, v) + jnp.einsum('bthk,bhkv->bthv', q * jnp.exp(G), S)
    g_last = G[:, -1]
    S = jnp.exp(g_last)[..., None] * S + jnp.einsum('bshk,bshv->bhkv', k * jnp.exp(g_last[:, None] - G), v)
    return S, y


def hgrn2_mixer(u, st, lead, lb, p):
    Bsz, T, _ = u.shape
    proj = u @ p['w_in']
    q = jax.nn.silu(proj[..., :H_FDIM].astype(F32))
    fpre = proj[..., H_FDIM:2 * H_FDIM].astype(F32)
    i = proj[..., 2 * H_FDIM:2 * H_FDIM + D_MODEL].astype(F32)
    g = proj[..., 2 * H_FDIM + D_MODEL:]
    f = lb + (1.0 - lb) * jax.nn.sigmoid(fpre)
    logf = jnp.log(f)
    kin = 1.0 - f
    q, logf, kin = (t.reshape(Bsz, T, H_HEADS, H_EXPAND) for t in (q, logf, kin))
    i = i.reshape(Bsz, T, H_HEADS, H_VDIM)
    S, y = run_chunks(gla_chunk, st.astype(F32), (q, logf, kin, i), lead, H_CHUNK)
    y = y * lax.rsqrt(jnp.mean(y * y, axis=-1, keepdims=True) + EPS) * p['norm_g'].astype(F32)
    y = y.reshape(Bsz, T, D_MODEL) * jax.nn.silu(g.astype(F32))
    return y.astype(u.dtype) @ p['w_out'], S.astype(u.dtype)


def _layer_params(p, j):
    return {name: arr[j] for name, arr in p.items()}


def trunk(h, states, lead, norm_g, ffn_w_gate, ffn_w_up, ffn_w_down, pm, pr, ph, lb_all):
    st_ssm, st_conv, st_wkv, st_shift, st_hgrn = states
    new_ssm, new_conv, new_wkv, new_shift, new_hgrn = [], [], [], [], []
    for l in range(DEPTH):
        g = norm_g[l]
        y = swiglu(rmsnorm(h, g[0]), ffn_w_gate[l, 0], ffn_w_up[l, 0], ffn_w_down[l, 0])
        h = h + 0.5 * rmsnorm(y, g[1])
        u = rmsnorm(h, g[2])
        kind, j = l % N_MIXERS, l // N_MIXERS
        if kind == 0:
            y, s_new, c_new = mamba2_mixer(u, st_ssm[j], st_conv[j], lead, _layer_params(pm, j))
            new_ssm.append(s_new)
            new_conv.append(c_new)
        elif kind == 1:
            y, s_new, c_new = rwkv7_mixer(u, st_wkv[j], st_shift[j], _layer_params(pr, j))
            new_wkv.append(s_new)
            new_shift.append(c_new)
        else:
            y, s_new = hgrn2_mixer(u, st_hgrn[j], lead, lb_all[l], _layer_params(ph, j))
            new_hgrn.append(s_new)
        h = h + rmsnorm(y, g[3])
        y = swiglu(rmsnorm(h, g[4]), ffn_w_gate[l, 1], ffn_w_up[l, 1], ffn_w_down[l, 1])
        h = h + 0.5 * rmsnorm(y, g[5])
    return h, (jnp.stack(new_ssm), jnp.stack(new_conv), jnp.stack(new_wkv), jnp.stack(new_shift), jnp.stack(new_hgrn))


def setup_inputs(seed: int = 0) -> dict:
    key = jax.random.key(seed)
    keys = iter(jax.random.split(key, 64))

    def nrm(shape, scale):
        return jax.random.normal(next(keys), shape, jnp.float32) * scale

    def unif(shape, lo, hi):
        return jax.random.uniform(next(keys), shape, jnp.float32, lo, hi)

    dt0 = jnp.exp(unif((N_A, M_HEADS), math.log(1e-3), math.log(1e-1)))
    sd = D_MODEL ** -0.5
    return {
        "x_prompt": nrm((BATCH, SEQ, D_MODEL), 1.0),
        "x_sample": nrm((DEC_BATCH, DEC_SEQ, D_MODEL), 1.0),
        "state_ssm": nrm((N_A, DEC_BATCH, M_HEADS, M_HEADDIM, M_D_STATE), 0.1),
        "state_conv": nrm((N_A, DEC_BATCH, M_CONV - 1, M_CONV_DIM), 1.0),
        "state_wkv": nrm((N_B, DEC_BATCH, R_HEADS, R_HEADSIZE, R_HEADSIZE), 0.1),
        "state_shift": nrm((N_B, DEC_BATCH, D_MODEL), 1.0),
        "state_hgrn": nrm((N_C, DEC_BATCH, H_HEADS, H_EXPAND, H_VDIM), 0.1),
        "meta_tokens": nrm((N_META, D_MODEL), 1.0),
        "norm_g": 1.0 + nrm((DEPTH, 6, D_MODEL), 0.02),
        "ffn_w_gate": nrm((DEPTH, 2, D_MODEL, D_FF), sd),
        "ffn_w_up": nrm((DEPTH, 2, D_MODEL, D_FF), sd),
        "ffn_w_down": nrm((DEPTH, 2, D_FF, D_MODEL), D_FF ** -0.5),
        "m_w_in": nrm((N_A, D_MODEL, M_PROJ), sd),
        "m_conv_w": nrm((N_A, M_CONV, M_CONV_DIM), M_CONV ** -0.5),
        "m_conv_b": nrm((N_A, M_CONV_DIM), 0.02),
        "m_dt_bias": dt0 + jnp.log(-jnp.expm1(-dt0)),
        "m_a_log": jnp.log(unif((N_A, M_HEADS), 1.0, 16.0)),
        "m_d": 1.0 + nrm((N_A, M_HEADS), 0.02),
        "m_norm_g": 1.0 + nrm((N_A, M_D_INNER), 0.02),
        "m_w_out": nrm((N_A, M_D_INNER, D_MODEL), M_D_INNER ** -0.5),
        "r_mu": unif((N_B, 6, D_MODEL), 0.0, 1.0),
        "r_w_r": nrm((N_B, D_MODEL, D_MODEL), sd),
        "r_w_k": nrm((N_B, D_MODEL, D_MODEL), sd),
        "r_w_v": nrm((N_B, D_MODEL, D_MODEL), sd),
        "r_w0": unif((N_B, D_MODEL), -5.5, -0.5),
        "r_w1": nrm((N_B, D_MODEL, R_DECAY_LORA), sd),
        "r_w2": nrm((N_B, R_DECAY_LORA, D_MODEL), 0.1 * R_DECAY_LORA ** -0.5),
        "r_a0": nrm((N_B, D_MODEL), 0.1),
        "r_a1": nrm((N_B, D_MODEL, R_A_LORA), sd),
        "r_a2": nrm((N_B, R_A_LORA, D_MODEL), 0.1 * R_A_LORA ** -0.5),
        "r_g1": nrm((N_B, D_MODEL, R_GATE_LORA), sd),
        "r_g2": nrm((N_B, R_GATE_LORA, D_MODEL), R_GATE_LORA ** -0.5),
        "r_k_k": 0.85 + nrm((N_B, D_MODEL), 0.02),
        "r_k_a": 1.0 + nrm((N_B, D_MODEL), 0.02),
        "r_r_k": nrm((N_B, R_HEADS, R_HEADSIZE), 0.1),
        "r_ln_g": 1.0 + nrm((N_B, D_MODEL), 0.02),
        "r_ln_b": nrm((N_B, D_MODEL), 0.02),
        "r_w_o": nrm((N_B, D_MODEL, D_MODEL), sd),
        "h_w_in": nrm((N_C, D_MODEL, 2 * H_FDIM + 2 * D_MODEL), sd),
        "h_lb": nrm((DEPTH, H_FDIM), 0.1),
        "h_norm_g": 1.0 + nrm((N_C, H_VDIM), 0.02),
        "h_w_out": nrm((N_C, D_MODEL, D_MODEL), sd),
    }


def reference(x_prompt, x_sample, state_ssm, state_conv, state_wkv, state_shift, state_hgrn,
              meta_tokens, norm_g, ffn_w_gate, ffn_w_up, ffn_w_down,
              m_w_in, m_conv_w, m_conv_b, m_dt_bias, m_a_log, m_d, m_norm_g, m_w_out,
              r_mu, r_w_r, r_w_k, r_w_v, r_w0, r_w1, r_w2, r_a0, r_a1, r_a2, r_g1, r_g2,
              r_k_k, r_k_a, r_r_k, r_ln_g, r_ln_b, r_w_o,
              h_w_in, h_lb, h_norm_g, h_w_out):
    pm = {'w_in': m_w_in, 'conv_w': m_conv_w, 'conv_b': m_conv_b, 'dt_bias': m_dt_bias,
          'a_log': m_a_log, 'd': m_d, 'norm_g': m_norm_g, 'w_out': m_w_out}
    pr = {'mu': r_mu, 'w_r': r_w_r, 'w_k': r_w_k, 'w_v': r_w_v, 'w0': r_w0, 'w1': r_w1, 'w2': r_w2,
          'a0': r_a0, 'a1': r_a1, 'a2': r_a2, 'g1': r_g1, 'g2': r_g2, 'k_k': r_k_k, 'k_a': r_k_a,
          'r_k': r_r_k, 'ln_g': r_ln_g, 'ln_b': r_ln_b, 'w_o': r_w_o}
    ph = {'w_in': h_w_in, 'norm_g': h_norm_g, 'w_out': h_w_out}
    lb_all = jnp.cumsum(jax.nn.softmax(h_lb.astype(F32), axis=0), axis=0)
    lb_all = lb_all - lb_all[0]

    bp = x_prompt.shape[0]
    dtp = x_prompt.dtype
    meta = jnp.broadcast_to(meta_tokens[None].astype(dtp), (bp, N_META, D_MODEL))
    h0 = jnp.concatenate([meta, x_prompt], axis=1)
    zero_states = (jnp.zeros((N_A, bp, M_HEADS, M_HEADDIM, M_D_STATE), dtp),
                   jnp.zeros((N_A, bp, M_CONV - 1, M_CONV_DIM), dtp),
                   jnp.zeros((N_B, bp, R_HEADS, R_HEADSIZE, R_HEADSIZE), dtp),
                   jnp.zeros((N_B, bp, D_MODEL), dtp),
                   jnp.zeros((N_C, bp, H_HEADS, H_EXPAND, H_VDIM), dtp))
    hp, p_states = trunk(h0, zero_states, N_META, norm_g, ffn_w_gate, ffn_w_up, ffn_w_down, pm, pr, ph, lb_all)
    p_ssm, p_conv, p_wkv, p_shift, p_hgrn = p_states
    y_prompt = hp[:, N_META:]

    s_in = (state_ssm, state_conv, state_wkv, state_shift, state_hgrn)
    y_sample, s_states = trunk(x_sample, s_in, 0, norm_g, ffn_w_gate, ffn_w_up, ffn_w_down, pm, pr, ph, lb_all)
    s_ssm, s_conv, s_wkv, s_shift, s_hgrn = s_states
    return (y_prompt, y_sample, p_ssm, p_conv, p_wkv, p_shift, p_hgrn, s_ssm, s_conv, s_wkv, s_shift, s_hgrn)
```

```python
import functools
import math

import jax
import jax.numpy as jnp
from jax import lax
from jax.experimental import pallas as pl
from jax.experimental.pallas import tpu as pltpu

F32 = jnp.float32
BF16 = jnp.bfloat16

D_MODEL = 2048
BATCH = 4
SEQ = 2048
DEPTH = 4
DEC_BATCH = 128
DEC_SEQ = 8
N_META = 16
EPS = 1e-6
D_FF = 5632

M_D_INNER = 2 * D_MODEL
M_HEADDIM = 64
M_HEADS = M_D_INNER // M_HEADDIM
M_GROUPS = 8
M_D_STATE = 128
M_CONV = 4
M_CONV_DIM = M_D_INNER + 2 * M_GROUPS * M_D_STATE
M_NORM_EPS = 1e-5
M_GW = M_D_INNER // M_GROUPS
M_R = M_HEADS // M_GROUPS

R_HEADSIZE = 64
R_HEADS = D_MODEL // R_HEADSIZE
R_LN_EPS = 64e-5

H_HEADS = 16
H_EXPAND = 128
H_FDIM = H_HEADS * H_EXPAND
H_VDIM = D_MODEL // H_HEADS

T_P = N_META + SEQ
T_S = DEC_SEQ
M_P = BATCH * T_P
M_S = DEC_BATCH * T_S
M_TOK = M_P + M_S
S_BS = 8
LANES = 128
SUBLANES = 8
VMEM_LIMIT = 56 * 1024 * 1024

GROUPS = ((BATCH, T_P, 0, 1), (DEC_BATCH, T_S, M_P, S_BS))


def _cparams(n_axes):
    return pltpu.CompilerParams(dimension_semantics=("arbitrary",) * n_axes,
                                vmem_limit_bytes=VMEM_LIMIT)


def _silu(x):
    return x * (1.0 / (1.0 + jnp.exp(-x)))


def _sigmoid(x):
    return 1.0 / (1.0 + jnp.exp(-x))


def _softplus(x):
    return jnp.maximum(x, 0.0) + jnp.log1p(jnp.exp(-jnp.abs(x)))


def _iota(shape, dim):
    return lax.broadcasted_iota(jnp.int32, shape, dim)


def _dot(a, b):
    return jnp.dot(a.astype(BF16), b.astype(BF16), preferred_element_type=F32)


def _dot_nt(a, b):
    return lax.dot_general(a.astype(BF16), b.astype(BF16), (((1,), (1,)), ((), ())),
                           preferred_element_type=F32)


def _dot_tn(a, b):
    return lax.dot_general(a.astype(BF16), b.astype(BF16), (((0,), (0,)), ((), ())),
                           preferred_element_type=F32)


def _split(x, terms):
    parts = []
    rem = x
    for _ in range(terms):
        hi = rem.astype(BF16)
        parts.append(hi)
        rem = rem - hi.astype(F32)
    return parts


def _xdot(x, sel, terms=3):
    return sum(jnp.dot(p, sel, preferred_element_type=F32) for p in _split(x, terms))


def _xdot_left(sel, x, terms=3):
    return sum(jnp.dot(sel, p, preferred_element_type=F32) for p in _split(x, terms))


def _xdot_nt(sel, x, terms=3):
    return sum(lax.dot_general(sel, p, (((1,), (1,)), ((), ())), preferred_element_type=F32)
               for p in _split(x, terms))


def _tril(n, strict=False):
    r, c = _iota((n, n), 0), _iota((n, n), 1)
    return (r > c) if strict else (r >= c)


def _mm_body(x_ref, w_ref, o_ref, wbf_ref, *, act):
    @pl.when(pl.program_id(1) == 0)
    def _():
        wbf_ref[...] = w_ref[...].astype(BF16)

    acc = jnp.dot(x_ref[...].astype(BF16), wbf_ref[...], preferred_element_type=F32)
    if act == "tanh":
        acc = jnp.tanh(acc)
    elif act == "sigmoid":
        acc = _sigmoid(acc)
    o_ref[...] = acc.astype(o_ref.dtype)


def _mm(x, w, lead=(), col0=0, ncols=None, tn=512, tm=928, out_dtype=F32, act=None):
    m, k = x.shape
    ncols = w.shape[-1] if ncols is None else ncols
    tn = min(tn, ncols)
    assert ncols % tn == 0 and col0 % tn == 0 and m % tm == 0 and w.shape[-2] == k
    jb = col0 // tn
    wspec = pl.BlockSpec((None,) * len(lead) + (k, tn), lambda j, i: lead + (0, j + jb))
    return pl.pallas_call(
        functools.partial(_mm_body, act=act),
        grid=(ncols // tn, m // tm),
        in_specs=[pl.BlockSpec((tm, k), lambda j, i: (i, 0)), wspec],
        out_specs=pl.BlockSpec((tm, tn), lambda j, i: (i, j)),
        out_shape=jax.ShapeDtypeStruct((m, ncols), out_dtype),
        scratch_shapes=[pltpu.VMEM((k, tn), BF16)],
        compiler_params=_cparams(2),
    )(x, w)


def _swiglu_up_body(x_ref, wg_ref, wu_ref, o_ref, wg_bf, wu_bf):
    @pl.when(pl.program_id(1) == 0)
    def _():
        wg_bf[...] = wg_ref[...].astype(BF16)
        wu_bf[...] = wu_ref[...].astype(BF16)

    x = x_ref[...]
    g = jnp.dot(x, wg_bf[...], preferred_element_type=F32)
    u = jnp.dot(x, wu_bf[...], preferred_element_type=F32)
    o_ref[...] = (_silu(g) * u).astype(o_ref.dtype)


def _swiglu_up(x, w_gate, w_up, lead, tn=512, tm=928):
    m, k = x.shape
    n = w_gate.shape[-1]
    wspec = pl.BlockSpec((None,) * len(lead) + (k, tn), lambda j, i: lead + (0, j))
    return pl.pallas_call(
        _swiglu_up_body,
        grid=(n // tn, m // tm),
        in_specs=[pl.BlockSpec((tm, k), lambda j, i: (i, 0)), wspec, wspec],
        out_specs=pl.BlockSpec((tm, tn), lambda j, i: (i, j)),
        out_shape=jax.ShapeDtypeStruct((m, n), BF16),
        scratch_shapes=[pltpu.VMEM((k, tn), BF16), pltpu.VMEM((k, tn), BF16)],
        compiler_params=_cparams(2),
    )(x, w_gate, w_up)


def _rms(x, g):
    return x * lax.rsqrt(jnp.mean(x * x, axis=-1, keepdims=True) + EPS) * g


def _norm_body(h_ref, g_ref, u_ref, *, gi):
    u_ref[...] = _rms(h_ref[...], g_ref[gi:gi + 1, :]).astype(u_ref.dtype)


def _norm(h, norm_g, layer, gi, out_dtype=BF16, tm=464):
    m, d = h.shape
    return pl.pallas_call(
        functools.partial(_norm_body, gi=gi),
        grid=(m // tm,),
        in_specs=[pl.BlockSpec((tm, d), lambda i: (i, 0)),
                  pl.BlockSpec((None, 6, d), lambda i: (layer, 0, 0))],
        out_specs=pl.BlockSpec((tm, d), lambda i: (i, 0)),
        out_shape=jax.ShapeDtypeStruct((m, d), out_dtype),
        compiler_params=_cparams(1),
    )(h, norm_g)


def _resnorm_body(h_ref, y_ref, ga_ref, gb_ref, hn_ref, u_ref, *, scale, ia, ib):
    hn = h_ref[...] + scale * _rms(y_ref[...], ga_ref[ia:ia + 1, :])
    hn_ref[...] = hn
    if u_ref is not None:
        u_ref[...] = _rms(hn, gb_ref[ib:ib + 1, :]).astype(u_ref.dtype)


def _resnorm(h, y, norm_g, scale, la, ia, lb=None, ib=None, u_dtype=BF16, tm=464):
    m, d = h.shape
    row = pl.BlockSpec((tm, d), lambda i: (i, 0))
    with_u = lb is not None
    body = functools.partial(_resnorm_body, scale=scale, ia=ia, ib=ib)
    if not with_u:
        body = functools.partial(_resnorm_body, u_ref=None, scale=scale, ia=ia, ib=ib)
        lb = la
    out_shape = [jax.ShapeDtypeStruct((m, d), F32)]
    out_specs = [row]
    if with_u:
        out_shape.append(jax.ShapeDtypeStruct((m, d), u_dtype))
        out_specs.append(row)
    res = pl.pallas_call(
        body,
        grid=(m // tm,),
        in_specs=[row, row,
                  pl.BlockSpec((None, 6, d), lambda i: (la, 0, 0)),
                  pl.BlockSpec((None, 6, d), lambda i: (lb, 0, 0))],
        out_specs=out_specs,
        out_shape=out_shape,
        compiler_params=_cparams(1),
    )(h, y, norm_g, norm_g)
    return res if with_u else (res[0], None)


SSD_CHUNK = 128


def _chunks(t, chunk):
    out = [(c * chunk, chunk) for c in range(t // chunk)]
    if t % chunk:
        out.append((t - t % chunk, t % chunk))
    return out


def _conv_silu(cur, prv, w, b):
    n = cur.shape[0]
    row = _iota((SUBLANES, cur.shape[1]), 0)
    out = b + cur * w[M_CONV - 1:M_CONV]
    for k in range(1, M_CONV):
        sh = pltpu.roll(cur, k, 0)
        top = jnp.where(row < k, pltpu.roll(prv, k, 0), sh[:SUBLANES])
        sh = top if n == SUBLANES else jnp.concatenate([top, sh[SUBLANES:]], axis=0)
        out = out + sh * w[M_CONV - 1 - k:M_CONV - k]
    return _silu(out)


def _pad_rows(x, n):
    if x.shape[0] == n:
        return x
    return jnp.concatenate([x, jnp.zeros((n - x.shape[0],) + x.shape[1:], x.dtype)], axis=0)


def _ssd_chunk(nv, zc, xr, br, cr, dtr, prev, st, cw, cb, hp, ng):
    n = SSD_CHUNK
    px, pb, pc = prev
    cwx, cwb, cwc = cw
    cbx, cbb, cbc = cb
    new_prev = (xr[nv - SUBLANES:], br[nv - SUBLANES:], cr[nv - SUBLANES:])
    zc, xr, br, cr, dtr = (_pad_rows(v, n) for v in (zc, xr, br, cr, dtr))
    x = _conv_silu(xr, px, cwx, cbx)
    bm = _conv_silu(br, pb, cwb, cbb)
    cm = _conv_silu(cr, pc, cwc, cbc)

    bias, a, dsk = hp[0:1], -jnp.exp(hp[1:2]), hp[2:3]
    dt = _softplus(dtr + bias)
    if nv < n:
        dt = jnp.where(_iota((n, 1), 0) < nv, dt, 0.0)
    tril = _tril(n)
    cum = _xdot_left(tril.astype(BF16), dt * a)
    eye8 = (_iota((M_R, M_R), 0) == _iota((M_R, M_R), 1)).astype(BF16)
    cum_t = _xdot_nt(eye8, cum)
    dt_t = _xdot_nt(eye8, dt)
    tail = jnp.exp(cum[n - 1:n] - cum) * dt
    ecum = jnp.exp(cum)

    cbm = _dot_nt(cm, bm)
    even = (_iota((1, M_GW), 1) & M_HEADDIM) == 0
    rhs = jnp.concatenate([st, x], axis=0)
    rhs_e = jnp.where(even, rhs, 0.0).astype(BF16)
    rhs_o = jnp.where(even, 0.0, rhs).astype(BF16)
    x_e = rhs_e[M_D_STATE:]
    x_o = rhs_o[M_D_STATE:]

    y_parts, s_parts = [], []
    for q in range(M_R // 2):
        lo, hi = q * LANES, (q + 1) * LANES
        acc_y = None
        acc_s = None
        for r, rhs_m, x_m in ((2 * q, rhs_e, x_e), (2 * q + 1, rhs_o, x_o)):
            col = cum[:, r:r + 1]
            rw = cum_t[r:r + 1, :]
            dec = jnp.exp(jnp.where(tril, col - rw, -jnp.inf))
            sc = cbm * dec * dt_t[r:r + 1, :]
            lhs = jnp.concatenate([ecum[:, r:r + 1] * cm, sc], axis=1).astype(BF16)
            yy = jnp.dot(lhs, rhs_m[:, lo:hi], preferred_element_type=F32)
            acc_y = yy if acc_y is None else acc_y + yy
            ss = _dot_tn(bm * tail[:, r:r + 1], x_m[:, lo:hi])
            acc_s = ss if acc_s is None else acc_s + ss
        y_parts.append(acc_y)
        s_parts.append(acc_s)
    y = jnp.concatenate(y_parts, axis=1)
    s_new = jnp.concatenate(s_parts, axis=1)

    head_of_lane = _iota((M_R, M_GW), 1) >> (M_HEADDIM.bit_length() - 1)
    expand = (head_of_lane == _iota((M_R, M_GW), 0)).astype(BF16)
    row = _iota((SUBLANES, M_R), 0)
    rows8 = jnp.where(row == 0, ecum[n - 1:n], jnp.where(row == 1, dsk, 0.0))
    ex = _xdot(rows8, expand)
    st = st * ex[0:1] + s_new

    yv = (y + ex[1:2] * x) * _silu(zc)
    out = yv * lax.rsqrt(jnp.mean(yv * yv, axis=-1, keepdims=True) + M_NORM_EPS) * ng
    return out[:nv], new_prev, st


def _ssd_body(*refs, t, bs, has_state):
    if has_state:
        (z_ref, x_ref, b_ref, c_ref, dt_ref, cwx_ref, cwb_ref, cwc_ref, cbx_ref, cbb_ref, cbc_ref,
         hp_ref, ng_ref, csx_ref, csb_ref, csc_ref, sin_ref, y_ref, sout_ref) = refs
    else:
        (z_ref, x_ref, b_ref, c_ref, dt_ref, cwx_ref, cwb_ref, cwc_ref, cbx_ref, cbb_ref, cbc_ref,
         hp_ref, ng_ref, y_ref, sout_ref) = refs
    cw = (cwx_ref[...], cwb_ref[...], cwc_ref[...])
    cb = (cbx_ref[...], cbb_ref[...], cbc_ref[...])
    hp = hp_ref[...]
    ng = ng_ref[...]

    def seq(i, carry):
        base = pl.multiple_of(i * t, SUBLANES)
        if has_state:
            prev = (csx_ref[i], csb_ref[i], csc_ref[i])
            st = sin_ref[i].reshape(M_GW, M_D_STATE).T
        else:
            prev = (jnp.zeros((SUBLANES, M_GW), F32), jnp.zeros((SUBLANES, M_D_STATE), F32),
                    jnp.zeros((SUBLANES, M_D_STATE), F32))
            st = jnp.zeros((M_D_STATE, M_GW), F32)

        def run(start, n, prev, st):
            rows = pl.ds(pl.multiple_of(base + start, SUBLANES), n)
            out, prev, st = _ssd_chunk(n, z_ref[rows, :], x_ref[rows, :], b_ref[rows, :], c_ref[rows, :],
                                       dt_ref[rows, :], prev, st, cw, cb, hp, ng)
            y_ref[rows, :] = out.astype(y_ref.dtype)
            return prev, st

        n_full = t // SSD_CHUNK
        if n_full:
            def step(c, carry):
                return run(c * SSD_CHUNK, SSD_CHUNK, *carry)
            prev, st = lax.fori_loop(0, n_full, step, (prev, st))
        if t % SSD_CHUNK:
            prev, st = run(n_full * SSD_CHUNK, t % SSD_CHUNK, prev, st)
        sout_ref[i] = st.T.reshape(M_R, M_HEADDIM, M_D_STATE)
        return carry

    lax.fori_loop(0, bs, seq, 0)


def _ssd(j, z, xbc, dtg, m_conv_w, conv_b3, hp, m_norm_g3, group, conv_state=None, ssm_state=None):
    nseq, t, row0, bs = group
    rb = bs * t
    r0 = row0 // rb
    assert row0 % rb == 0 and nseq % bs == 0
    has_state = ssm_state is not None
    nxb = M_D_INNER // M_GW
    b0 = M_D_INNER // M_D_STATE
    c0 = b0 + M_GROUPS
    in_specs = [
        pl.BlockSpec((rb, M_GW), lambda s, g: (s + r0, g)),
        pl.BlockSpec((rb, M_GW), lambda s, g: (s + r0, g)),
        pl.BlockSpec((rb, M_D_STATE), lambda s, g: (s + r0, b0 + g)),
        pl.BlockSpec((rb, M_D_STATE), lambda s, g: (s + r0, c0 + g)),
        pl.BlockSpec((None, rb, M_R), lambda s, g: (g, s + r0, 0)),
        pl.BlockSpec((None, M_CONV, M_GW), lambda s, g: (j, 0, g)),
        pl.BlockSpec((None, M_CONV, M_D_STATE), lambda s, g: (j, 0, b0 + g)),
        pl.BlockSpec((None, M_CONV, M_D_STATE), lambda s, g: (j, 0, c0 + g)),
        pl.BlockSpec((None, 1, M_GW), lambda s, g: (j, 0, g)),
        pl.BlockSpec((None, 1, M_D_STATE), lambda s, g: (j, 0, b0 + g)),
        pl.BlockSpec((None, 1, M_D_STATE), lambda s, g: (j, 0, c0 + g)),
        pl.BlockSpec((None, None, 3, M_R), lambda s, g: (j, g, 0, 0)),
        pl.BlockSpec((None, 1, M_GW), lambda s, g: (j, 0, g)),
    ]
    args = [z, xbc, xbc, xbc, dtg, m_conv_w, m_conv_w, m_conv_w, conv_b3, conv_b3, conv_b3, hp, m_norm_g3]
    del nxb
    if has_state:
        in_specs += [
            pl.BlockSpec((None, bs, SUBLANES, M_GW), lambda s, g: (j, s, 0, g)),
            pl.BlockSpec((None, bs, SUBLANES, M_D_STATE), lambda s, g: (j, s, 0, b0 + g)),
            pl.BlockSpec((None, bs, SUBLANES, M_D_STATE), lambda s, g: (j, s, 0, c0 + g)),
            pl.BlockSpec((None, bs, M_R, M_HEADDIM, M_D_STATE), lambda s, g: (j, s, g, 0, 0)),
        ]
        args += [conv_state, conv_state, conv_state, ssm_state]
    return pl.pallas_call(
        functools.partial(_ssd_body, t=t, bs=bs, has_state=has_state),
        grid=(nseq // bs, M_GROUPS),
        in_specs=in_specs,
        out_specs=[pl.BlockSpec((rb, M_GW), lambda s, g: (s, g)),
                   pl.BlockSpec((bs, M_R, M_HEADDIM, M_D_STATE), lambda s, g: (s, g, 0, 0))],
        out_shape=[jax.ShapeDtypeStruct((nseq * t, M_D_INNER), BF16),
                   jax.ShapeDtypeStruct((nseq, M_HEADS, M_HEADDIM, M_D_STATE), F32)],
        compiler_params=_cparams(2),
    )(*args)


def _mamba_layer(j, u, m_w_in, m_conv_w, m_conv_b, m_dt_bias, m_a_log, m_d, m_norm_g, m_w_out,
                 state_conv_pad, state_ssm):
    z = _mm(u, m_w_in, (j,), 0, M_D_INNER)
    xbc = _mm(u, m_w_in, (j,), M_D_INNER, M_CONV_DIM)
    dt = _mm(u, m_w_in[:, :, M_D_INNER + M_CONV_DIM:], (j,))
    dtg = dt.reshape(M_TOK, M_GROUPS, M_R).transpose(1, 0, 2)
    hp = jnp.stack([m_dt_bias, m_a_log, m_d], axis=1).reshape(-1, 3, M_GROUPS, M_R).transpose(0, 2, 1, 3)
    conv_b3 = m_conv_b[:, None, :]
    ng3 = m_norm_g[:, None, :]
    yp, sp = _ssd(j, z, xbc, dtg, m_conv_w, conv_b3, hp, ng3, GROUPS[0])
    ys, ss = _ssd(j, z, xbc, dtg, m_conv_w, conv_b3, hp, ng3, GROUPS[1], state_conv_pad, state_ssm)
    y = _mm(jnp.concatenate([yp, ys], axis=0), m_w_out, (j,), tm=464)
    xp = xbc[:M_P].reshape(BATCH, T_P, M_CONV_DIM)[:, T_P - (M_CONV - 1):]
    xs = xbc[M_P:].reshape(DEC_BATCH, T_S, M_CONV_DIM)[:, T_S - (M_CONV - 1):]
    return y, (sp, xp), (ss, xs)


R_CHUNK = 64


def _rwkv_chunk(nv, r, kr, v, wl, al, g, sbd, prm):
    n = R_CHUNK
    r, kr, v, wl, al, g = (_pad_rows(t_, n) for t_ in (r, kr, v, wl, al, g))
    w0, a0, k_k, k_a, r_k, ln_g, ln_b = prm
    lane = _iota((1, LANES), 1)
    m0 = lane < R_HEADSIZE
    ones_blk = ((_iota((LANES, LANES), 0) < R_HEADSIZE) == (_iota((LANES, LANES), 1) < R_HEADSIZE)).astype(BF16)

    w = -_softplus(-(w0 + wl)) - 0.5
    lw = -jnp.exp(w)
    a = _sigmoid(a0 + al)
    kk = kr * k_k
    kk = kk / jnp.maximum(jnp.sqrt(_xdot(kk * kk, ones_blk)), 1e-12)
    k = kr * (1.0 + (a - 1.0) * k_a)
    if nv < n:
        valid = _iota((n, 1), 0) < nv
        lw = jnp.where(valid, lw, 0.0)
        kk = jnp.where(valid, kk, 0.0)
        k = jnp.where(valid, k, 0.0)
        v = jnp.where(valid, v, 0.0)
    beta = kk * a

    lam = _xdot_left(_tril(n).astype(BF16), lw)
    lam_n = lam[n - 1:n]
    e_in = jnp.exp(lam)
    e_out = jnp.exp(-lam)
    e_end = jnp.exp(lam_n - lam)
    abar = kk * jnp.exp(lam - lw)
    rbar = r * e_in
    khat = k * e_out
    bhat = beta * e_out

    def stack(x):
        return jnp.concatenate([jnp.where(m0, x, 0.0), jnp.where(m0, 0.0, x)], axis=0).astype(BF16)

    am, rm, khm, bhm = stack(abar), stack(rbar), stack(khat), stack(bhat)
    vst = stack(v)
    n2 = 2 * n
    tt = _iota((n2, n2), 0) & (n - 1)
    ii = _iota((n2, n2), 1) & (n - 1)
    strict, incl = tt > ii, tt >= ii
    a_ak = jnp.where(strict, _dot_nt(am, khm), 0.0)
    nmat = jnp.where(strict, -_dot_nt(am, bhm), 0.0)
    a_rk = jnp.where(incl, _dot_nt(rm, khm), 0.0)
    a_rb = jnp.where(incl, _dot_nt(rm, bhm), 0.0)

    eye = (_iota((n2, n2), 0) == _iota((n2, n2), 1)).astype(F32)
    tm = eye + nmat
    npow = nmat
    for _ in range(int(math.log2(n)) - 1):
        npow = _dot(npow, npow)
        tm = tm + _dot(tm, npow)

    sb = sbd.astype(BF16)
    rhs = _dot_nt(am, sb) + _dot(a_ak, vst)
    u = _dot(tm, rhs)
    yst = _dot_nt(rm, sb) + _dot(a_rk, vst) - _dot(a_rb, u)
    y = yst[:n] + yst[n:]

    ktm = stack(k * e_end)
    btm = stack(beta * e_end)
    sbd = (sbd * jnp.exp(lam_n)
           + _dot_tn(jnp.concatenate([vst, (-u).astype(BF16)], axis=0), jnp.concatenate([ktm, btm], axis=0)))

    inv = 1.0 / R_HEADSIZE
    mean = _xdot(y, ones_blk) * inv
    yc = y - mean
    var = _xdot(yc * yc, ones_blk) * inv
    yn = yc * lax.rsqrt(var + R_LN_EPS) * ln_g + ln_b
    bonus = _xdot(r * k * r_k, ones_blk) * v
    return ((yn + bonus) * g)[:nv], sbd


def _rwkv_body(*refs, t, bs, has_state):
    if has_state:
        (r_ref, k_ref, v_ref, wl_ref, al_ref, g_ref, prm_ref, sin_ref, y_ref, sout_ref) = refs
    else:
        (r_ref, k_ref, v_ref, wl_ref, al_ref, g_ref, prm_ref, y_ref, sout_ref) = refs
    p = prm_ref[...]
    prm = tuple(p[i:i + 1] for i in range(7))
    zeros = jnp.zeros((R_HEADSIZE, R_HEADSIZE), F32)

    def seq(i, carry):
        base = pl.multiple_of(i * t, SUBLANES)
        if has_state:
            s2 = sin_ref[i]
            sbd = jnp.concatenate([jnp.concatenate([s2[0], zeros], axis=1),
                                   jnp.concatenate([zeros, s2[1]], axis=1)], axis=0)
        else:
            sbd = jnp.zeros((LANES, LANES), F32)

        def run(start, n, sbd):
            rows = pl.ds(pl.multiple_of(base + start, SUBLANES), n)
            out, sbd = _rwkv_chunk(n, r_ref[rows, :], k_ref[rows, :], v_ref[rows, :], wl_ref[rows, :],
                                   al_ref[rows, :], g_ref[rows, :], sbd, prm)
            y_ref[rows, :] = out.astype(y_ref.dtype)
            return sbd

        n_full = t // R_CHUNK
        if n_full:
            sbd = lax.fori_loop(0, n_full, lambda c, s: run(c * R_CHUNK, R_CHUNK, s), sbd)
        if t % R_CHUNK:
            sbd = run(n_full * R_CHUNK, t % R_CHUNK, sbd)
        sout_ref[i, 0] = sbd[:R_HEADSIZE, :R_HEADSIZE]
        sout_ref[i, 1] = sbd[R_HEADSIZE:, R_HEADSIZE:]
        return carry

    lax.fori_loop(0, bs, seq, 0)


def _rwkv(r, k, v, wl, al, g, prm, group, wkv_state=None):
    nseq, t, row0, bs = group
    rb = bs * t
    r0 = row0 // rb
    has_state = wkv_state is not None
    tok = pl.BlockSpec((rb, LANES), lambda s, hp_: (s + r0, hp_))
    in_specs = [tok] * 6 + [pl.BlockSpec((SUBLANES, LANES), lambda s, hp_: (0, hp_))]
    args = [r, k, v, wl, al, g, prm]
    if has_state:
        in_specs.append(pl.BlockSpec((None, bs, 2, R_HEADSIZE, R_HEADSIZE), lambda s, hp_: (0, s, hp_, 0, 0)))
        args.append(wkv_state)
    return pl.pallas_call(
        functools.partial(_rwkv_body, t=t, bs=bs, has_state=has_state),
        grid=(nseq // bs, R_HEADS // 2),
        in_specs=in_specs,
        out_specs=[pl.BlockSpec((rb, LANES), lambda s, hp_: (s, hp_)),
                   pl.BlockSpec((bs, 2, R_HEADSIZE, R_HEADSIZE), lambda s, hp_: (s, hp_, 0, 0))],
        out_shape=[jax.ShapeDtypeStruct((nseq * t, D_MODEL), BF16),
                   jax.ShapeDtypeStruct((nseq, R_HEADS, R_HEADSIZE, R_HEADSIZE), F32)],
        compiler_params=_cparams(2),
    )(*args)


def _mix_body(u_ref, p_ref, mu_ref, *o_refs):
    u = u_ref[...]
    dx = p_ref[...] - u
    for i, o_ref in enumerate(o_refs):
        o_ref[...] = (u + dx * mu_ref[i:i + 1, :]).astype(o_ref.dtype)


def _rwkv_mix(u, prev, r_mu, tm=464):
    m, d = u.shape
    row = pl.BlockSpec((tm, d), lambda i: (i, 0))
    return pl.pallas_call(
        _mix_body,
        grid=(m // tm,),
        in_specs=[row, row, pl.BlockSpec((None, 6, d), lambda i: (0, 0, 0))],
        out_specs=[row] * 6,
        out_shape=[jax.ShapeDtypeStruct((m, d), BF16)] * 6,
        compiler_params=_cparams(1),
    )(u, prev, r_mu)


def _rwkv_layer(u, state_wkv, state_shift, r_mu, r_w_r, r_w_k, r_w_v, r_w0, r_w1, r_w2, r_a0, r_a1, r_a2,
                r_g1, r_g2, r_k_k, r_k_a, r_r_k, r_ln_g, r_ln_b, r_w_o):
    up = u[:M_P].reshape(BATCH, T_P, D_MODEL)
    us = u[M_P:].reshape(DEC_BATCH, T_S, D_MODEL)
    prev = jnp.concatenate([
        jnp.concatenate([jnp.zeros((BATCH, 1, D_MODEL), F32), up[:, :-1]], axis=1).reshape(M_P, D_MODEL),
        jnp.concatenate([state_shift[0][:, None, :], us[:, :-1]], axis=1).reshape(M_S, D_MODEL)], axis=0)
    xr, xw, xk, xv, xa, xg = _rwkv_mix(u, prev, r_mu)
    r = _mm(xr, r_w_r, (0,))
    k = _mm(xk, r_w_k, (0,))
    v = _mm(xv, r_w_v, (0,))
    wl = _mm(_mm(xw, r_w1, (0,), out_dtype=BF16, act="tanh"), r_w2, (0,))
    al = _mm(_mm(xa, r_a1, (0,), out_dtype=BF16), r_a2, (0,))
    g = _mm(_mm(xg, r_g1, (0,), out_dtype=BF16, act="sigmoid"), r_g2, (0,))
    prm = jnp.concatenate([r_w0, r_a0, r_k_k, r_k_a, r_r_k.reshape(1, D_MODEL), r_ln_g, r_ln_b,
                           jnp.zeros((1, D_MODEL), F32)], axis=0)
    yp, sp = _rwkv(r, k, v, wl, al, g, prm, GROUPS[0])
    ys, ss = _rwkv(r, k, v, wl, al, g, prm, GROUPS[1], state_wkv)
    y = _mm(jnp.concatenate([yp, ys], axis=0), r_w_o, (0,))
    return y, (sp, up[:, -1]), (ss, us[:, -1])


H_CHUNK = 128
H_SUB = 16


def _gla_chunk(nv, q, logf, k, v, st):
    n = -(-nv // H_SUB) * H_SUB
    q, logf, k, v = (_pad_rows(t_, n) for t_ in (q, logf, k, v))
    gcum = _xdot_left(_tril(n).astype(BF16), logf)
    y = _dot_nt(q * jnp.exp(gcum), st)
    g_n = gcum[n - 1:n]
    st = st * jnp.exp(g_n) + _dot_tn(v, k * jnp.exp(g_n - gcum))

    sub = H_SUB
    outs = []
    for i in range(n // sub):
        lo = i * sub
        qi, gi, ki, vi = q[lo:lo + sub], gcum[lo:lo + sub], k[lo:lo + sub], v[lo:lo + sub]
        yi = y[lo:lo + sub]
        if i > 0:
            gs = gcum[lo - 1:lo]
            att = _dot_nt(qi * jnp.exp(gi - gs), k[:lo] * jnp.exp(gs - gcum[:lo]))
            yi = yi + _dot(att, v[:lo])
        trow = _iota((sub, 1), 0)
        for s in range(sub):
            arg = jnp.where(trow >= s, gi - gi[s:s + 1], -jnp.inf)
            a_s = jnp.sum(qi * ki[s:s + 1] * jnp.exp(arg), axis=-1, keepdims=True)
            yi = yi + a_s * vi[s:s + 1]
        outs.append(yi)
    return (outs[0] if len(outs) == 1 else jnp.concatenate(outs, axis=0))[:nv], st


def _hgrn_body(*refs, t, bs, has_state, layer):
    if has_state:
        (q_ref, f_ref, i_ref, g_ref, lb_ref, ng_ref, sin_ref, y_ref, sout_ref) = refs
    else:
        (q_ref, f_ref, i_ref, g_ref, lb_ref, ng_ref, y_ref, sout_ref) = refs
    hl = lb_ref[...]
    e = jnp.exp(hl - jnp.max(hl, axis=0, keepdims=True))
    sm = e / jnp.sum(e, axis=0, keepdims=True)
    lb = jnp.sum(sm[1:layer + 1], axis=0, keepdims=True) if layer > 0 else jnp.zeros((1, LANES), F32)
    ng = ng_ref[...]

    def seq(i, carry):
        base = pl.multiple_of(i * t, SUBLANES)
        st = sin_ref[i].T if has_state else jnp.zeros((H_VDIM, H_EXPAND), F32)

        def run(start, n, st):
            rows = pl.ds(pl.multiple_of(base + start, SUBLANES), n)
            q = _silu(q_ref[rows, :])
            f = lb + (1.0 - lb) * _sigmoid(f_ref[rows, :])
            yy, st = _gla_chunk(n, q, jnp.log(f), 1.0 - f, i_ref[rows, :], st)
            yy = yy * lax.rsqrt(jnp.mean(yy * yy, axis=-1, keepdims=True) + EPS) * ng
            y_ref[rows, :] = (yy * _silu(g_ref[rows, :])).astype(y_ref.dtype)
            return st

        n_full = t // H_CHUNK
        if n_full:
            st = lax.fori_loop(0, n_full, lambda c, s: run(c * H_CHUNK, H_CHUNK, s), st)
        if t % H_CHUNK:
            st = run(n_full * H_CHUNK, t % H_CHUNK, st)
        sout_ref[i] = st.T
        return carry

    lax.fori_loop(0, bs, seq, 0)


def _hgrn(layer, proj, h_lb, h_norm_g, group, state=None):
    nseq, t, row0, bs = group
    rb = bs * t
    r0 = row0 // rb
    has_state = state is not None
    nh = H_HEADS

    def tok(off):
        return pl.BlockSpec((rb, LANES), lambda s, h: (s + r0, h + off))

    in_specs = [tok(0), tok(nh), tok(2 * nh), tok(3 * nh),
                pl.BlockSpec((DEPTH, LANES), lambda s, h: (0, h)),
                pl.BlockSpec((1, LANES), lambda s, h: (0, 0))]
    args = [proj, proj, proj, proj, h_lb, h_norm_g]
    if has_state:
        in_specs.append(pl.BlockSpec((None, bs, None, H_EXPAND, H_VDIM), lambda s, h: (0, s, h, 0, 0)))
        args.append(state)
    return pl.pallas_call(
        functools.partial(_hgrn_body, t=t, bs=bs, has_state=has_state, layer=layer),
        grid=(nseq // bs, nh),
        in_specs=in_specs,
        out_specs=[pl.BlockSpec((rb, LANES), lambda s, h: (s, h)),
                   pl.BlockSpec((bs, None, H_EXPAND, H_VDIM), lambda s, h: (s, h, 0, 0))],
        out_shape=[jax.ShapeDtypeStruct((nseq * t, D_MODEL), BF16),
                   jax.ShapeDtypeStruct((nseq, nh, H_EXPAND, H_VDIM), F32)],
        compiler_params=_cparams(2),
    )(*args)


def _hgrn_layer(layer, u, state_hgrn, h_w_in, h_lb, h_norm_g, h_w_out):
    proj = _mm(u, h_w_in, (0,))
    yp, sp = _hgrn(layer, proj, h_lb, h_norm_g, GROUPS[0])
    ys, ss = _hgrn(layer, proj, h_lb, h_norm_g, GROUPS[1], state_hgrn)
    y = _mm(jnp.concatenate([yp, ys], axis=0), h_w_out, (0,))
    return y, sp, ss


def kernel(x_prompt, x_sample, state_ssm, state_conv, state_wkv, state_shift, state_hgrn, meta_tokens, norm_g, ffn_w_gate, ffn_w_up, ffn_w_down, m_w_in, m_conv_w, m_conv_b, m_dt_bias, m_a_log, m_d, m_norm_g, m_w_out, r_mu, r_w_r, r_w_k, r_w_v, r_w0, r_w1, r_w2, r_a0, r_a1, r_a2, r_g1, r_g2, r_k_k, r_k_a, r_r_k, r_ln_g, r_ln_b, r_w_o, h_w_in, h_lb, h_norm_g, h_w_out):
    meta = jnp.broadcast_to(meta_tokens[None], (BATCH, N_META, D_MODEL))
    h = jnp.concatenate([jnp.concatenate([meta, x_prompt], axis=1).reshape(M_P, D_MODEL),
                         x_sample.reshape(M_S, D_MODEL)], axis=0)
    conv_pad = jnp.pad(state_conv, ((0, 0), (0, 0), (SUBLANES - (M_CONV - 1), 0), (0, 0)))

    p_ssm, p_conv, s_ssm, s_conv = [], [], [], []
    u = _norm(h, norm_g, 0, 0)
    for l in range(DEPTH):
        kind, j = l % 3, l // 3
        y = _mm(_swiglu_up(u, ffn_w_gate, ffn_w_up, (l, 0)), ffn_w_down, (l, 0), tm=464)
        h, u = _resnorm(h, y, norm_g, 0.5, l, 1, l, 2, u_dtype=F32 if kind == 1 else BF16)
        if kind == 0:
            y, (sp, cp), (ss, cs) = _mamba_layer(j, u, m_w_in, m_conv_w, m_conv_b, m_dt_bias, m_a_log, m_d,
                                                 m_norm_g, m_w_out, conv_pad, state_ssm)
            p_ssm.append(sp)
            p_conv.append(cp)
            s_ssm.append(ss)
            s_conv.append(cs)
        elif kind == 1:
            y, (p_wkv, p_shift), (s_wkv, s_shift) = _rwkv_layer(
                u, state_wkv, state_shift, r_mu, r_w_r, r_w_k, r_w_v, r_w0, r_w1, r_w2, r_a0, r_a1, r_a2,
                r_g1, r_g2, r_k_k, r_k_a, r_r_k, r_ln_g, r_ln_b, r_w_o)
        else:
            y, p_hgrn, s_hgrn = _hgrn_layer(l, u, state_hgrn, h_w_in, h_lb, h_norm_g, h_w_out)
        h, u = _resnorm(h, y, norm_g, 1.0, l, 3, l, 4)
        y = _mm(_swiglu_up(u, ffn_w_gate, ffn_w_up, (l, 1)), ffn_w_down, (l, 1), tm=464)
        if l + 1 < DEPTH:
            h, u = _resnorm(h, y, norm_g, 0.5, l, 5, l + 1, 0)
        else:
            h, _ = _resnorm(h, y, norm_g, 0.5, l, 5)

    y_prompt = h[:M_P].reshape(BATCH, T_P, D_MODEL)[:, N_META:]
    y_sample = h[M_P:].reshape(DEC_BATCH, T_S, D_MODEL)
    return (y_prompt, y_sample,
            jnp.stack(p_ssm), jnp.stack(p_conv), p_wkv[None], p_shift[None], p_hgrn[None],
            jnp.stack(s_ssm), jnp.stack(s_conv), s_wkv[None], s_shift[None], s_hgrn[None])
```

```python
import functools

import jax
import jax.numpy as jnp
from jax import lax
from jax.experimental import pallas as pl
from jax.experimental.pallas import tpu as pltpu

F32 = jnp.float32
BF16 = jnp.bfloat16

D_MODEL = 2048
BATCH = 4
SEQ = 2048
DEPTH = 4
DEC_BATCH = 128
DEC_SEQ = 8
N_META = 16
EPS = 1e-6
D_FF = 5632

M_D_INNER = 2 * D_MODEL
M_HEADDIM = 64
M_HEADS = M_D_INNER // M_HEADDIM
M_GROUPS = 8
M_D_STATE = 128
M_CONV = 4
M_CONV_DIM = M_D_INNER + 2 * M_GROUPS * M_D_STATE
M_NORM_EPS = 1e-5
M_GW = M_D_INNER // M_GROUPS
M_R = M_HEADS // M_GROUPS

R_HEADSIZE = 64
R_HEADS = D_MODEL // R_HEADSIZE
R_LN_EPS = 64e-5

H_HEADS = 16
H_EXPAND = 128
H_FDIM = H_HEADS * H_EXPAND
H_VDIM = D_MODEL // H_HEADS

T_P = N_META + SEQ
T_S = DEC_SEQ
M_P = BATCH * T_P
M_S = DEC_BATCH * T_S
M_TOK = M_P + M_S
S_BS = 8
LANES = 128
SUBLANES = 8
BF16_ROWS = 16
VMEM_LIMIT = 56 * 1024 * 1024

GROUPS = ((BATCH, T_P, 0, 1), (DEC_BATCH, T_S, M_P, S_BS))


def _cparams(n_axes):
    return pltpu.CompilerParams(dimension_semantics=("arbitrary",) * n_axes,
                                vmem_limit_bytes=VMEM_LIMIT)


def _silu(x):
    return x * (1.0 / (1.0 + jnp.exp(-x)))


def _sigmoid(x):
    return 1.0 / (1.0 + jnp.exp(-x))


def _softplus(x):
    return jnp.maximum(x, 0.0) + jnp.log1p(jnp.exp(-jnp.abs(x)))


def _iota(shape, dim):
    return lax.broadcasted_iota(jnp.int32, shape, dim)


def _log2(n):
    assert n & (n - 1) == 0
    return n.bit_length() - 1


def _dot(a, b):
    return jnp.dot(a.astype(BF16), b.astype(BF16), preferred_element_type=F32)


def _dot_nt(a, b):
    return lax.dot_general(a.astype(BF16), b.astype(BF16), (((1,), (1,)), ((), ())),
                           preferred_element_type=F32)


def _dot_tn(a, b):
    return lax.dot_general(a.astype(BF16), b.astype(BF16), (((0,), (0,)), ((), ())),
                           preferred_element_type=F32)


def _split(x, terms):
    parts = []
    rem = x
    for _ in range(terms):
        hi = rem.astype(BF16)
        parts.append(hi)
        rem = rem - hi.astype(F32)
    return parts


def _xdot(x, sel, terms=3):
    return sum(jnp.dot(p, sel, preferred_element_type=F32) for p in _split(x, terms))


def _xdot_left(sel, x, terms=3):
    return sum(jnp.dot(sel, p, preferred_element_type=F32) for p in _split(x, terms))


def _xdot_nt(sel, x, terms=3):
    return sum(lax.dot_general(sel, p, (((1,), (1,)), ((), ())), preferred_element_type=F32)
               for p in _split(x, terms))


def _pad_rows(x, n):
    if x.shape[0] >= n:
        return x
    return jnp.concatenate([x, jnp.zeros((n - x.shape[0],) + x.shape[1:], x.dtype)], axis=0)


def _seq_masks(ntok, n):
    r, c = _iota((ntok, ntok), 0), _iota((ntok, ntok), 1)
    same = (r >> _log2(n)) == (c >> _log2(n))
    return same, same & (r >= c)


def _lockstep(gens):
    results = [None] * len(gens)
    live = list(enumerate(gens))
    while live:
        still = []
        for i, gen in live:
            try:
                next(gen)
                still.append((i, gen))
            except StopIteration as done:
                results[i] = done.value
        live = still
    return results


def _row_tile(m, want):
    return next(d for d in range(want - want % BF16_ROWS, 0, -BF16_ROWS) if m % d == 0)


def _single(mode):
    return dict(pipeline_mode=pl.Buffered(1)) if mode else {}


def _mm_body(x_ref, w_ref, o_ref, wbf_ref, *, act):
    @pl.when(pl.program_id(1) == 0)
    def _():
        wbf_ref[...] = w_ref[...].astype(BF16)

    acc = jnp.dot(x_ref[...].astype(BF16), wbf_ref[...], preferred_element_type=F32)
    if act == "tanh":
        acc = jnp.tanh(acc)
    elif act == "sigmoid":
        acc = _sigmoid(acc)
    o_ref[...] = acc.astype(o_ref.dtype)


def _mm(x, w, lead=(), col0=0, ncols=None, tn=512, tm=928, out_dtype=F32, act=None, name="mm"):
    m, k = x.shape
    ncols = w.shape[-1] if ncols is None else ncols
    tn = min(tn, ncols)
    tm = _row_tile(m, tm)
    assert ncols % tn == 0 and col0 % tn == 0 and m % tm == 0 and w.shape[-2] == k
    jb = col0 // tn
    wspec = pl.BlockSpec((None,) * len(lead) + (k, tn), lambda j, i: lead + (0, j + jb))
    return pl.pallas_call(
        functools.partial(_mm_body, act=act),
        grid=(ncols // tn, m // tm),
        in_specs=[pl.BlockSpec((tm, k), lambda j, i: (i, 0)), wspec],
        out_specs=pl.BlockSpec((tm, tn), lambda j, i: (i, j)),
        out_shape=jax.ShapeDtypeStruct((m, ncols), out_dtype),
        scratch_shapes=[pltpu.VMEM((k, tn), BF16)],
        compiler_params=_cparams(2),
        name=name,
    )(x, w)


def _swiglu_up_body(x_ref, wg_ref, wu_ref, o_ref, wg_bf, wu_bf):
    @pl.when(pl.program_id(1) == 0)
    def _():
        wg_bf[...] = wg_ref[...].astype(BF16)
        wu_bf[...] = wu_ref[...].astype(BF16)

    x = x_ref[...]
    g = jnp.dot(x, wg_bf[...], preferred_element_type=F32)
    u = jnp.dot(x, wu_bf[...], preferred_element_type=F32)
    o_ref[...] = (_silu(g) * u).astype(o_ref.dtype)


def _swiglu_up(x, w_gate, w_up, lead, tn=512, tm=928):
    m, k = x.shape
    n = w_gate.shape[-1]
    tm = _row_tile(m, tm)
    wspec = pl.BlockSpec((None,) * len(lead) + (k, tn), lambda j, i: lead + (0, j))
    return pl.pallas_call(
        _swiglu_up_body,
        grid=(n // tn, m // tm),
        in_specs=[pl.BlockSpec((tm, k), lambda j, i: (i, 0)), wspec, wspec],
        out_specs=pl.BlockSpec((tm, tn), lambda j, i: (i, j)),
        out_shape=jax.ShapeDtypeStruct((m, n), BF16),
        scratch_shapes=[pltpu.VMEM((k, tn), BF16), pltpu.VMEM((k, tn), BF16)],
        compiler_params=_cparams(2),
        name="swiglu_up",
    )(x, w_gate, w_up)


def _rms(x, g):
    return x * lax.rsqrt(jnp.mean(x * x, axis=-1, keepdims=True) + EPS) * g


def _norm_body(h_ref, g_ref, u_ref, *, gi):
    u_ref[...] = _rms(h_ref[...], g_ref[gi:gi + 1, :]).astype(u_ref.dtype)


def _norm(h, norm_g, layer, gi, out_dtype=BF16, tm=464):
    m, d = h.shape
    tm = _row_tile(m, tm)
    return pl.pallas_call(
        functools.partial(_norm_body, gi=gi),
        grid=(m // tm,),
        in_specs=[pl.BlockSpec((tm, d), lambda i: (i, 0)),
                  pl.BlockSpec((None, 6, d), lambda i: (layer, 0, 0))],
        out_specs=pl.BlockSpec((tm, d), lambda i: (i, 0)),
        out_shape=jax.ShapeDtypeStruct((m, d), out_dtype),
        compiler_params=_cparams(1),
        name="norm",
    )(h, norm_g)


def _resnorm_body(h_ref, y_ref, ga_ref, gb_ref, hn_ref, u_ref, *, scale, ia, ib):
    hn = h_ref[...] + scale * _rms(y_ref[...], ga_ref[ia:ia + 1, :])
    hn_ref[...] = hn
    if u_ref is not None:
        u_ref[...] = _rms(hn, gb_ref[ib:ib + 1, :]).astype(u_ref.dtype)


def _resnorm(h, y, norm_g, scale, la, ia, lb=None, ib=None, u_dtype=BF16, tm=464):
    m, d = h.shape
    tm = _row_tile(m, tm)
    row = pl.BlockSpec((tm, d), lambda i: (i, 0))
    with_u = lb is not None
    body = functools.partial(_resnorm_body, scale=scale, ia=ia, ib=ib)
    if not with_u:
        body = functools.partial(_resnorm_body, u_ref=None, scale=scale, ia=ia, ib=ib)
        lb = la
    out_shape = [jax.ShapeDtypeStruct((m, d), F32)]
    out_specs = [row]
    if with_u:
        out_shape.append(jax.ShapeDtypeStruct((m, d), u_dtype))
        out_specs.append(row)
    res = pl.pallas_call(
        body,
        grid=(m // tm,),
        in_specs=[row, row,
                  pl.BlockSpec((None, 6, d), lambda i: (la, 0, 0)),
                  pl.BlockSpec((None, 6, d), lambda i: (lb, 0, 0))],
        out_specs=out_specs,
        out_shape=out_shape,
        compiler_params=_cparams(1),
        name="resnorm",
    )(h, y, norm_g, norm_g)
    return res if with_u else (res[0], None)


SSD_CHUNK = 128
SSD_GPS = 2


def _conv_taps(cur, shifted, w, b):
    out = b + cur * w[M_CONV - 1:M_CONV]
    for k in range(1, M_CONV):
        out = out + shifted(k) * w[M_CONV - 1 - k:M_CONV - k]
    return _silu(out)


def _conv_silu_seq(cur, prv, w, b):
    row = _iota((SUBLANES, cur.shape[1]), 0)

    def shifted(k):
        sh = pltpu.roll(cur, k, 0)
        top = jnp.where(row < k, pltpu.roll(prv, k, 0), sh[:SUBLANES])
        return jnp.concatenate([top, sh[SUBLANES:]], axis=0)

    return _conv_taps(cur, shifted, w, b)


def _conv_silu_batch(cur, stt, w, b):
    ntok = cur.shape[0]
    tpos = _iota((ntok, 1), 0) & (SUBLANES - 1)

    def shifted(k):
        return jnp.where(tpos < k, pltpu.roll(stt, (k - SUBLANES) % ntok, 0), pltpu.roll(cur, k, 0))

    return _conv_taps(cur, shifted, w, b)


def _expand_heads(rows):
    head_of_lane = _iota((M_R, M_GW), 1) >> _log2(M_HEADDIM)
    sel = (head_of_lane == _iota((M_R, M_GW), 0)).astype(BF16)
    return _xdot(rows, sel, terms=2)


def _ssd_common(x, bm, cm, dtr, hp, causal, same, nv, extra_rows):
    ntok = x.shape[0]
    bias, a = hp[0:1], -jnp.exp(hp[1:2])
    dt = _softplus(dtr + bias)
    if nv < ntok and same is None:
        dt = jnp.where(_iota((ntok, 1), 0) < nv, dt, 0.0)
    da = dt * a
    cum = _xdot_left(causal.astype(BF16), da)
    cum_tot = cum[ntok - 1:ntok] if same is None else _xdot_left(same.astype(BF16), da)
    cbm = _dot_nt(cm, bm)
    eye8 = (_iota((M_R, M_R), 0) == _iota((M_R, M_R), 1)).astype(BF16)
    dt_t = _xdot_nt(eye8, dt)
    even = (_iota((1, M_GW), 1) & M_HEADDIM) == 0
    x_e = jnp.where(even, x, 0.0).astype(BF16)
    x_o = jnp.where(even, 0.0, x).astype(BF16)
    yield
    cum_t = _xdot_nt(eye8, cum)
    tail = jnp.exp(cum_tot - cum) * dt
    ex = _expand_heads(jnp.concatenate([jnp.exp(cum), tail, extra_rows(cum_tot)], axis=0))
    yield
    parts = []
    for q in range(M_R // 2):
        acc = None
        for r, x_m in ((2 * q, x_e), (2 * q + 1, x_o)):
            dec = jnp.exp(jnp.where(causal, cum[:, r:r + 1] - cum_t[r:r + 1, :], -jnp.inf))
            sc = (cbm * dec * dt_t[r:r + 1, :]).astype(BF16)
            yy = jnp.dot(sc, x_m[:, q * LANES:(q + 1) * LANES], preferred_element_type=F32)
            acc = yy if acc is None else acc + yy
        parts.append(acc)
    return jnp.concatenate(parts, axis=1), ex, cum_tot


def _ssd_finish(y, x, zc, d_row, ng):
    yv = (y + d_row * x) * _silu(zc)
    return yv * lax.rsqrt(jnp.mean(yv * yv, axis=-1, keepdims=True) + M_NORM_EPS) * ng


def _ssd_prompt_chunk(nv, zc, xr, br, cr, dtr, prev, st, cw, cb, hp, ng):
    n = SSD_CHUNK
    new_prev = (xr[nv - SUBLANES:], br[nv - SUBLANES:], cr[nv - SUBLANES:])
    zc, xr, br, cr, dtr = (_pad_rows(v, n) for v in (zc, xr, br, cr, dtr))
    x = _conv_silu_seq(xr, prev[0], cw[0], cb[0])
    bm = _conv_silu_seq(br, prev[1], cw[1], cb[1])
    cm = _conv_silu_seq(cr, prev[2], cw[2], cb[2])
    causal = _iota((n, n), 0) >= _iota((n, n), 1)
    row = _iota((SUBLANES, M_R), 0)

    def extra_rows(cum_tot):
        return jnp.where(row == 0, jnp.exp(cum_tot), jnp.where(row == 1, hp[2:3], 0.0))

    y_state = _dot(cm, st)
    y, ex, _ = yield from _ssd_common(x, bm, cm, dtr, hp, causal, None, nv, extra_rows)
    upd = _dot_tn(bm, ex[n:2 * n] * x)
    yield
    y = y + y_state * ex[:n]
    st = st * ex[2 * n:2 * n + 1] + upd
    return _ssd_finish(y, x, zc, ex[2 * n + 1:2 * n + 2], ng)[:nv], new_prev, st


def _ssd_sample_chunk(nb, zc, xr, br, cr, dtr, stt, states, cw, cb, hp, ng):
    n = SUBLANES
    ntok = nb * n
    x = _conv_silu_batch(xr, stt[0], cw[0], cb[0])
    bm = _conv_silu_batch(br, stt[1], cw[1], cb[1])
    cm = _conv_silu_batch(cr, stt[2], cw[2], cb[2])
    same, causal = _seq_masks(ntok, n)
    row = _iota((SUBLANES, M_R), 0)
    ys = [_dot_nt(_pad_rows(cm[b * n:(b + 1) * n], BF16_ROWS), states[b])[:n] for b in range(nb)]
    y, ex, cum_tot = yield from _ssd_common(x, bm, cm, dtr, hp, causal, same, n,
                                            lambda _: jnp.where(row == 0, hp[2:3], 0.0))
    xt = ex[ntok:2 * ntok] * x
    upds = [_dot_tn(_pad_rows(xt[b * n:(b + 1) * n], BF16_ROWS), _pad_rows(bm[b * n:(b + 1) * n], BF16_ROWS))
            for b in range(nb)]
    yield
    etot = jnp.exp(cum_tot)
    new_states = []
    for b in range(nb):
        decay = jnp.concatenate([jnp.broadcast_to(etot[b * n:b * n + 1, r:r + 1], (M_HEADDIM, M_D_STATE))
                                 for r in range(M_R)], axis=0)
        new_states.append(states[b] * decay + upds[b])
    y = y + jnp.concatenate(ys, axis=0) * ex[:ntok]
    return _ssd_finish(y, x, zc, ex[2 * ntok:2 * ntok + 1], ng), new_states


def _ssd_body(*refs, t, nb, has_state):
    if has_state:
        (z_ref, x_ref, b_ref, c_ref, dt_ref, cwx_ref, cwb_ref, cwc_ref, cbx_ref, cbb_ref, cbc_ref,
         hp_ref, ng_ref, csx_ref, csb_ref, csc_ref, sin_ref, y_ref, sout_ref) = refs
    else:
        (z_ref, x_ref, b_ref, c_ref, dt_ref, cwx_ref, cwb_ref, cwc_ref, cbx_ref, cbb_ref, cbc_ref,
         hp_ref, ng_ref, y_ref, sout_ref) = refs

    def consts(gi):
        wide = slice(gi * M_GW, (gi + 1) * M_GW)
        nar = slice(gi * M_D_STATE, (gi + 1) * M_D_STATE)
        cw = (cwx_ref[:, wide], cwb_ref[:, nar], cwc_ref[:, nar])
        cb = (cbx_ref[:, wide], cbb_ref[:, nar], cbc_ref[:, nar])
        return wide, nar, cw, cb, hp_ref[gi], ng_ref[:, wide]

    if has_state:
        ntok = nb * t
        gens = []
        for gi in range(SSD_GPS):
            wide, nar, cw, cb, hp, ng = consts(gi)
            stt = (csx_ref[:, :, wide].reshape(ntok, M_GW), csb_ref[:, :, nar].reshape(ntok, M_D_STATE),
                   csc_ref[:, :, nar].reshape(ntok, M_D_STATE))
            states = [sin_ref[b, gi * M_R:(gi + 1) * M_R].reshape(M_GW, M_D_STATE) for b in range(nb)]
            gens.append(_ssd_sample_chunk(nb, z_ref[:, wide], x_ref[:, wide], b_ref[:, nar], c_ref[:, nar],
                                          dt_ref[gi], stt, states, cw, cb, hp, ng))
        for gi, (out, states) in enumerate(_lockstep(gens)):
            y_ref[:, gi * M_GW:(gi + 1) * M_GW] = out.astype(y_ref.dtype)
            for b in range(nb):
                sout_ref[b, gi * M_R:(gi + 1) * M_R] = states[b].reshape(M_R, M_HEADDIM, M_D_STATE)
        return

    def run(start, nv, carry):
        rows = pl.ds(pl.multiple_of(start, SUBLANES), nv)
        gens = []
        for gi in range(SSD_GPS):
            wide, nar, cw, cb, hp, ng = consts(gi)
            prev, st = carry[gi]
            gens.append(_ssd_prompt_chunk(nv, z_ref[rows, wide], x_ref[rows, wide], b_ref[rows, nar],
                                          c_ref[rows, nar], dt_ref[gi, rows, :], prev, st, cw, cb, hp, ng))
        new = []
        for gi, (out, prev, st) in enumerate(_lockstep(gens)):
            y_ref[rows, gi * M_GW:(gi + 1) * M_GW] = out.astype(y_ref.dtype)
            new.append((prev, st))
        return tuple(new)

    zero_prev = (jnp.zeros((SUBLANES, M_GW), F32), jnp.zeros((SUBLANES, M_D_STATE), F32),
                 jnp.zeros((SUBLANES, M_D_STATE), F32))
    carry = tuple((zero_prev, jnp.zeros((M_D_STATE, M_GW), F32)) for _ in range(SSD_GPS))
    n_full = t // SSD_CHUNK
    carry = lax.fori_loop(0, n_full, lambda c, cr: run(c * SSD_CHUNK, SSD_CHUNK, cr), carry)
    if t % SSD_CHUNK:
        carry = run(n_full * SSD_CHUNK, t % SSD_CHUNK, carry)
    for gi in range(SSD_GPS):
        sout_ref[0, gi * M_R:(gi + 1) * M_R] = carry[gi][1].T.reshape(M_R, M_HEADDIM, M_D_STATE)


def _ssd(j, z, xbc, dtg, m_conv_w, conv_b3, hp, m_norm_g3, group, conv_state=None, ssm_state=None):
    nseq, t, row0, nb = group
    rb = nb * t
    r0 = row0 // rb
    assert row0 % rb == 0 and nseq % nb == 0 and M_GROUPS % SSD_GPS == 0
    has_state = ssm_state is not None
    gw, gn, gr = SSD_GPS * M_GW, SSD_GPS * M_D_STATE, SSD_GPS * M_R
    b0 = M_D_INNER // gn
    c0 = b0 + M_GROUPS // SSD_GPS
    one = _single(not has_state)
    in_specs = [
        pl.BlockSpec((rb, gw), lambda s, g: (s + r0, g), **one),
        pl.BlockSpec((rb, gw), lambda s, g: (s + r0, g), **one),
        pl.BlockSpec((rb, gn), lambda s, g: (s + r0, b0 + g), **one),
        pl.BlockSpec((rb, gn), lambda s, g: (s + r0, c0 + g), **one),
        pl.BlockSpec((SSD_GPS, rb, M_R), lambda s, g: (g, s + r0, 0), **one),
        pl.BlockSpec((None, M_CONV, gw), lambda s, g: (j, 0, g)),
        pl.BlockSpec((None, M_CONV, gn), lambda s, g: (j, 0, b0 + g)),
        pl.BlockSpec((None, M_CONV, gn), lambda s, g: (j, 0, c0 + g)),
        pl.BlockSpec((None, 1, gw), lambda s, g: (j, 0, g)),
        pl.BlockSpec((None, 1, gn), lambda s, g: (j, 0, b0 + g)),
        pl.BlockSpec((None, 1, gn), lambda s, g: (j, 0, c0 + g)),
        pl.BlockSpec((None, SSD_GPS, 3, M_R), lambda s, g: (j, g, 0, 0)),
        pl.BlockSpec((None, 1, gw), lambda s, g: (j, 0, g)),
    ]
    args = [z, xbc, xbc, xbc, dtg, m_conv_w, m_conv_w, m_conv_w, conv_b3, conv_b3, conv_b3, hp, m_norm_g3]
    if has_state:
        in_specs += [
            pl.BlockSpec((None, nb, SUBLANES, gw), lambda s, g: (j, s, 0, g)),
            pl.BlockSpec((None, nb, SUBLANES, gn), lambda s, g: (j, s, 0, b0 + g)),
            pl.BlockSpec((None, nb, SUBLANES, gn), lambda s, g: (j, s, 0, c0 + g)),
            pl.BlockSpec((None, nb, gr, M_HEADDIM, M_D_STATE), lambda s, g: (j, s, g, 0, 0)),
        ]
        args += [conv_state, conv_state, conv_state, ssm_state]
    return pl.pallas_call(
        functools.partial(_ssd_body, t=t, nb=nb, has_state=has_state),
        grid=(nseq // nb, M_GROUPS // SSD_GPS),
        in_specs=in_specs,
        out_specs=[pl.BlockSpec((rb, gw), lambda s, g: (s, g)),
                   pl.BlockSpec((nb, gr, M_HEADDIM, M_D_STATE), lambda s, g: (s, g, 0, 0))],
        out_shape=[jax.ShapeDtypeStruct((nseq * t, M_D_INNER), BF16),
                   jax.ShapeDtypeStruct((nseq, M_HEADS, M_HEADDIM, M_D_STATE), F32)],
        compiler_params=_cparams(2),
        name="ssd_sample" if has_state else "ssd_prompt",
    )(*args)


def _mamba_layer(j, u, groups, m_w_in, m_conv_w, m_conv_b, m_dt_bias, m_a_log, m_d, m_norm_g, m_w_out,
                 state_conv_pad, state_ssm):
    m = u.shape[0]
    gp, gs = groups
    z = _mm(u, m_w_in, (j,), 0, M_D_INNER, name="ssd_in_z")
    xbc = _mm(u, m_w_in, (j,), M_D_INNER, M_CONV_DIM, name="ssd_in_xbc")
    dt = _mm(u, m_w_in[:, :, M_D_INNER + M_CONV_DIM:], (j,), name="ssd_in_dt")
    dtg = dt.reshape(m, M_GROUPS, M_R).transpose(1, 0, 2)
    hp = jnp.stack([m_dt_bias, m_a_log, m_d], axis=1).reshape(-1, 3, M_GROUPS, M_R).transpose(0, 2, 1, 3)
    conv_b3 = m_conv_b[:, None, :]
    ng3 = m_norm_g[:, None, :]
    yp, sp = _ssd(j, z, xbc, dtg, m_conv_w, conv_b3, hp, ng3, gp)
    ys, ss = _ssd(j, z, xbc, dtg, m_conv_w, conv_b3, hp, ng3, gs, state_conv_pad, state_ssm)
    y = _mm(jnp.concatenate([yp, ys], axis=0), m_w_out, (j,), tm=464, name="ssd_out")
    mp = gp[0] * gp[1]
    xp = xbc[:mp].reshape(gp[0], gp[1], M_CONV_DIM)[:, gp[1] - (M_CONV - 1):]
    xs = xbc[mp:].reshape(gs[0], gs[1], M_CONV_DIM)[:, gs[1] - (M_CONV - 1):]
    return y, (sp, xp), (ss, xs)


R_TOK = 64
R_PAIRS = 4


def _rwkv_chunk(nb, nv, toks, sbs, prm):
    n = R_TOK // nb
    ntok = R_TOK
    r, kr, v, wl, al, g = (_pad_rows(t_, ntok) for t_ in toks)
    w0, a0, k_k, k_a, r_k, ln_g, ln_b = prm
    m0 = _iota((1, LANES), 1) < R_HEADSIZE
    ones_blk = ((_iota((LANES, LANES), 0) < R_HEADSIZE) == (_iota((LANES, LANES), 1) < R_HEADSIZE)).astype(BF16)

    w = -_softplus(-(w0 + wl)) - 0.5
    lw = -jnp.exp(w)
    a = _sigmoid(a0 + al)
    kk = kr * k_k
    k = kr * (1.0 + (a - 1.0) * k_a)
    if nv < n:
        valid = _iota((ntok, 1), 0) < nv
        lw = jnp.where(valid, lw, 0.0)
        kk = jnp.where(valid, kk, 0.0)
        k = jnp.where(valid, k, 0.0)
        v = jnp.where(valid, v, 0.0)
    same, causal = _seq_masks(ntok, n)
    kk_sq = _xdot(kk * kk, ones_blk)
    lam = _xdot_left(causal.astype(BF16), lw)
    lam_tot = lam[ntok - 1:ntok] if nb == 1 else _xdot_left(same.astype(BF16), lw)
    bonus = _xdot(r * k * r_k, ones_blk) * v
    yield
    kk = kk / jnp.maximum(jnp.sqrt(kk_sq), 1e-12)
    beta = kk * a
    e_in = jnp.exp(lam)
    e_out = jnp.exp(-lam)
    e_end = jnp.exp(lam_tot - lam)

    def stack(x):
        x3 = x.reshape(nb, n, LANES)
        s = jnp.concatenate([jnp.where(m0, x3, 0.0), jnp.where(m0, 0.0, x3)], axis=1)
        return s.reshape(2 * ntok, LANES).astype(BF16)

    am, rm = stack(kk * jnp.exp(lam - lw)), stack(r * e_in)
    khm, bhm = stack(k * e_out), stack(beta * e_out)
    vst = stack(v)
    n2 = 2 * ntok
    rr, cc = _iota((n2, n2), 0), _iota((n2, n2), 1)
    same_blk = (rr >> _log2(n)) == (cc >> _log2(n))
    strict = same_blk & ((rr & (n - 1)) > (cc & (n - 1)))
    incl = same_blk & ((rr & (n - 1)) >= (cc & (n - 1)))
    p_ak, p_ab, p_rk, p_rb = _dot_nt(am, khm), _dot_nt(am, bhm), _dot_nt(rm, khm), _dot_nt(rm, bhm)
    w2 = 2 * n
    qs = [_dot_nt(jnp.concatenate([am[b * w2:(b + 1) * w2], rm[b * w2:(b + 1) * w2]], axis=0), sbs[b])
          for b in range(nb)]
    ktm, btm = stack(k * e_end), stack(beta * e_end)
    yield
    a_ak = jnp.where(strict, p_ak, 0.0)
    nmat = jnp.where(strict, -p_ab, 0.0)
    a_rk = jnp.where(incl, p_rk, 0.0)
    a_rb = jnp.where(incl, p_rb, 0.0)
    rhs_s = qs[0][:w2] if nb == 1 else jnp.concatenate([q[:w2] for q in qs], axis=0)
    ys = qs[0][w2:] if nb == 1 else jnp.concatenate([q[w2:] for q in qs], axis=0)

    tm = (rr == cc).astype(F32) + nmat
    npow = _dot(nmat, nmat)
    rhs = rhs_s + _dot(a_ak, vst)
    ys = ys + _dot(a_rk, vst)
    yield
    for i in range(_log2(n) - 1):
        prod = _dot(tm, npow)
        if i + 2 < _log2(n):
            npow = _dot(npow, npow)
        yield
        tm = tm + prod
    u = _dot(tm, rhs)
    yield
    yst = (ys - _dot(a_rb, u)).reshape(nb, w2, LANES)
    nu = (-u).astype(BF16)
    new_sbs = []
    for b in range(nb):
        rows = slice(b * w2, (b + 1) * w2)
        lt = lam_tot if nb == 1 else lam_tot[b * n:b * n + 1]
        new_sbs.append(sbs[b] * jnp.exp(lt)
                       + _dot_tn(jnp.concatenate([vst[rows], nu[rows]], axis=0),
                                 jnp.concatenate([ktm[rows], btm[rows]], axis=0)))
    yield
    y = (yst[:, :n] + yst[:, n:]).reshape(ntok, LANES)
    inv = 1.0 / R_HEADSIZE
    mean = _xdot(y, ones_blk) * inv
    yield
    yc = y - mean
    var = _xdot(yc * yc, ones_blk) * inv
    yield
    yn = yc * lax.rsqrt(var + R_LN_EPS) * ln_g + ln_b
    return ((yn + bonus) * g)[:nb * nv], new_sbs


def _rwkv_body(*refs, t, nb, has_state):
    if has_state:
        (r_ref, k_ref, v_ref, wl_ref, al_ref, g_ref, prm_ref, sin_ref, y_ref, sout_ref) = refs
    else:
        (r_ref, k_ref, v_ref, wl_ref, al_ref, g_ref, prm_ref, y_ref, sout_ref) = refs
    tok_refs = (r_ref, k_ref, v_ref, wl_ref, al_ref, g_ref)
    zeros = jnp.zeros((R_HEADSIZE, R_HEADSIZE), F32)

    def lanes(p):
        return slice(p * LANES, (p + 1) * LANES)

    def run(rows, nv, states):
        res = _lockstep([
            _rwkv_chunk(nb, nv, tuple(ref[rows, lanes(p)] for ref in tok_refs), states[p],
                        tuple(prm_ref[i:i + 1, lanes(p)] for i in range(7)))
            for p in range(R_PAIRS)])
        for p, (out, _) in enumerate(res):
            y_ref[rows, lanes(p)] = out.astype(y_ref.dtype)
        return tuple(tuple(sbs) for _, sbs in res)

    if has_state:
        states = tuple(
            tuple(jnp.concatenate([jnp.concatenate([sin_ref[b, 2 * p], zeros], axis=1),
                                   jnp.concatenate([zeros, sin_ref[b, 2 * p + 1]], axis=1)], axis=0)
                  for b in range(nb))
            for p in range(R_PAIRS))
    else:
        states = tuple(tuple(jnp.zeros((LANES, LANES), F32) for _ in range(nb)) for _ in range(R_PAIRS))

    if nb == 1:
        n_full = t // R_TOK
        states = lax.fori_loop(
            0, n_full, lambda c, st: run(pl.ds(pl.multiple_of(c * R_TOK, R_TOK), R_TOK), R_TOK, st), states)
        if t % R_TOK:
            states = run(pl.ds(n_full * R_TOK, t % R_TOK), t % R_TOK, states)
    else:
        assert nb * t == R_TOK
        states = run(slice(None), t, states)
    for p in range(R_PAIRS):
        for b in range(nb):
            sout_ref[b, 2 * p] = states[p][b][:R_HEADSIZE, :R_HEADSIZE]
            sout_ref[b, 2 * p + 1] = states[p][b][R_HEADSIZE:, R_HEADSIZE:]


def _rwkv(r, k, v, wl, al, g, prm, group, wkv_state=None):
    nseq, t, row0, nb = group
    rb = nb * t
    r0 = row0 // rb
    assert row0 % rb == 0 and nseq % nb == 0
    has_state = wkv_state is not None
    gl = R_PAIRS * LANES
    tok = pl.BlockSpec((rb, gl), lambda s, hp_: (s + r0, hp_), **_single(not has_state))
    in_specs = [tok] * 6 + [pl.BlockSpec((SUBLANES, gl), lambda s, hp_: (0, hp_))]
    args = [r, k, v, wl, al, g, prm]
    if has_state:
        in_specs.append(pl.BlockSpec((None, nb, 2 * R_PAIRS, R_HEADSIZE, R_HEADSIZE),
                                     lambda s, hp_: (0, s, hp_, 0, 0)))
        args.append(wkv_state)
    return pl.pallas_call(
        functools.partial(_rwkv_body, t=t, nb=nb, has_state=has_state),
        grid=(nseq // nb, R_HEADS // (2 * R_PAIRS)),
        in_specs=in_specs,
        out_specs=[pl.BlockSpec((rb, gl), lambda s, hp_: (s, hp_)),
                   pl.BlockSpec((nb, 2 * R_PAIRS, R_HEADSIZE, R_HEADSIZE), lambda s, hp_: (s, hp_, 0, 0))],
        out_shape=[jax.ShapeDtypeStruct((nseq * t, D_MODEL), BF16),
                   jax.ShapeDtypeStruct((nseq, R_HEADS, R_HEADSIZE, R_HEADSIZE), F32)],
        compiler_params=_cparams(2),
        name="rwkv_sample" if has_state else "rwkv_prompt",
    )(*args)


def _mix_body(u_ref, p_ref, mu_ref, *o_refs):
    u = u_ref[...]
    dx = p_ref[...] - u
    for i, o_ref in enumerate(o_refs):
        o_ref[...] = (u + dx * mu_ref[i:i + 1, :]).astype(o_ref.dtype)


def _rwkv_mix(u, prev, r_mu, tm=464):
    m, d = u.shape
    tm = _row_tile(m, tm)
    row = pl.BlockSpec((tm, d), lambda i: (i, 0))
    return pl.pallas_call(
        _mix_body,
        grid=(m // tm,),
        in_specs=[row, row, pl.BlockSpec((None, 6, d), lambda i: (0, 0, 0))],
        out_specs=[row] * 6,
        out_shape=[jax.ShapeDtypeStruct((m, d), BF16)] * 6,
        compiler_params=_cparams(1),
        name="rwkv_mix",
    )(u, prev, r_mu)


def _rwkv_layer(u, groups, state_wkv, state_shift, r_mu, r_w_r, r_w_k, r_w_v, r_w0, r_w1, r_w2, r_a0, r_a1,
                r_a2, r_g1, r_g2, r_k_k, r_k_a, r_r_k, r_ln_g, r_ln_b, r_w_o):
    gp, gs = groups
    mp, ms = gp[0] * gp[1], gs[0] * gs[1]
    up = u[:mp].reshape(gp[0], gp[1], D_MODEL)
    us = u[mp:].reshape(gs[0], gs[1], D_MODEL)
    prev = jnp.concatenate([
        jnp.concatenate([jnp.zeros((gp[0], 1, D_MODEL), F32), up[:, :-1]], axis=1).reshape(mp, D_MODEL),
        jnp.concatenate([state_shift[0][:, None, :], us[:, :-1]], axis=1).reshape(ms, D_MODEL)], axis=0)
    xr, xw, xk, xv, xa, xg = _rwkv_mix(u, prev, r_mu)
    r = _mm(xr, r_w_r, (0,), name="rwkv_r")
    k = _mm(xk, r_w_k, (0,), name="rwkv_k")
    v = _mm(xv, r_w_v, (0,), name="rwkv_v")
    wl = _mm(_mm(xw, r_w1, (0,), out_dtype=BF16, act="tanh", name="rwkv_w1"), r_w2, (0,), name="rwkv_w2")
    al = _mm(_mm(xa, r_a1, (0,), out_dtype=BF16, name="rwkv_a1"), r_a2, (0,), name="rwkv_a2")
    g = _mm(_mm(xg, r_g1, (0,), out_dtype=BF16, act="sigmoid", name="rwkv_g1"), r_g2, (0,), name="rwkv_g2")
    prm = jnp.concatenate([r_w0, r_a0, r_k_k, r_k_a, r_r_k.reshape(1, D_MODEL), r_ln_g, r_ln_b,
                           jnp.zeros((1, D_MODEL), F32)], axis=0)
    yp, sp = _rwkv(r, k, v, wl, al, g, prm, gp)
    ys, ss = _rwkv(r, k, v, wl, al, g, prm, gs, state_wkv)
    y = _mm(jnp.concatenate([yp, ys], axis=0), r_w_o, (0,), name="rwkv_out")
    return y, (sp, up[:, -1]), (ss, us[:, -1])


H_CHUNK = 128
H_SUB = 16
H_HPS = 2


def _gla_diag(q3, g3, k3, v3):
    sub = q3.shape[1]
    tpos = _iota((1, sub, 1), 1)
    y3 = jnp.zeros(v3.shape, F32)
    for s in range(sub):
        arg = jnp.where(tpos >= s, g3 - g3[:, s:s + 1, :], -jnp.inf)
        a_s = jnp.sum(q3 * k3[:, s:s + 1, :] * jnp.exp(arg), axis=-1, keepdims=True)
        yield
        y3 = y3 + a_s * v3[:, s:s + 1, :]
    return y3


def _gla_prompt_chunk(nv, q, logf, k, v, st):
    n = -(-nv // H_SUB) * H_SUB
    q, logf, k, v = (_pad_rows(t_, n) for t_ in (q, logf, k, v))
    causal = _iota((n, n), 0) >= _iota((n, n), 1)
    gcum = _xdot_left(causal.astype(BF16), logf)
    yield
    y = _dot_nt(q * jnp.exp(gcum), st)
    g_n = gcum[n - 1:n]
    upd = _dot_tn(v, k * jnp.exp(g_n - gcum))
    atts = []
    for i in range(1, n // H_SUB):
        lo = i * H_SUB
        gs = gcum[lo - 1:lo]
        atts.append(_dot_nt(q[lo:lo + H_SUB] * jnp.exp(gcum[lo:lo + H_SUB] - gs),
                            k[:lo] * jnp.exp(gs - gcum[:lo])))
    yield
    st = st * jnp.exp(g_n) + upd
    offs = [jnp.zeros((H_SUB, LANES), F32)] + [_dot(att, v[:(i + 1) * H_SUB]) for i, att in enumerate(atts)]
    shape3 = (n // H_SUB, H_SUB, LANES)
    y3 = yield from _gla_diag(q.reshape(shape3), gcum.reshape(shape3), k.reshape(shape3), v.reshape(shape3))
    off = offs[0] if len(offs) == 1 else jnp.concatenate(offs, axis=0)
    return (y + off + y3.reshape(n, LANES))[:nv], st


def _gla_sample_chunk(nb, q, logf, k, v, sts):
    n = SUBLANES
    ntok = nb * n
    _, causal = _seq_masks(ntok, n)
    gcum = _xdot_left(causal.astype(BF16), logf)
    yield
    shape3 = (nb, n, LANES)
    g3 = gcum.reshape(shape3)
    g_tot = g3[:, n - 1:n, :]
    qe = q * jnp.exp(gcum)
    kd = (k.reshape(shape3) * jnp.exp(g_tot - g3)).reshape(ntok, LANES)
    ys = [_dot_nt(_pad_rows(qe[b * n:(b + 1) * n], BF16_ROWS), sts[b])[:n] for b in range(nb)]
    upds = [_dot_tn(_pad_rows(v[b * n:(b + 1) * n], BF16_ROWS), _pad_rows(kd[b * n:(b + 1) * n], BF16_ROWS))
            for b in range(nb)]
    y3 = yield from _gla_diag(q.reshape(shape3), g3, k.reshape(shape3), v.reshape(shape3))
    new = [sts[b] * jnp.exp(g_tot[b]) + upds[b] for b in range(nb)]
    return y3.reshape(ntok, LANES) + jnp.concatenate(ys, axis=0), new


def _hgrn_body(*refs, t, nb, has_state, layer):
    if has_state:
        (q_ref, f_ref, i_ref, g_ref, lb_ref, ng_ref, sin_ref, y_ref, sout_ref) = refs
    else:
        (q_ref, f_ref, i_ref, g_ref, lb_ref, ng_ref, y_ref, sout_ref) = refs
    hl = lb_ref[...]
    e = jnp.exp(hl - jnp.max(hl, axis=0, keepdims=True))
    sm = e / jnp.sum(e, axis=0, keepdims=True)
    lb_all = (jnp.sum(sm[1:layer + 1], axis=0, keepdims=True) if layer > 0
              else jnp.zeros((1, H_HPS * LANES), F32))
    ng = ng_ref[...]

    def lanes(hi):
        return slice(hi * LANES, (hi + 1) * LANES)

    def gates(rows, hi):
        lb = lb_all[:, lanes(hi)]
        q = _silu(q_ref[rows, lanes(hi)])
        f = lb + (1.0 - lb) * _sigmoid(f_ref[rows, lanes(hi)])
        return q, jnp.log(f), 1.0 - f, i_ref[rows, lanes(hi)]

    def finish(rows, hi, yy):
        yy = yy * lax.rsqrt(jnp.mean(yy * yy, axis=-1, keepdims=True) + EPS) * ng
        y_ref[rows, lanes(hi)] = (yy * _silu(g_ref[rows, lanes(hi)])).astype(y_ref.dtype)

    if has_state:
        rows = slice(None)
        res = _lockstep([_gla_sample_chunk(nb, *gates(rows, hi), [sin_ref[b, hi].T for b in range(nb)])
                         for hi in range(H_HPS)])
        for hi, (yy, sts) in enumerate(res):
            finish(rows, hi, yy)
            for b in range(nb):
                sout_ref[b, hi] = sts[b].T
        return

    def run(start, nv, sts):
        rows = pl.ds(pl.multiple_of(start, SUBLANES), nv)
        res = _lockstep([_gla_prompt_chunk(nv, *gates(rows, hi), sts[hi]) for hi in range(H_HPS)])
        for hi, (yy, _) in enumerate(res):
            finish(rows, hi, yy)
        return tuple(st for _, st in res)

    sts = tuple(jnp.zeros((H_VDIM, H_EXPAND), F32) for _ in range(H_HPS))
    n_full = t // H_CHUNK
    sts = lax.fori_loop(0, n_full, lambda c, s: run(c * H_CHUNK, H_CHUNK, s), sts)
    if t % H_CHUNK:
        sts = run(n_full * H_CHUNK, t % H_CHUNK, sts)
    for hi in range(H_HPS):
        sout_ref[0, hi] = sts[hi].T


def _hgrn(layer, proj, h_lb, h_norm_g, group, state=None):
    nseq, t, row0, nb = group
    rb = nb * t
    r0 = row0 // rb
    assert row0 % rb == 0 and nseq % nb == 0
    has_state = state is not None
    nhb = H_HEADS // H_HPS
    hw = H_HPS * LANES

    def tok(section):
        return pl.BlockSpec((rb, hw), lambda s, h: (s + r0, h + section * nhb))

    in_specs = [tok(0), tok(1), tok(2), tok(3),
                pl.BlockSpec((DEPTH, hw), lambda s, h: (0, h)),
                pl.BlockSpec((1, LANES), lambda s, h: (0, 0))]
    args = [proj, proj, proj, proj, h_lb, h_norm_g]
    if has_state:
        in_specs.append(pl.BlockSpec((None, nb, H_HPS, H_EXPAND, H_VDIM), lambda s, h: (0, s, h, 0, 0)))
        args.append(state)
    return pl.pallas_call(
        functools.partial(_hgrn_body, t=t, nb=nb, has_state=has_state, layer=layer),
        grid=(nseq // nb, nhb),
        in_specs=in_specs,
        out_specs=[pl.BlockSpec((rb, hw), lambda s, h: (s, h)),
                   pl.BlockSpec((nb, H_HPS, H_EXPAND, H_VDIM), lambda s, h: (s, h, 0, 0))],
        out_shape=[jax.ShapeDtypeStruct((nseq * t, D_MODEL), BF16),
                   jax.ShapeDtypeStruct((nseq, H_HEADS, H_EXPAND, H_VDIM), F32)],
        compiler_params=_cparams(2),
        name="hgrn_sample" if has_state else "hgrn_prompt",
    )(*args)


def _hgrn_layer(layer, u, groups, state_hgrn, h_w_in, h_lb, h_norm_g, h_w_out):
    proj = _mm(u, h_w_in, (0,), name="hgrn_in")
    yp, sp = _hgrn(layer, proj, h_lb, h_norm_g, groups[0])
    ys, ss = _hgrn(layer, proj, h_lb, h_norm_g, groups[1], state_hgrn)
    y = _mm(jnp.concatenate([yp, ys], axis=0), h_w_out, (0,), name="hgrn_out")
    return y, sp, ss


def kernel(x_prompt, x_sample, state_ssm, state_conv, state_wkv, state_shift, state_hgrn, meta_tokens, norm_g, ffn_w_gate, ffn_w_up, ffn_w_down, m_w_in, m_conv_w, m_conv_b, m_dt_bias, m_a_log, m_d, m_norm_g, m_w_out, r_mu, r_w_r, r_w_k, r_w_v, r_w0, r_w1, r_w2, r_a0, r_a1, r_a2, r_g1, r_g2, r_k_k, r_k_a, r_r_k, r_ln_g, r_ln_b, r_w_o, h_w_in, h_lb, h_norm_g, h_w_out):
    meta = jnp.broadcast_to(meta_tokens[None], (BATCH, N_META, D_MODEL))
    h = jnp.concatenate([jnp.concatenate([meta, x_prompt], axis=1).reshape(M_P, D_MODEL),
                         x_sample.reshape(M_S, D_MODEL)], axis=0)
    conv_pad = jnp.pad(state_conv, ((0, 0), (0, 0), (SUBLANES - (M_CONV - 1), 0), (0, 0)))

    p_ssm, p_conv, s_ssm, s_conv = [], [], [], []
    u = _norm(h, norm_g, 0, 0)
    for l in range(DEPTH):
        kind, j = l % 3, l // 3
        y = _mm(_swiglu_up(u, ffn_w_gate, ffn_w_up, (l, 0)), ffn_w_down, (l, 0), tm=464, name="ffn_down")
        h, u = _resnorm(h, y, norm_g, 0.5, l, 1, l, 2, u_dtype=F32 if kind == 1 else BF16)
        if kind == 0:
            y, (sp, cp), (ss, cs) = _mamba_layer(j, u, GROUPS, m_w_in, m_conv_w, m_conv_b, m_dt_bias, m_a_log,
                                                 m_d, m_norm_g, m_w_out, conv_pad, state_ssm)
            p_ssm.append(sp)
            p_conv.append(cp)
            s_ssm.append(ss)
            s_conv.append(cs)
        elif kind == 1:
            y, (p_wkv, p_shift), (s_wkv, s_shift) = _rwkv_layer(
                u, GROUPS, state_wkv, state_shift, r_mu, r_w_r, r_w_k, r_w_v, r_w0, r_w1, r_w2, r_a0, r_a1,
                r_a2, r_g1, r_g2, r_k_k, r_k_a, r_r_k, r_ln_g, r_ln_b, r_w_o)
        else:
            y, p_hgrn, s_hgrn = _hgrn_layer(l, u, GROUPS, state_hgrn, h_w_in, h_lb, h_norm_g, h_w_out)
        h, u = _resnorm(h, y, norm_g, 1.0, l, 3, l, 4)
        y = _mm(_swiglu_up(u, ffn_w_gate, ffn_w_up, (l, 1)), ffn_w_down, (l, 1), tm=464, name="ffn_down")
        if l + 1 < DEPTH:
            h, u = _resnorm(h, y, norm_g, 0.5, l, 5, l + 1, 0)
        else:
            h, _ = _resnorm(h, y, norm_g, 0.5, l, 5)

    y_prompt = h[:M_P].reshape(BATCH, T_P, D_MODEL)[:, N_META:]
    y_sample = h[M_P:].reshape(DEC_BATCH, T_S, D_MODEL)
    return (y_prompt, y_sample,
            jnp.stack(p_ssm), jnp.stack(p_conv), p_wkv[None], p_shift[None], p_hgrn[None],
            jnp.stack(s_ssm), jnp.stack(s_conv), s_wkv[None], s_shift[None], s_hgrn[None])
```

```python
import functools

import jax
import jax.numpy as jnp
from jax import lax
from jax.experimental import pallas as pl
from jax.experimental.pallas import tpu as pltpu

F32 = jnp.float32
BF16 = jnp.bfloat16

D_MODEL = 2048
BATCH = 4
SEQ = 2048
DEPTH = 4
DEC_BATCH = 128
DEC_SEQ = 8
N_META = 16
EPS = 1e-6
D_FF = 5632

M_D_INNER = 2 * D_MODEL
M_HEADDIM = 64
M_HEADS = M_D_INNER // M_HEADDIM
M_GROUPS = 8
M_D_STATE = 128
M_CONV = 4
M_CONV_DIM = M_D_INNER + 2 * M_GROUPS * M_D_STATE
M_NORM_EPS = 1e-5
M_GW = M_D_INNER // M_GROUPS
M_R = M_HEADS // M_GROUPS

R_HEADSIZE = 64
R_HEADS = D_MODEL // R_HEADSIZE
R_LN_EPS = 64e-5

H_HEADS = 16
H_EXPAND = 128
H_FDIM = H_HEADS * H_EXPAND
H_VDIM = D_MODEL // H_HEADS

T_P = N_META + SEQ
T_S = DEC_SEQ
M_P = BATCH * T_P
M_S = DEC_BATCH * T_S
M_TOK = M_P + M_S
S_BS = 8
LANES = 128
SUBLANES = 8
BF16_ROWS = 16
VMEM_LIMIT = 56 * 1024 * 1024

GROUPS = ((BATCH, T_P, 0, 1), (DEC_BATCH, T_S, M_P, S_BS))


def _cparams(n_axes):
    return pltpu.CompilerParams(dimension_semantics=("arbitrary",) * n_axes,
                                vmem_limit_bytes=VMEM_LIMIT)


def _silu(x):
    return x * (1.0 / (1.0 + jnp.exp(-x)))


def _sigmoid(x):
    return 1.0 / (1.0 + jnp.exp(-x))


def _softplus(x):
    return jnp.maximum(x, 0.0) + jnp.log1p(jnp.exp(-jnp.abs(x)))


def _iota(shape, dim):
    return lax.broadcasted_iota(jnp.int32, shape, dim)


def _log2(n):
    assert n & (n - 1) == 0
    return n.bit_length() - 1


def _dot(a, b):
    return jnp.dot(a.astype(BF16), b.astype(BF16), preferred_element_type=F32)


def _dot_nt(a, b):
    return lax.dot_general(a.astype(BF16), b.astype(BF16), (((1,), (1,)), ((), ())),
                           preferred_element_type=F32)


def _dot_tn(a, b):
    return lax.dot_general(a.astype(BF16), b.astype(BF16), (((0,), (0,)), ((), ())),
                           preferred_element_type=F32)


def _split(x, terms):
    parts = []
    rem = x
    for _ in range(terms):
        hi = rem.astype(BF16)
        parts.append(hi)
        rem = rem - hi.astype(F32)
    return parts


def _xdot(x, sel, terms=3):
    return sum(jnp.dot(p, sel, preferred_element_type=F32) for p in _split(x, terms))


def _xdot_left(sel, x, terms=3):
    return sum(jnp.dot(sel, p, preferred_element_type=F32) for p in _split(x, terms))


def _xdot_nt(sel, x, terms=3):
    return sum(lax.dot_general(sel, p, (((1,), (1,)), ((), ())), preferred_element_type=F32)
               for p in _split(x, terms))


def _pad_rows(x, n):
    if x.shape[0] >= n:
        return x
    return jnp.concatenate([x, jnp.zeros((n - x.shape[0],) + x.shape[1:], x.dtype)], axis=0)


def _seq_masks(ntok, n):
    r, c = _iota((ntok, ntok), 0), _iota((ntok, ntok), 1)
    same = (r >> _log2(n)) == (c >> _log2(n))
    return same, same & (r >= c)


def _lockstep(gens):
    results = [None] * len(gens)
    live = list(enumerate(gens))
    while live:
        still = []
        for i, gen in live:
            try:
                next(gen)
                still.append((i, gen))
            except StopIteration as done:
                results[i] = done.value
        live = still
    return results


def _write_into(body, in_specs, args, targets):
    n_in = len(args)
    aliases = {}
    for k, arr in enumerate(targets):
        if arr is not None:
            aliases[len(args)] = k
            in_specs.append(pl.BlockSpec(memory_space=pl.ANY))
            args.append(arr)
    n_extra = len(args) - n_in

    def wrapped(*refs):
        return body(*refs[:n_in], *refs[n_in + n_extra:])

    return (wrapped if n_extra else body), aliases


def _row_tile(m, want):
    return next(d for d in range(want - want % BF16_ROWS, 0, -BF16_ROWS) if m % d == 0)


def _single(mode):
    return dict(pipeline_mode=pl.Buffered(1)) if mode else {}


def _mm_body(x_ref, w_ref, o_ref, wbf_ref, *, act, valid_cols):
    @pl.when(pl.program_id(1) == 0)
    def _():
        w = w_ref[...]
        if valid_cols is not None:
            w = jnp.where(_iota(w.shape, 1) < valid_cols, w, 0.0)
        wbf_ref[...] = w.astype(BF16)

    acc = jnp.dot(x_ref[...].astype(BF16), wbf_ref[...], preferred_element_type=F32)
    if act == "tanh":
        acc = jnp.tanh(acc)
    elif act == "sigmoid":
        acc = _sigmoid(acc)
    o_ref[...] = acc.astype(o_ref.dtype)


def _mm(x, w, lead=(), col0=0, ncols=None, tn=512, tm=1856, out_dtype=F32, act=None, name="mm"):
    m, k = x.shape
    ncols = w.shape[-1] if ncols is None else ncols
    tn = min(tn, ncols)
    tm = _row_tile(m, tm)
    assert ncols % tn == 0 and col0 % tn == 0 and m % tm == 0 and w.shape[-2] == k
    valid_cols = None
    if col0 + ncols > w.shape[-1]:
        assert ncols == tn
        valid_cols = w.shape[-1] - col0
    jb = col0 // tn
    wspec = pl.BlockSpec((None,) * len(lead) + (k, tn), lambda j, i: lead + (0, j + jb))
    return pl.pallas_call(
        functools.partial(_mm_body, act=act, valid_cols=valid_cols),
        grid=(ncols // tn, m // tm),
        in_specs=[pl.BlockSpec((tm, k), lambda j, i: (i, 0)), wspec],
        out_specs=pl.BlockSpec((tm, tn), lambda j, i: (i, j)),
        out_shape=jax.ShapeDtypeStruct((m, ncols), out_dtype),
        scratch_shapes=[pltpu.VMEM((k, tn), BF16)],
        compiler_params=_cparams(2),
        name=name,
    )(x, w)


def _swiglu_up_body(x_ref, wg_ref, wu_ref, o_ref, wg_bf, wu_bf):
    @pl.when(pl.program_id(1) == 0)
    def _():
        wg_bf[...] = wg_ref[...].astype(BF16)
        wu_bf[...] = wu_ref[...].astype(BF16)

    x = x_ref[...]
    g = jnp.dot(x, wg_bf[...], preferred_element_type=F32)
    u = jnp.dot(x, wu_bf[...], preferred_element_type=F32)
    o_ref[...] = (_silu(g) * u).astype(o_ref.dtype)


def _swiglu_up(x, w_gate, w_up, lead, tn=512, tm=1856):
    m, k = x.shape
    n = w_gate.shape[-1]
    tm = _row_tile(m, tm)
    wspec = pl.BlockSpec((None,) * len(lead) + (k, tn), lambda j, i: lead + (0, j))
    return pl.pallas_call(
        _swiglu_up_body,
        grid=(n // tn, m // tm),
        in_specs=[pl.BlockSpec((tm, k), lambda j, i: (i, 0)), wspec, wspec],
        out_specs=pl.BlockSpec((tm, tn), lambda j, i: (i, j)),
        out_shape=jax.ShapeDtypeStruct((m, n), BF16),
        scratch_shapes=[pltpu.VMEM((k, tn), BF16), pltpu.VMEM((k, tn), BF16)],
        compiler_params=_cparams(2),
        name="swiglu_up",
    )(x, w_gate, w_up)


def _rms(x, g):
    return x * lax.rsqrt(jnp.mean(x * x, axis=-1, keepdims=True) + EPS) * g


def _norm_body(h_ref, g_ref, u_ref, *, gi):
    u_ref[...] = _rms(h_ref[...], g_ref[gi:gi + 1, :]).astype(u_ref.dtype)


def _norm(h, norm_g, layer, gi, out_dtype=BF16, tm=464):
    m, d = h.shape
    tm = _row_tile(m, tm)
    return pl.pallas_call(
        functools.partial(_norm_body, gi=gi),
        grid=(m // tm,),
        in_specs=[pl.BlockSpec((tm, d), lambda i: (i, 0)),
                  pl.BlockSpec((None, 6, d), lambda i: (layer, 0, 0))],
        out_specs=pl.BlockSpec((tm, d), lambda i: (i, 0)),
        out_shape=jax.ShapeDtypeStruct((m, d), out_dtype),
        compiler_params=_cparams(1),
        name="norm",
    )(h, norm_g)


def _resnorm_body(h_ref, y_ref, ga_ref, gb_ref, hn_ref, u_ref, *, scale, ia, ib):
    hn = h_ref[...] + scale * _rms(y_ref[...], ga_ref[ia:ia + 1, :])
    hn_ref[...] = hn
    if u_ref is not None:
        u_ref[...] = _rms(hn, gb_ref[ib:ib + 1, :]).astype(u_ref.dtype)


def _resnorm(h, y, norm_g, scale, la, ia, lb=None, ib=None, u_dtype=BF16, tm=464):
    m, d = h.shape
    tm = _row_tile(m, tm)
    row = pl.BlockSpec((tm, d), lambda i: (i, 0))
    with_u = lb is not None
    body = functools.partial(_resnorm_body, scale=scale, ia=ia, ib=ib)
    if not with_u:
        body = functools.partial(_resnorm_body, u_ref=None, scale=scale, ia=ia, ib=ib)
        lb = la
    out_shape = [jax.ShapeDtypeStruct((m, d), F32)]
    out_specs = [row]
    if with_u:
        out_shape.append(jax.ShapeDtypeStruct((m, d), u_dtype))
        out_specs.append(row)
    res = pl.pallas_call(
        body,
        grid=(m // tm,),
        in_specs=[row, row,
                  pl.BlockSpec((None, 6, d), lambda i: (la, 0, 0)),
                  pl.BlockSpec((None, 6, d), lambda i: (lb, 0, 0))],
        out_specs=out_specs,
        out_shape=out_shape,
        compiler_params=_cparams(1),
        name="resnorm",
    )(h, y, norm_g, norm_g)
    return res if with_u else (res[0], None)


SSD_CHUNK = 128
SSD_GPS = 2


def _conv_taps(cur, shifted, w, b):
    out = b + cur * w[M_CONV - 1:M_CONV]
    for k in range(1, M_CONV):
        out = out + shifted(k) * w[M_CONV - 1 - k:M_CONV - k]
    return _silu(out)


def _conv_silu_seq(cur, prv, w, b):
    row = _iota((SUBLANES, cur.shape[1]), 0)

    def shifted(k):
        sh = pltpu.roll(cur, k, 0)
        top = jnp.where(row < k, pltpu.roll(prv, k, 0), sh[:SUBLANES])
        return jnp.concatenate([top, sh[SUBLANES:]], axis=0)

    return _conv_taps(cur, shifted, w, b)


def _conv_silu_batch(cur, stt, w, b):
    ntok = cur.shape[0]
    tpos = _iota((ntok, 1), 0) & (SUBLANES - 1)

    def shifted(k):
        return jnp.where(tpos < k, pltpu.roll(stt, (k - SUBLANES) % ntok, 0), pltpu.roll(cur, k, 0))

    return _conv_taps(cur, shifted, w, b)


def _expand_heads(rows):
    head_of_lane = _iota((M_R, M_GW), 1) >> _log2(M_HEADDIM)
    sel = (head_of_lane == _iota((M_R, M_GW), 0)).astype(BF16)
    return _xdot(rows, sel, terms=2)


def _ssd_common(x, bm, cm, dtr, hp, causal, same, nv, extra_rows):
    ntok = x.shape[0]
    bias, a = hp[0:1], -jnp.exp(hp[1:2])
    dt = _softplus(dtr + bias)
    if nv < ntok and same is None:
        dt = jnp.where(_iota((ntok, 1), 0) < nv, dt, 0.0)
    da = dt * a
    cum = _xdot_left(causal.astype(BF16), da)
    cum_tot = cum[ntok - 1:ntok] if same is None else _xdot_left(same.astype(BF16), da)
    cbm = _dot_nt(cm, bm)
    eye8 = (_iota((M_R, M_R), 0) == _iota((M_R, M_R), 1)).astype(BF16)
    dt_t = _xdot_nt(eye8, dt)
    even = (_iota((1, M_GW), 1) & M_HEADDIM) == 0
    x_e = jnp.where(even, x, 0.0).astype(BF16)
    x_o = jnp.where(even, 0.0, x).astype(BF16)
    yield
    cum_t = _xdot_nt(eye8, cum)
    tail = jnp.exp(cum_tot - cum) * dt
    ex = _expand_heads(jnp.concatenate([jnp.exp(cum), tail, extra_rows(cum_tot)], axis=0))
    yield
    parts = []
    for q in range(M_R // 2):
        acc = None
        for r, x_m in ((2 * q, x_e), (2 * q + 1, x_o)):
            dec = jnp.exp(jnp.where(causal, cum[:, r:r + 1] - cum_t[r:r + 1, :], -jnp.inf))
            sc = (cbm * dec * dt_t[r:r + 1, :]).astype(BF16)
            yy = jnp.dot(sc, x_m[:, q * LANES:(q + 1) * LANES], preferred_element_type=F32)
            acc = yy if acc is None else acc + yy
        parts.append(acc)
    return jnp.concatenate(parts, axis=1), ex, cum_tot


def _ssd_finish(y, x, zc, d_row, ng):
    yv = (y + d_row * x) * _silu(zc)
    return yv * lax.rsqrt(jnp.mean(yv * yv, axis=-1, keepdims=True) + M_NORM_EPS) * ng


def _ssd_prompt_chunk(nv, zc, xr, br, cr, dtr, prev, st, cw, cb, hp, ng):
    n = SSD_CHUNK
    new_prev = (xr[nv - SUBLANES:], br[nv - SUBLANES:], cr[nv - SUBLANES:])
    zc, xr, br, cr, dtr = (_pad_rows(v, n) for v in (zc, xr, br, cr, dtr))
    x = _conv_silu_seq(xr, prev[0], cw[0], cb[0])
    bm = _conv_silu_seq(br, prev[1], cw[1], cb[1])
    cm = _conv_silu_seq(cr, prev[2], cw[2], cb[2])
    causal = _iota((n, n), 0) >= _iota((n, n), 1)
    row = _iota((SUBLANES, M_R), 0)

    def extra_rows(cum_tot):
        return jnp.where(row == 0, jnp.exp(cum_tot), jnp.where(row == 1, hp[2:3], 0.0))

    y_state = _dot(cm, st)
    y, ex, _ = yield from _ssd_common(x, bm, cm, dtr, hp, causal, None, nv, extra_rows)
    upd = _dot_tn(bm, ex[n:2 * n] * x)
    yield
    y = y + y_state * ex[:n]
    st = st * ex[2 * n:2 * n + 1] + upd
    return _ssd_finish(y, x, zc, ex[2 * n + 1:2 * n + 2], ng)[:nv], new_prev, st


def _ssd_sample_chunk(nb, zc, xr, br, cr, dtr, stt, states, cw, cb, hp, ng):
    n = SUBLANES
    ntok = nb * n
    x = _conv_silu_batch(xr, stt[0], cw[0], cb[0])
    bm = _conv_silu_batch(br, stt[1], cw[1], cb[1])
    cm = _conv_silu_batch(cr, stt[2], cw[2], cb[2])
    same, causal = _seq_masks(ntok, n)
    row = _iota((SUBLANES, M_R), 0)
    ys = [_dot_nt(_pad_rows(cm[b * n:(b + 1) * n], BF16_ROWS), states[b])[:n] for b in range(nb)]
    y, ex, cum_tot = yield from _ssd_common(x, bm, cm, dtr, hp, causal, same, n,
                                            lambda _: jnp.where(row == 0, hp[2:3], 0.0))
    xt = ex[ntok:2 * ntok] * x
    upds = [_dot_tn(_pad_rows(xt[b * n:(b + 1) * n], BF16_ROWS), _pad_rows(bm[b * n:(b + 1) * n], BF16_ROWS))
            for b in range(nb)]
    yield
    etot = jnp.exp(cum_tot)
    new_states = []
    for b in range(nb):
        decay = jnp.concatenate([jnp.broadcast_to(etot[b * n:b * n + 1, r:r + 1], (M_HEADDIM, M_D_STATE))
                                 for r in range(M_R)], axis=0)
        new_states.append(states[b] * decay + upds[b])
    y = y + jnp.concatenate(ys, axis=0) * ex[:ntok]
    return _ssd_finish(y, x, zc, ex[2 * ntok:2 * ntok + 1], ng), new_states


def _ssd_body(*refs, t, nb, has_state):
    if has_state:
        (z_ref, x_ref, b_ref, c_ref, dt_ref, cwx_ref, cwb_ref, cwc_ref, cbx_ref, cbb_ref, cbc_ref,
         hp_ref, ng_ref, csx_ref, csb_ref, csc_ref, sin_ref, y_ref, sout_ref) = refs
    else:
        (z_ref, x_ref, b_ref, c_ref, dt_ref, cwx_ref, cwb_ref, cwc_ref, cbx_ref, cbb_ref, cbc_ref,
         hp_ref, ng_ref, y_ref, sout_ref) = refs

    def consts(gi):
        wide = slice(gi * M_GW, (gi + 1) * M_GW)
        nar = slice(gi * M_D_STATE, (gi + 1) * M_D_STATE)
        cw = (cwx_ref[:, wide], cwb_ref[:, nar], cwc_ref[:, nar])
        cb = (cbx_ref[:, wide], cbb_ref[:, nar], cbc_ref[:, nar])
        return wide, nar, cw, cb, hp_ref[gi], ng_ref[:, wide]

    if has_state:
        ntok = nb * t
        gens = []
        for gi in range(SSD_GPS):
            wide, nar, cw, cb, hp, ng = consts(gi)
            stt = (csx_ref[:, :, wide].reshape(ntok, M_GW), csb_ref[:, :, nar].reshape(ntok, M_D_STATE),
                   csc_ref[:, :, nar].reshape(ntok, M_D_STATE))
            states = [sin_ref[b, gi * M_R:(gi + 1) * M_R].reshape(M_GW, M_D_STATE) for b in range(nb)]
            gens.append(_ssd_sample_chunk(nb, z_ref[:, wide], x_ref[:, wide], b_ref[:, nar], c_ref[:, nar],
                                          dt_ref[gi], stt, states, cw, cb, hp, ng))
        for gi, (out, states) in enumerate(_lockstep(gens)):
            y_ref[:, gi * M_GW:(gi + 1) * M_GW] = out.astype(y_ref.dtype)
            for b in range(nb):
                sout_ref[b, gi * M_R:(gi + 1) * M_R] = states[b].reshape(M_R, M_HEADDIM, M_D_STATE)
        return

    def run(start, nv, carry):
        rows = pl.ds(pl.multiple_of(start, SUBLANES), nv)
        gens = []
        for gi in range(SSD_GPS):
            wide, nar, cw, cb, hp, ng = consts(gi)
            prev, st = carry[gi]
            gens.append(_ssd_prompt_chunk(nv, z_ref[rows, wide], x_ref[rows, wide], b_ref[rows, nar],
                                          c_ref[rows, nar], dt_ref[gi, rows, :], prev, st, cw, cb, hp, ng))
        new = []
        for gi, (out, prev, st) in enumerate(_lockstep(gens)):
            y_ref[rows, gi * M_GW:(gi + 1) * M_GW] = out.astype(y_ref.dtype)
            new.append((prev, st))
        return tuple(new)

    zero_prev = (jnp.zeros((SUBLANES, M_GW), F32), jnp.zeros((SUBLANES, M_D_STATE), F32),
                 jnp.zeros((SUBLANES, M_D_STATE), F32))
    carry = tuple((zero_prev, jnp.zeros((M_D_STATE, M_GW), F32)) for _ in range(SSD_GPS))
    n_full = t // SSD_CHUNK
    carry = lax.fori_loop(0, n_full, lambda c, cr: run(c * SSD_CHUNK, SSD_CHUNK, cr), carry)
    if t % SSD_CHUNK:
        carry = run(n_full * SSD_CHUNK, t % SSD_CHUNK, carry)
    for gi in range(SSD_GPS):
        sout_ref[0, gi * M_R:(gi + 1) * M_R] = carry[gi][1].T.reshape(M_R, M_HEADDIM, M_D_STATE)


def _ssd(j, z, xbc, dtg, m_conv_w, conv_b3, hp, m_norm_g3, group, conv_state=None, ssm_state=None,
         y_into=None, state_into=None):
    nseq, t, row0, nb = group
    rb = nb * t
    r0 = row0 // rb
    assert row0 % rb == 0 and nseq % nb == 0 and M_GROUPS % SSD_GPS == 0
    has_state = ssm_state is not None
    gw, gn, gr = SSD_GPS * M_GW, SSD_GPS * M_D_STATE, SSD_GPS * M_R
    b0 = M_D_INNER // gn
    c0 = b0 + M_GROUPS // SSD_GPS
    one = _single(not has_state)
    in_specs = [
        pl.BlockSpec((rb, gw), lambda s, g: (s + r0, g), **one),
        pl.BlockSpec((rb, gw), lambda s, g: (s + r0, g), **one),
        pl.BlockSpec((rb, gn), lambda s, g: (s + r0, b0 + g), **one),
        pl.BlockSpec((rb, gn), lambda s, g: (s + r0, c0 + g), **one),
        pl.BlockSpec((SSD_GPS, rb, M_R), lambda s, g: (g, s + r0, 0), **one),
        pl.BlockSpec((None, M_CONV, gw), lambda s, g: (j, 0, g)),
        pl.BlockSpec((None, M_CONV, gn), lambda s, g: (j, 0, b0 + g)),
        pl.BlockSpec((None, M_CONV, gn), lambda s, g: (j, 0, c0 + g)),
        pl.BlockSpec((None, 1, gw), lambda s, g: (j, 0, g)),
        pl.BlockSpec((None, 1, gn), lambda s, g: (j, 0, b0 + g)),
        pl.BlockSpec((None, 1, gn), lambda s, g: (j, 0, c0 + g)),
        pl.BlockSpec((None, SSD_GPS, 3, M_R), lambda s, g: (j, g, 0, 0)),
        pl.BlockSpec((None, 1, gw), lambda s, g: (j, 0, g)),
    ]
    args = [z, xbc, xbc, xbc, dtg, m_conv_w, m_conv_w, m_conv_w, conv_b3, conv_b3, conv_b3, hp, m_norm_g3]
    if has_state:
        in_specs += [
            pl.BlockSpec((None, nb, SUBLANES, gw), lambda s, g: (j, s, 0, g)),
            pl.BlockSpec((None, nb, SUBLANES, gn), lambda s, g: (j, s, 0, b0 + g)),
            pl.BlockSpec((None, nb, SUBLANES, gn), lambda s, g: (j, s, 0, c0 + g)),
            pl.BlockSpec((None, nb, gr, M_HEADDIM, M_D_STATE), lambda s, g: (j, s, g, 0, 0)),
        ]
        args += [conv_state, conv_state, conv_state, ssm_state]
    body, aliases = _write_into(functools.partial(_ssd_body, t=t, nb=nb, has_state=has_state),
                                in_specs, args, (y_into, state_into))
    return pl.pallas_call(
        body,
        grid=(nseq // nb, M_GROUPS // SSD_GPS),
        in_specs=in_specs,
        out_specs=[pl.BlockSpec((rb, gw), lambda s, g: (s + r0, g)),
                   pl.BlockSpec((None, nb, gr, M_HEADDIM, M_D_STATE), lambda s, g: (j, s, g, 0, 0))],
        out_shape=[jax.ShapeDtypeStruct((z.shape[0], M_D_INNER), BF16),
                   jax.ShapeDtypeStruct((hp.shape[0], nseq, M_HEADS, M_HEADDIM, M_D_STATE), F32)],
        input_output_aliases=aliases,
        compiler_params=_cparams(2),
        name="ssd_sample" if has_state else "ssd_prompt",
    )(*args)


def _mamba_layer(j, u, groups, m_w_in, m_conv_w, m_conv_b, m_dt_bias, m_a_log, m_d, m_norm_g, m_w_out,
                 state_conv_pad, state_ssm, ssm_into=(None, None)):
    m = u.shape[0]
    gp, gs = groups
    z = _mm(u, m_w_in, (j,), 0, M_D_INNER, name="ssd_in_z")
    xbc = _mm(u, m_w_in, (j,), M_D_INNER, M_CONV_DIM, name="ssd_in_xbc")
    dt = _mm(u, m_w_in, (j,), M_D_INNER + M_CONV_DIM, LANES, tn=LANES, name="ssd_in_dt")[:, :M_HEADS]
    dtg = dt.reshape(m, M_GROUPS, M_R).transpose(1, 0, 2)
    hp = jnp.stack([m_dt_bias, m_a_log, m_d], axis=1).reshape(-1, 3, M_GROUPS, M_R).transpose(0, 2, 1, 3)
    conv_b3 = m_conv_b[:, None, :]
    ng3 = m_norm_g[:, None, :]
    y, sp = _ssd(j, z, xbc, dtg, m_conv_w, conv_b3, hp, ng3, gp, state_into=ssm_into[0])
    y, ss = _ssd(j, z, xbc, dtg, m_conv_w, conv_b3, hp, ng3, gs, state_conv_pad, state_ssm,
                 y_into=y, state_into=ssm_into[1])
    y = _mm(y, m_w_out, (j,), tm=928, name="ssd_out")
    keep = M_CONV - 1
    xp = jnp.stack([xbc[(b + 1) * gp[1] - keep:(b + 1) * gp[1]] for b in range(gp[0])])
    xs = xbc[gp[0] * gp[1]:].reshape(gs[0], gs[1], M_CONV_DIM)[:, gs[1] - keep:]
    return y, (sp, xp), (ss, xs)


R_TOK = 64
R_PAIRS = 4


def _rwkv_chunk(nb, nv, toks, sbs, prm):
    n = R_TOK // nb
    ntok = R_TOK
    r, kr, v, wl, al, g = (_pad_rows(t_, ntok) for t_ in toks)
    w0, a0, k_k, k_a, r_k, ln_g, ln_b = prm
    m0 = _iota((1, LANES), 1) < R_HEADSIZE
    ones_blk = ((_iota((LANES, LANES), 0) < R_HEADSIZE) == (_iota((LANES, LANES), 1) < R_HEADSIZE)).astype(BF16)

    w = -_softplus(-(w0 + wl)) - 0.5
    lw = -jnp.exp(w)
    a = _sigmoid(a0 + al)
    kk = kr * k_k
    k = kr * (1.0 + (a - 1.0) * k_a)
    if nv < n:
        valid = _iota((ntok, 1), 0) < nv
        lw = jnp.where(valid, lw, 0.0)
        kk = jnp.where(valid, kk, 0.0)
        k = jnp.where(valid, k, 0.0)
        v = jnp.where(valid, v, 0.0)
    same, causal = _seq_masks(ntok, n)
    head_sums = _xdot(jnp.concatenate([kk * kk, r * k * r_k], axis=0), ones_blk, terms=2)
    lam = _xdot_left(causal.astype(BF16), lw)
    lam_tot = lam[ntok - 1:ntok] if nb == 1 else _xdot_left(same.astype(BF16), lw)
    yield
    bonus = head_sums[ntok:] * v
    kk = kk / jnp.maximum(jnp.sqrt(head_sums[:ntok]), 1e-12)
    beta = kk * a
    e_in = jnp.exp(lam)
    e_out = jnp.exp(-lam)
    e_end = jnp.exp(lam_tot - lam)

    def stack(x):
        x3 = x.reshape(nb, n, LANES)
        s = jnp.concatenate([jnp.where(m0, x3, 0.0), jnp.where(m0, 0.0, x3)], axis=1)
        return s.reshape(2 * ntok, LANES).astype(BF16)

    am, rm = stack(kk * jnp.exp(lam - lw)), stack(r * e_in)
    khm, bhm = stack(k * e_out), stack(beta * e_out)
    vst = stack(v)
    n2 = 2 * ntok
    rr, cc = _iota((n2, n2), 0), _iota((n2, n2), 1)
    same_blk = (rr >> _log2(n)) == (cc >> _log2(n))
    strict = same_blk & ((rr & (n - 1)) > (cc & (n - 1)))
    incl = same_blk & ((rr & (n - 1)) >= (cc & (n - 1)))
    qm = jnp.concatenate([am, rm], axis=0)
    pmat = _dot_nt(qm, jnp.concatenate([khm, bhm], axis=0))
    w2 = 2 * n
    if nb == 1:
        qs = [_dot_nt(qm, sbs[0])]
    else:
        qs = [_dot_nt(jnp.concatenate([am[b * w2:(b + 1) * w2], rm[b * w2:(b + 1) * w2]], axis=0), sbs[b])
              for b in range(nb)]
    ktm, btm = stack(k * e_end), stack(beta * e_end)
    yield
    a_ak = jnp.where(strict, pmat[:n2, :n2], 0.0)
    nmat = jnp.where(strict, -pmat[:n2, n2:], 0.0)
    a_rk = jnp.where(incl, pmat[n2:, :n2], 0.0)
    a_rb = jnp.where(incl, pmat[n2:, n2:], 0.0)
    rhs_s = qs[0][:w2] if nb == 1 else jnp.concatenate([q[:w2] for q in qs], axis=0)
    ys = qs[0][w2:] if nb == 1 else jnp.concatenate([q[w2:] for q in qs], axis=0)

    tm = (rr == cc).astype(F32) + nmat
    npow = _dot(nmat, nmat)
    av = _dot(jnp.concatenate([a_ak, a_rk], axis=0), vst)
    yield
    rhs = rhs_s + av[:n2]
    ys = ys + av[n2:]
    for i in range(_log2(n) - 1):
        if i + 2 < _log2(n):
            both = _dot(jnp.concatenate([tm, npow], axis=0), npow)
            prod, npow = both[:n2], both[n2:]
        else:
            prod = _dot(tm, npow)
        yield
        tm = tm + prod
    u = _dot(tm, rhs)
    yield
    yst = (ys - _dot(a_rb, u)).reshape(nb, w2, LANES)
    nu = (-u).astype(BF16)
    new_sbs = []
    for b in range(nb):
        rows = slice(b * w2, (b + 1) * w2)
        lt = lam_tot if nb == 1 else lam_tot[b * n:b * n + 1]
        new_sbs.append(sbs[b] * jnp.exp(lt)
                       + _dot_tn(jnp.concatenate([vst[rows], nu[rows]], axis=0),
                                 jnp.concatenate([ktm[rows], btm[rows]], axis=0)))
    yield
    y = (yst[:, :n] + yst[:, n:]).reshape(ntok, LANES)
    inv = 1.0 / R_HEADSIZE
    mean = _xdot(y, ones_blk, terms=2) * inv
    yield
    yc = y - mean
    var = _xdot(yc * yc, ones_blk, terms=2) * inv
    yield
    yn = yc * lax.rsqrt(var + R_LN_EPS) * ln_g + ln_b
    return ((yn + bonus) * g)[:nb * nv], new_sbs


def _rwkv_body(*refs, t, nb, has_state):
    if has_state:
        (r_ref, k_ref, v_ref, wl_ref, al_ref, g_ref, prm_ref, sin_ref, y_ref, sout_ref) = refs
    else:
        (r_ref, k_ref, v_ref, wl_ref, al_ref, g_ref, prm_ref, y_ref, sout_ref) = refs
    tok_refs = (r_ref, k_ref, v_ref, wl_ref, al_ref, g_ref)
    zeros = jnp.zeros((R_HEADSIZE, R_HEADSIZE), F32)

    def lanes(p):
        return slice(p * LANES, (p + 1) * LANES)

    def run(rows, nv, states):
        res = _lockstep([
            _rwkv_chunk(nb, nv, tuple(ref[rows, lanes(p)] for ref in tok_refs), states[p],
                        tuple(prm_ref[i:i + 1, lanes(p)] for i in range(7)))
            for p in range(R_PAIRS)])
        for p, (out, _) in enumerate(res):
            y_ref[rows, lanes(p)] = out.astype(y_ref.dtype)
        return tuple(tuple(sbs) for _, sbs in res)

    if has_state:
        states = tuple(
            tuple(jnp.concatenate([jnp.concatenate([sin_ref[b, 2 * p], zeros], axis=1),
                                   jnp.concatenate([zeros, sin_ref[b, 2 * p + 1]], axis=1)], axis=0)
                  for b in range(nb))
            for p in range(R_PAIRS))
    else:
        states = tuple(tuple(jnp.zeros((LANES, LANES), F32) for _ in range(nb)) for _ in range(R_PAIRS))

    if nb == 1:
        n_full = t // R_TOK
        states = lax.fori_loop(
            0, n_full, lambda c, st: run(pl.ds(pl.multiple_of(c * R_TOK, R_TOK), R_TOK), R_TOK, st), states)
        if t % R_TOK:
            states = run(pl.ds(n_full * R_TOK, t % R_TOK), t % R_TOK, states)
    else:
        assert nb * t == R_TOK
        states = run(slice(None), t, states)
    for p in range(R_PAIRS):
        for b in range(nb):
            sout_ref[b, 2 * p] = states[p][b][:R_HEADSIZE, :R_HEADSIZE]
            sout_ref[b, 2 * p + 1] = states[p][b][R_HEADSIZE:, R_HEADSIZE:]


def _rwkv(r, k, v, wl, al, g, prm, group, wkv_state=None, y_into=None):
    nseq, t, row0, nb = group
    rb = nb * t
    r0 = row0 // rb
    assert row0 % rb == 0 and nseq % nb == 0
    has_state = wkv_state is not None
    gl = R_PAIRS * LANES
    tok = pl.BlockSpec((rb, gl), lambda s, hp_: (s + r0, hp_), **_single(not has_state))
    in_specs = [tok] * 6 + [pl.BlockSpec((SUBLANES, gl), lambda s, hp_: (0, hp_))]
    args = [r, k, v, wl, al, g, prm]
    if has_state:
        in_specs.append(pl.BlockSpec((None, nb, 2 * R_PAIRS, R_HEADSIZE, R_HEADSIZE),
                                     lambda s, hp_: (0, s, hp_, 0, 0)))
        args.append(wkv_state)
    body, aliases = _write_into(functools.partial(_rwkv_body, t=t, nb=nb, has_state=has_state),
                                in_specs, args, (y_into,))
    return pl.pallas_call(
        body,
        grid=(nseq // nb, R_HEADS // (2 * R_PAIRS)),
        in_specs=in_specs,
        out_specs=[pl.BlockSpec((rb, gl), lambda s, hp_: (s + r0, hp_)),
                   pl.BlockSpec((nb, 2 * R_PAIRS, R_HEADSIZE, R_HEADSIZE), lambda s, hp_: (s, hp_, 0, 0))],
        out_shape=[jax.ShapeDtypeStruct((r.shape[0], D_MODEL), BF16),
                   jax.ShapeDtypeStruct((nseq, R_HEADS, R_HEADSIZE, R_HEADSIZE), F32)],
        input_output_aliases=aliases,
        compiler_params=_cparams(2),
        name="rwkv_sample" if has_state else "rwkv_prompt",
    )(*args)


def _mix_body(u_ref, p_ref, mu_ref, *o_refs):
    u = u_ref[...]
    dx = p_ref[...] - u
    for i, o_ref in enumerate(o_refs):
        o_ref[...] = (u + dx * mu_ref[i:i + 1, :]).astype(o_ref.dtype)


def _rwkv_mix(u, prev, r_mu, tm=464):
    m, d = u.shape
    tm = _row_tile(m, tm)
    row = pl.BlockSpec((tm, d), lambda i: (i, 0))
    return pl.pallas_call(
        _mix_body,
        grid=(m // tm,),
        in_specs=[row, row, pl.BlockSpec((None, 6, d), lambda i: (0, 0, 0))],
        out_specs=[row] * 6,
        out_shape=[jax.ShapeDtypeStruct((m, d), BF16)] * 6,
        compiler_params=_cparams(1),
        name="rwkv_mix",
    )(u, prev, r_mu)


def _rwkv_layer(u, groups, state_wkv, state_shift, r_mu, r_w_r, r_w_k, r_w_v, r_w0, r_w1, r_w2, r_a0, r_a1,
                r_a2, r_g1, r_g2, r_k_k, r_k_a, r_r_k, r_ln_g, r_ln_b, r_w_o):
    gp, gs = groups
    mp, ms = gp[0] * gp[1], gs[0] * gs[1]
    up = u[:mp].reshape(gp[0], gp[1], D_MODEL)
    us = u[mp:].reshape(gs[0], gs[1], D_MODEL)
    prev = jnp.concatenate([
        jnp.concatenate([jnp.zeros((gp[0], 1, D_MODEL), F32), up[:, :-1]], axis=1).reshape(mp, D_MODEL),
        jnp.concatenate([state_shift[0][:, None, :], us[:, :-1]], axis=1).reshape(ms, D_MODEL)], axis=0)
    xr, xw, xk, xv, xa, xg = _rwkv_mix(u, prev, r_mu)
    r = _mm(xr, r_w_r, (0,), name="rwkv_r")
    k = _mm(xk, r_w_k, (0,), name="rwkv_k")
    v = _mm(xv, r_w_v, (0,), name="rwkv_v")
    wl = _mm(_mm(xw, r_w1, (0,), out_dtype=BF16, act="tanh", name="rwkv_w1"), r_w2, (0,), name="rwkv_w2")
    al = _mm(_mm(xa, r_a1, (0,), out_dtype=BF16, name="rwkv_a1"), r_a2, (0,), name="rwkv_a2")
    g = _mm(_mm(xg, r_g1, (0,), out_dtype=BF16, act="sigmoid", name="rwkv_g1"), r_g2, (0,), name="rwkv_g2")
    prm = jnp.concatenate([r_w0, r_a0, r_k_k, r_k_a, r_r_k.reshape(1, D_MODEL), r_ln_g, r_ln_b,
                           jnp.zeros((1, D_MODEL), F32)], axis=0)
    y, sp = _rwkv(r, k, v, wl, al, g, prm, gp)
    y, ss = _rwkv(r, k, v, wl, al, g, prm, gs, state_wkv, y_into=y)
    y = _mm(y, r_w_o, (0,), name="rwkv_out")
    return y, (sp, up[:, -1]), (ss, us[:, -1])


H_CHUNK = 128
H_SUB = 16
H_HPS = 2


def _gla_diag(q3, g3, k3, v3):
    sub = q3.shape[1]
    tpos = _iota((1, sub, 1), 1)
    y3 = jnp.zeros(v3.shape, F32)
    for s in range(sub):
        arg = jnp.where(tpos >= s, g3 - g3[:, s:s + 1, :], -jnp.inf)
        a_s = jnp.sum(q3 * k3[:, s:s + 1, :] * jnp.exp(arg), axis=-1, keepdims=True)
        yield
        y3 = y3 + a_s * v3[:, s:s + 1, :]
    return y3


def _gla_prompt_chunk(nv, q, logf, k, v, st):
    n = -(-nv // H_SUB) * H_SUB
    q, logf, k, v = (_pad_rows(t_, n) for t_ in (q, logf, k, v))
    causal = _iota((n, n), 0) >= _iota((n, n), 1)
    gcum = _xdot_left(causal.astype(BF16), logf)
    yield
    y = _dot_nt(q * jnp.exp(gcum), st)
    g_n = gcum[n - 1:n]
    upd = _dot_tn(v, k * jnp.exp(g_n - gcum))
    atts = []
    for i in range(1, n // H_SUB):
        lo = i * H_SUB
        gs = gcum[lo - 1:lo]
        atts.append(_dot_nt(q[lo:lo + H_SUB] * jnp.exp(gcum[lo:lo + H_SUB] - gs),
                            k[:lo] * jnp.exp(gs - gcum[:lo])))
    yield
    st = st * jnp.exp(g_n) + upd
    offs = [jnp.zeros((H_SUB, LANES), F32)] + [_dot(att, v[:(i + 1) * H_SUB]) for i, att in enumerate(atts)]
    shape3 = (n // H_SUB, H_SUB, LANES)
    y3 = yield from _gla_diag(q.reshape(shape3), gcum.reshape(shape3), k.reshape(shape3), v.reshape(shape3))
    off = offs[0] if len(offs) == 1 else jnp.concatenate(offs, axis=0)
    return (y + off + y3.reshape(n, LANES))[:nv], st


def _gla_sample_chunk(nb, q, logf, k, v, sts):
    n = SUBLANES
    ntok = nb * n
    _, causal = _seq_masks(ntok, n)
    gcum = _xdot_left(causal.astype(BF16), logf)
    yield
    shape3 = (nb, n, LANES)
    g3 = gcum.reshape(shape3)
    g_tot = g3[:, n - 1:n, :]
    qe = q * jnp.exp(gcum)
    kd = (k.reshape(shape3) * jnp.exp(g_tot - g3)).reshape(ntok, LANES)
    ys = [_dot_nt(_pad_rows(qe[b * n:(b + 1) * n], BF16_ROWS), sts[b])[:n] for b in range(nb)]
    upds = [_dot_tn(_pad_rows(v[b * n:(b + 1) * n], BF16_ROWS), _pad_rows(kd[b * n:(b + 1) * n], BF16_ROWS))
            for b in range(nb)]
    y3 = yield from _gla_diag(q.reshape(shape3), g3, k.reshape(shape3), v.reshape(shape3))
    new = [sts[b] * jnp.exp(g_tot[b]) + upds[b] for b in range(nb)]
    return y3.reshape(ntok, LANES) + jnp.concatenate(ys, axis=0), new


def _hgrn_body(*refs, t, nb, has_state, layer):
    if has_state:
        (q_ref, f_ref, i_ref, g_ref, lb_ref, ng_ref, sin_ref, y_ref, sout_ref) = refs
    else:
        (q_ref, f_ref, i_ref, g_ref, lb_ref, ng_ref, y_ref, sout_ref) = refs
    hl = lb_ref[...]
    e = jnp.exp(hl - jnp.max(hl, axis=0, keepdims=True))
    sm = e / jnp.sum(e, axis=0, keepdims=True)
    lb_all = (jnp.sum(sm[1:layer + 1], axis=0, keepdims=True) if layer > 0
              else jnp.zeros((1, H_HPS * LANES), F32))
    ng = ng_ref[...]

    def lanes(hi):
        return slice(hi * LANES, (hi + 1) * LANES)

    def gates(rows, hi):
        lb = lb_all[:, lanes(hi)]
        q = _silu(q_ref[rows, lanes(hi)])
        f = lb + (1.0 - lb) * _sigmoid(f_ref[rows, lanes(hi)])
        return q, jnp.log(f), 1.0 - f, i_ref[rows, lanes(hi)]

    def finish(rows, hi, yy):
        yy = yy * lax.rsqrt(jnp.mean(yy * yy, axis=-1, keepdims=True) + EPS) * ng
        y_ref[rows, lanes(hi)] = (yy * _silu(g_ref[rows, lanes(hi)])).astype(y_ref.dtype)

    if has_state:
        rows = slice(None)
        res = _lockstep([_gla_sample_chunk(nb, *gates(rows, hi), [sin_ref[b, hi].T for b in range(nb)])
                         for hi in range(H_HPS)])
        for hi, (yy, sts) in enumerate(res):
            finish(rows, hi, yy)
            for b in range(nb):
                sout_ref[b, hi] = sts[b].T
        return

    def run(start, nv, sts):
        rows = pl.ds(pl.multiple_of(start, SUBLANES), nv)
        res = _lockstep([_gla_prompt_chunk(nv, *gates(rows, hi), sts[hi]) for hi in range(H_HPS)])
        for hi, (yy, _) in enumerate(res):
            finish(rows, hi, yy)
        return tuple(st for _, st in res)

    sts = tuple(jnp.zeros((H_VDIM, H_EXPAND), F32) for _ in range(H_HPS))
    n_full = t // H_CHUNK
    sts = lax.fori_loop(0, n_full, lambda c, s: run(c * H_CHUNK, H_CHUNK, s), sts)
    if t % H_CHUNK:
        sts = run(n_full * H_CHUNK, t % H_CHUNK, sts)
    for hi in range(H_HPS):
        sout_ref[0, hi] = sts[hi].T


def _hgrn(layer, proj, h_lb, h_norm_g, group, state=None, y_into=None):
    nseq, t, row0, nb = group
    rb = nb * t
    r0 = row0 // rb
    assert row0 % rb == 0 and nseq % nb == 0
    has_state = state is not None
    nhb = H_HEADS // H_HPS
    hw = H_HPS * LANES

    def tok(section):
        return pl.BlockSpec((rb, hw), lambda s, h: (s + r0, h + section * nhb))

    in_specs = [tok(0), tok(1), tok(2), tok(3),
                pl.BlockSpec((DEPTH, hw), lambda s, h: (0, h)),
                pl.BlockSpec((1, LANES), lambda s, h: (0, 0))]
    args = [proj, proj, proj, proj, h_lb, h_norm_g]
    if has_state:
        in_specs.append(pl.BlockSpec((None, nb, H_HPS, H_EXPAND, H_VDIM), lambda s, h: (0, s, h, 0, 0)))
        args.append(state)
    body, aliases = _write_into(functools.partial(_hgrn_body, t=t, nb=nb, has_state=has_state, layer=layer),
                                in_specs, args, (y_into,))
    return pl.pallas_call(
        body,
        grid=(nseq // nb, nhb),
        in_specs=in_specs,
        out_specs=[pl.BlockSpec((rb, hw), lambda s, h: (s + r0, h)),
                   pl.BlockSpec((nb, H_HPS, H_EXPAND, H_VDIM), lambda s, h: (s, h, 0, 0))],
        out_shape=[jax.ShapeDtypeStruct((proj.shape[0], D_MODEL), BF16),
                   jax.ShapeDtypeStruct((nseq, H_HEADS, H_EXPAND, H_VDIM), F32)],
        input_output_aliases=aliases,
        compiler_params=_cparams(2),
        name="hgrn_sample" if has_state else "hgrn_prompt",
    )(*args)


def _hgrn_layer(layer, u, groups, state_hgrn, h_w_in, h_lb, h_norm_g, h_w_out):
    proj = _mm(u, h_w_in, (0,), name="hgrn_in")
    y, sp = _hgrn(layer, proj, h_lb, h_norm_g, groups[0])
    y, ss = _hgrn(layer, proj, h_lb, h_norm_g, groups[1], state_hgrn, y_into=y)
    y = _mm(y, h_w_out, (0,), name="hgrn_out")
    return y, sp, ss


def kernel(x_prompt, x_sample, state_ssm, state_conv, state_wkv, state_shift, state_hgrn, meta_tokens, norm_g, ffn_w_gate, ffn_w_up, ffn_w_down, m_w_in, m_conv_w, m_conv_b, m_dt_bias, m_a_log, m_d, m_norm_g, m_w_out, r_mu, r_w_r, r_w_k, r_w_v, r_w0, r_w1, r_w2, r_a0, r_a1, r_a2, r_g1, r_g2, r_k_k, r_k_a, r_r_k, r_ln_g, r_ln_b, r_w_o, h_w_in, h_lb, h_norm_g, h_w_out):
    meta = jnp.broadcast_to(meta_tokens[None], (BATCH, N_META, D_MODEL))
    h = jnp.concatenate([jnp.concatenate([meta, x_prompt], axis=1).reshape(M_P, D_MODEL),
                         x_sample.reshape(M_S, D_MODEL)], axis=0)
    conv_pad = jnp.pad(state_conv, ((0, 0), (0, 0), (SUBLANES - (M_CONV - 1), 0), (0, 0)))

    p_conv, s_conv = [], []
    ssm = (None, None)
    u = _norm(h, norm_g, 0, 0)
    for l in range(DEPTH):
        kind, j = l % 3, l // 3
        y = _mm(_swiglu_up(u, ffn_w_gate, ffn_w_up, (l, 0)), ffn_w_down, (l, 0), tm=464, name="ffn_down")
        h, u = _resnorm(h, y, norm_g, 0.5, l, 1, l, 2, u_dtype=F32 if kind == 1 else BF16)
        if kind == 0:
            y, (sp, cp), (ss, cs) = _mamba_layer(j, u, GROUPS, m_w_in, m_conv_w, m_conv_b, m_dt_bias, m_a_log,
                                                 m_d, m_norm_g, m_w_out, conv_pad, state_ssm, ssm)
            ssm = (sp, ss)
            p_conv.append(cp)
            s_conv.append(cs)
        elif kind == 1:
            y, (p_wkv, p_shift), (s_wkv, s_shift) = _rwkv_layer(
                u, GROUPS, state_wkv, state_shift, r_mu, r_w_r, r_w_k, r_w_v, r_w0, r_w1, r_w2, r_a0, r_a1,
                r_a2, r_g1, r_g2, r_k_k, r_k_a, r_r_k, r_ln_g, r_ln_b, r_w_o)
        else:
            y, p_hgrn, s_hgrn = _hgrn_layer(l, u, GROUPS, state_hgrn, h_w_in, h_lb, h_norm_g, h_w_out)
        h, u = _resnorm(h, y, norm_g, 1.0, l, 3, l, 4)
        y = _mm(_swiglu_up(u, ffn_w_gate, ffn_w_up, (l, 1)), ffn_w_down, (l, 1), tm=464, name="ffn_down")
        if l + 1 < DEPTH:
            h, u = _resnorm(h, y, norm_g, 0.5, l, 5, l + 1, 0)
        else:
            h, _ = _resnorm(h, y, norm_g, 0.5, l, 5)

    y_prompt = jnp.stack([h[b * T_P + N_META:(b + 1) * T_P] for b in range(BATCH)])
    y_sample = h[M_P:].reshape(DEC_BATCH, T_S, D_MODEL)
    return (y_prompt, y_sample,
            ssm[0], jnp.stack(p_conv), p_wkv[None], p_shift[None], p_hgrn[None],
            ssm[1], jnp.stack(s_conv), s_wkv[None], s_shift[None], s_hgrn[None])
```

```python
import functools

import jax
import jax.numpy as jnp
from jax import lax
from jax.experimental import pallas as pl
from jax.experimental.pallas import tpu as pltpu

F32 = jnp.float32
BF16 = jnp.bfloat16

D_MODEL = 2048
BATCH = 4
SEQ = 2048
DEPTH = 4
DEC_BATCH = 128
DEC_SEQ = 8
N_META = 16
EPS = 1e-6
D_FF = 5632

M_D_INNER = 2 * D_MODEL
M_HEADDIM = 64
M_HEADS = M_D_INNER // M_HEADDIM
M_GROUPS = 8
M_D_STATE = 128
M_CONV = 4
M_CONV_DIM = M_D_INNER + 2 * M_GROUPS * M_D_STATE
M_NORM_EPS = 1e-5
M_GW = M_D_INNER // M_GROUPS
M_R = M_HEADS // M_GROUPS

R_HEADSIZE = 64
R_HEADS = D_MODEL // R_HEADSIZE
R_LN_EPS = 64e-5

H_HEADS = 16
H_EXPAND = 128
H_FDIM = H_HEADS * H_EXPAND
H_VDIM = D_MODEL // H_HEADS

T_P = N_META + SEQ
T_S = DEC_SEQ
M_P = BATCH * T_P
M_S = DEC_BATCH * T_S
M_TOK = M_P + M_S
S_BS = 8
LANES = 128
SUBLANES = 8
BF16_ROWS = 16
VMEM_LIMIT = 56 * 1024 * 1024
GROUPS = ((BATCH, T_P, 0, 1), (DEC_BATCH, T_S, M_P, S_BS))


def _cparams(n_axes):
    return pltpu.CompilerParams(dimension_semantics=("arbitrary",) * n_axes,
                                vmem_limit_bytes=VMEM_LIMIT)


def _silu(x):
    h = 0.5 * x
    return h + h * jnp.tanh(h)


def _sigmoid(x):
    return 1.0 / (1.0 + jnp.exp(-x))


def _softplus(x):
    return jnp.maximum(x, 0.0) + jnp.log1p(jnp.exp(-jnp.abs(x)))


def _iota(shape, dim):
    return lax.broadcasted_iota(jnp.int32, shape, dim)


def _log2(n):
    assert n & (n - 1) == 0
    return n.bit_length() - 1


def _dot(a, b):
    return jnp.dot(a.astype(BF16), b.astype(BF16), preferred_element_type=F32)


def _dot_nt(a, b):
    return lax.dot_general(a.astype(BF16), b.astype(BF16), (((1,), (1,)), ((), ())),
                           preferred_element_type=F32)


def _dot_tn(a, b):
    return lax.dot_general(a.astype(BF16), b.astype(BF16), (((0,), (0,)), ((), ())),
                           preferred_element_type=F32)


def _split(x, terms):
    parts = []
    rem = x
    for _ in range(terms):
        hi = rem.astype(BF16)
        parts.append(hi)
        rem = rem - hi.astype(F32)
    return parts


def _xdot(x, sel, terms=3):
    return sum(jnp.dot(p, sel, preferred_element_type=F32) for p in _split(x, terms))


def _xdot_left(sel, x, terms=3):
    return sum(jnp.dot(sel, p, preferred_element_type=F32) for p in _split(x, terms))


def _xdot_nt(sel, x, terms=3):
    return sum(lax.dot_general(sel, p, (((1,), (1,)), ((), ())), preferred_element_type=F32)
               for p in _split(x, terms))


def _pad_rows(x, n):
    if x.shape[0] >= n:
        return x
    return jnp.concatenate([x, jnp.zeros((n - x.shape[0],) + x.shape[1:], x.dtype)], axis=0)


def _seq_masks(ntok, n):
    r, c = _iota((ntok, ntok), 0), _iota((ntok, ntok), 1)
    same = (r >> _log2(n)) == (c >> _log2(n))
    return same, same & (r >= c)


def _lockstep(gens):
    results = [None] * len(gens)
    live = list(enumerate(gens))
    while live:
        still = []
        for i, gen in live:
            try:
                next(gen)
                still.append((i, gen))
            except StopIteration as done:
                results[i] = done.value
        live = still
    return results


def _write_into(body, in_specs, args, targets):
    n_in = len(args)
    aliases = {}
    for k, arr in enumerate(targets):
        if arr is not None:
            aliases[len(args)] = k
            in_specs.append(pl.BlockSpec(memory_space=pl.ANY))
            args.append(arr)
    n_extra = len(args) - n_in

    def wrapped(*refs):
        return body(*refs[:n_in], *refs[n_in + n_extra:])

    return (wrapped if n_extra else body), aliases


def _row_tile(m, want):
    return next(d for d in range(want - want % BF16_ROWS, 0, -BF16_ROWS) if m % d == 0)


def _single(mode):
    return dict(pipeline_mode=pl.Buffered(1)) if mode else {}


def _mm_body(x_ref, w_ref, o_ref, wbf_ref, *, act, valid_cols, w_t):
    @pl.when(pl.program_id(1) == 0)
    def _():
        w = w_ref[...].T if w_t else w_ref[...]
        if valid_cols is not None:
            w = jnp.where(_iota(w.shape, 1) < valid_cols, w, 0.0)
        wbf_ref[...] = w.astype(BF16)

    acc = jnp.dot(x_ref[...].astype(BF16), wbf_ref[...], preferred_element_type=F32)
    if act == "tanh":
        acc = jnp.tanh(acc)
    elif act == "sigmoid":
        acc = _sigmoid(acc)
    o_ref[...] = acc.astype(o_ref.dtype)


def _mm(x, w, lead=(), col0=0, ncols=None, tn=512, tm=1856, out_dtype=F32, act=None, w_t=False, name="mm"):
    m, k = x.shape
    n_all = w.shape[-2] if w_t else w.shape[-1]
    ncols = n_all if ncols is None else ncols
    tn = min(tn, ncols)
    tm = _row_tile(m, tm)
    assert ncols % tn == 0 and col0 % tn == 0 and m % tm == 0 and w.shape[-1 if w_t else -2] == k
    valid_cols = None
    if col0 + ncols > n_all:
        assert ncols == tn
        valid_cols = n_all - col0
    jb = col0 // tn
    if w_t:
        wspec = pl.BlockSpec((None,) * len(lead) + (tn, k), lambda j, i: lead + (j + jb, 0))
    else:
        wspec = pl.BlockSpec((None,) * len(lead) + (k, tn), lambda j, i: lead + (0, j + jb))
    return pl.pallas_call(
        functools.partial(_mm_body, act=act, valid_cols=valid_cols, w_t=w_t),
        grid=(ncols // tn, m // tm),
        in_specs=[pl.BlockSpec((tm, k), lambda j, i: (i, 0)), wspec],
        out_specs=pl.BlockSpec((tm, tn), lambda j, i: (i, j)),
        out_shape=jax.ShapeDtypeStruct((m, ncols), out_dtype),
        scratch_shapes=[pltpu.VMEM((k, tn), BF16)],
        compiler_params=_cparams(2),
        name=name,
    )(x, w)


def _swiglu_up_body(x_ref, wg_ref, wu_ref, o_ref, wg_bf, wu_bf):
    @pl.when(pl.program_id(1) == 0)
    def _():
        wg_bf[...] = wg_ref[...].astype(BF16)
        wu_bf[...] = wu_ref[...].astype(BF16)

    x = x_ref[...]
    g = jnp.dot(x, wg_bf[...], preferred_element_type=F32)
    u = jnp.dot(x, wu_bf[...], preferred_element_type=F32)
    o_ref[...] = (_silu(g) * u).astype(o_ref.dtype)


def _swiglu_up(x, w_gate, w_up, lead, tn=512, tm=1856):
    m, k = x.shape
    n = w_gate.shape[-1]
    tm = _row_tile(m, tm)
    wspec = pl.BlockSpec((None,) * len(lead) + (k, tn), lambda j, i: lead + (0, j))
    return pl.pallas_call(
        _swiglu_up_body,
        grid=(n // tn, m // tm),
        in_specs=[pl.BlockSpec((tm, k), lambda j, i: (i, 0)), wspec, wspec],
        out_specs=pl.BlockSpec((tm, tn), lambda j, i: (i, j)),
        out_shape=jax.ShapeDtypeStruct((m, n), BF16),
        scratch_shapes=[pltpu.VMEM((k, tn), BF16), pltpu.VMEM((k, tn), BF16)],
        compiler_params=_cparams(2),
        name="swiglu_up",
    )(x, w_gate, w_up)


def _rms(x, g):
    return x * lax.rsqrt(jnp.mean(x * x, axis=-1, keepdims=True) + EPS) * g


def _norm_body(h_ref, g_ref, u_ref, *, gi):
    u_ref[...] = _rms(h_ref[...], g_ref[gi:gi + 1, :]).astype(u_ref.dtype)


def _norm(h, norm_g, layer, gi, out_dtype=BF16, tm=464):
    m, d = h.shape
    tm = _row_tile(m, tm)
    return pl.pallas_call(
        functools.partial(_norm_body, gi=gi),
        grid=(m // tm,),
        in_specs=[pl.BlockSpec((tm, d), lambda i: (i, 0)),
                  pl.BlockSpec((None, 6, d), lambda i: (layer, 0, 0))],
        out_specs=pl.BlockSpec((tm, d), lambda i: (i, 0)),
        out_shape=jax.ShapeDtypeStruct((m, d), out_dtype),
        compiler_params=_cparams(1),
        name="norm",
    )(h, norm_g)


def _resnorm_body(h_ref, y_ref, ga_ref, gb_ref, hn_ref, u_ref, *, scale, ia, ib):
    hn = h_ref[...] + scale * _rms(y_ref[...], ga_ref[ia:ia + 1, :])
    hn_ref[...] = hn
    if u_ref is not None:
        u_ref[...] = _rms(hn, gb_ref[ib:ib + 1, :]).astype(u_ref.dtype)


def _resnorm(h, y, norm_g, scale, la, ia, lb=None, ib=None, u_dtype=BF16, tm=464):
    m, d = h.shape
    tm = _row_tile(m, tm)
    row = pl.BlockSpec((tm, d), lambda i: (i, 0))
    with_u = lb is not None
    body = functools.partial(_resnorm_body, scale=scale, ia=ia, ib=ib)
    if not with_u:
        body = functools.partial(_resnorm_body, u_ref=None, scale=scale, ia=ia, ib=ib)
        lb = la
    out_shape = [jax.ShapeDtypeStruct((m, d), F32)]
    out_specs = [row]
    if with_u:
        out_shape.append(jax.ShapeDtypeStruct((m, d), u_dtype))
        out_specs.append(row)
    res = pl.pallas_call(
        body,
        grid=(m // tm,),
        in_specs=[row, row,
                  pl.BlockSpec((None, 6, d), lambda i: (la, 0, 0)),
                  pl.BlockSpec((None, 6, d), lambda i: (lb, 0, 0))],
        out_specs=out_specs,
        out_shape=out_shape,
        compiler_params=_cparams(1),
        name="resnorm",
    )(h, y, norm_g, norm_g)
    return res if with_u else (res[0], None)


SSD_CHUNK = 128
SSD_GPS = 2


def _conv_taps(cur, shifted, w, b):
    out = b + cur * w[M_CONV - 1:M_CONV]
    for k in range(1, M_CONV):
        out = out + shifted(k) * w[M_CONV - 1 - k:M_CONV - k]
    return _silu(out)


def _conv_silu_seq(cur, prv, w, b):
    row = _iota((SUBLANES, cur.shape[1]), 0)

    def shifted(k):
        sh = pltpu.roll(cur, k, 0)
        top = jnp.where(row < k, pltpu.roll(prv, k, 0), sh[:SUBLANES])
        return jnp.concatenate([top, sh[SUBLANES:]], axis=0)

    return _conv_taps(cur, shifted, w, b)


def _conv_silu_batch(cur, stt, w, b):
    ntok = cur.shape[0]
    tpos = _iota((ntok, 1), 0) & (SUBLANES - 1)

    def shifted(k):
        return jnp.where(tpos < k, pltpu.roll(stt, (k - SUBLANES) % ntok, 0), pltpu.roll(cur, k, 0))

    return _conv_taps(cur, shifted, w, b)


def _expand_heads(rows):
    head_of_lane = _iota((M_R, M_GW), 1) >> _log2(M_HEADDIM)
    sel = (head_of_lane == _iota((M_R, M_GW), 0)).astype(BF16)
    return _xdot(rows, sel, terms=2)


def _ssd_common(x, bm, cm, dtr, hp, causal, same, nv, extra_rows):
    ntok = x.shape[0]
    bias, a = hp[0:1], -jnp.exp(hp[1:2])
    dt = _softplus(dtr + bias)
    if nv < ntok and same is None:
        dt = jnp.where(_iota((ntok, 1), 0) < nv, dt, 0.0)
    da = dt * a
    cum = _xdot_left(causal.astype(BF16), da)
    cum_tot = cum[ntok - 1:ntok] if same is None else _xdot_left(same.astype(BF16), da)
    cbm = _dot_nt(cm, bm)
    eye8 = (_iota((M_R, M_R), 0) == _iota((M_R, M_R), 1)).astype(BF16)
    dt_t = _xdot_nt(eye8, dt)
    even = (_iota((1, M_GW), 1) & M_HEADDIM) == 0
    x_e = jnp.where(even, x, 0.0).astype(BF16)
    x_o = jnp.where(even, 0.0, x).astype(BF16)
    yield
    cum_t = _xdot_nt(eye8, cum)
    tail = jnp.exp(cum_tot - cum) * dt
    ex = _expand_heads(jnp.concatenate([jnp.exp(cum), tail, extra_rows(cum_tot)], axis=0))
    yield
    parts = []
    for q in range(M_R // 2):
        acc = None
        for r, x_m in ((2 * q, x_e), (2 * q + 1, x_o)):
            dec = jnp.exp(jnp.where(causal, cum[:, r:r + 1] - cum_t[r:r + 1, :], -jnp.inf))
            sc = (cbm * dec * dt_t[r:r + 1, :]).astype(BF16)
            yy = jnp.dot(sc, x_m[:, q * LANES:(q + 1) * LANES], preferred_element_type=F32)
            acc = yy if acc is None else acc + yy
        parts.append(acc)
    return jnp.concatenate(parts, axis=1), ex, cum_tot


def _ssd_finish(y, x, zc, d_row, ng):
    yv = (y + d_row * x) * _silu(zc)
    return yv * lax.rsqrt(jnp.mean(yv * yv, axis=-1, keepdims=True) + M_NORM_EPS) * ng


def _ssd_prompt_chunk(nv, zc, xr, br, cr, dtr, prev, st, cw, cb, hp, ng):
    n = SSD_CHUNK
    new_prev = (xr[nv - SUBLANES:], br[nv - SUBLANES:], cr[nv - SUBLANES:])
    zc, xr, br, cr, dtr = (_pad_rows(v, n) for v in (zc, xr, br, cr, dtr))
    x = _conv_silu_seq(xr, prev[0], cw[0], cb[0])
    bm = _conv_silu_seq(br, prev[1], cw[1], cb[1])
    cm = _conv_silu_seq(cr, prev[2], cw[2], cb[2])
    causal = _iota((n, n), 0) >= _iota((n, n), 1)
    row = _iota((SUBLANES, M_R), 0)

    def extra_rows(cum_tot):
        return jnp.where(row == 0, jnp.exp(cum_tot), jnp.where(row == 1, hp[2:3], 0.0))

    y_state = _dot(cm, st)
    y, ex, _ = yield from _ssd_common(x, bm, cm, dtr, hp, causal, None, nv, extra_rows)
    upd = _dot_tn(bm, ex[n:2 * n] * x)
    yield
    y = y + y_state * ex[:n]
    st = st * ex[2 * n:2 * n + 1] + upd
    return _ssd_finish(y, x, zc, ex[2 * n + 1:2 * n + 2], ng)[:nv], new_prev, st


def _ssd_sample_chunk(nb, zc, xr, br, cr, dtr, stt, states, cw, cb, hp, ng):
    n = SUBLANES
    ntok = nb * n
    x = _conv_silu_batch(xr, stt[0], cw[0], cb[0])
    bm = _conv_silu_batch(br, stt[1], cw[1], cb[1])
    cm = _conv_silu_batch(cr, stt[2], cw[2], cb[2])
    same, causal = _seq_masks(ntok, n)
    row = _iota((SUBLANES, M_R), 0)
    ys = [_dot_nt(_pad_rows(cm[b * n:(b + 1) * n], BF16_ROWS), states[b])[:n] for b in range(nb)]
    y, ex, cum_tot = yield from _ssd_common(x, bm, cm, dtr, hp, causal, same, n,
                                            lambda _: jnp.where(row == 0, hp[2:3], 0.0))
    xt = ex[ntok:2 * ntok] * x
    upds = [_dot_tn(_pad_rows(xt[b * n:(b + 1) * n], BF16_ROWS), _pad_rows(bm[b * n:(b + 1) * n], BF16_ROWS))
            for b in range(nb)]
    yield
    etot = jnp.exp(cum_tot)
    new_states = []
    for b in range(nb):
        decay = jnp.concatenate([jnp.broadcast_to(etot[b * n:b * n + 1, r:r + 1], (M_HEADDIM, M_D_STATE))
                                 for r in range(M_R)], axis=0)
        new_states.append(states[b] * decay + upds[b])
    y = y + jnp.concatenate(ys, axis=0) * ex[:ntok]
    return _ssd_finish(y, x, zc, ex[2 * ntok:2 * ntok + 1], ng), new_states


def _ssd_body(*refs, t, nb, has_state):
    if has_state:
        (z_ref, x_ref, b_ref, c_ref, dt_ref, cwx_ref, cwb_ref, cwc_ref, cbx_ref, cbb_ref, cbc_ref,
         hp_ref, ng_ref, csx_ref, csb_ref, csc_ref, sin_ref, y_ref, sout_ref) = refs
    else:
        (z_ref, x_ref, b_ref, c_ref, dt_ref, cwx_ref, cwb_ref, cwc_ref, cbx_ref, cbb_ref, cbc_ref,
         hp_ref, ng_ref, y_ref, sout_ref) = refs

    def consts(gi):
        wide = slice(gi * M_GW, (gi + 1) * M_GW)
        nar = slice(gi * M_D_STATE, (gi + 1) * M_D_STATE)
        cw = (cwx_ref[:, wide], cwb_ref[:, nar], cwc_ref[:, nar])
        cb = (cbx_ref[:, wide], cbb_ref[:, nar], cbc_ref[:, nar])
        return wide, nar, cw, cb, hp_ref[gi], ng_ref[:, wide]

    if has_state:
        ntok = nb * t
        gens = []
        for gi in range(SSD_GPS):
            wide, nar, cw, cb, hp, ng = consts(gi)
            stt = (csx_ref[:, :, wide].reshape(ntok, M_GW), csb_ref[:, :, nar].reshape(ntok, M_D_STATE),
                   csc_ref[:, :, nar].reshape(ntok, M_D_STATE))
            states = [sin_ref[b, gi * M_R:(gi + 1) * M_R].reshape(M_GW, M_D_STATE) for b in range(nb)]
            gens.append(_ssd_sample_chunk(nb, z_ref[:, wide], x_ref[:, wide], b_ref[:, nar], c_ref[:, nar],
                                          dt_ref[gi], stt, states, cw, cb, hp, ng))
        for gi, (out, states) in enumerate(_lockstep(gens)):
            y_ref[:, gi * M_GW:(gi + 1) * M_GW] = out.astype(y_ref.dtype)
            for b in range(nb):
                sout_ref[b, gi * M_R:(gi + 1) * M_R] = states[b].reshape(M_R, M_HEADDIM, M_D_STATE)
        return

    def run(start, nv, carry):
        rows = pl.ds(pl.multiple_of(start, SUBLANES), nv)
        gens = []
        for gi in range(SSD_GPS):
            wide, nar, cw, cb, hp, ng = consts(gi)
            prev, st = carry[gi]
            gens.append(_ssd_prompt_chunk(nv, z_ref[rows, wide], x_ref[rows, wide], b_ref[rows, nar],
                                          c_ref[rows, nar], dt_ref[gi, rows, :], prev, st, cw, cb, hp, ng))
        new = []
        for gi, (out, prev, st) in enumerate(_lockstep(gens)):
            y_ref[rows, gi * M_GW:(gi + 1) * M_GW] = out.astype(y_ref.dtype)
            new.append((prev, st))
        return tuple(new)

    zero_prev = (jnp.zeros((SUBLANES, M_GW), F32), jnp.zeros((SUBLANES, M_D_STATE), F32),
                 jnp.zeros((SUBLANES, M_D_STATE), F32))
    carry = tuple((zero_prev, jnp.zeros((M_D_STATE, M_GW), F32)) for _ in range(SSD_GPS))
    n_full = t // SSD_CHUNK
    carry = lax.fori_loop(0, n_full, lambda c, cr: run(c * SSD_CHUNK, SSD_CHUNK, cr), carry)
    if t % SSD_CHUNK:
        carry = run(n_full * SSD_CHUNK, t % SSD_CHUNK, carry)
    for gi in range(SSD_GPS):
        sout_ref[0, gi * M_R:(gi + 1) * M_R] = carry[gi][1].T.reshape(M_R, M_HEADDIM, M_D_STATE)


def _ssd(j, z, xbc, dtg, m_conv_w, conv_b3, hp, m_norm_g3, group, conv_state=None, ssm_state=None,
         y_into=None, state_into=None):
    nseq, t, row0, nb = group
    rb = nb * t
    r0 = row0 // rb
    assert row0 % rb == 0 and nseq % nb == 0 and M_GROUPS % SSD_GPS == 0
    has_state = ssm_state is not None
    gw, gn, gr = SSD_GPS * M_GW, SSD_GPS * M_D_STATE, SSD_GPS * M_R
    b0 = M_D_INNER // gn
    c0 = b0 + M_GROUPS // SSD_GPS
    one = _single(not has_state)
    in_specs = [
        pl.BlockSpec((rb, gw), lambda s, g: (s + r0, g), **one),
        pl.BlockSpec((rb, gw), lambda s, g: (s + r0, g), **one),
        pl.BlockSpec((rb, gn), lambda s, g: (s + r0, b0 + g), **one),
        pl.BlockSpec((rb, gn), lambda s, g: (s + r0, c0 + g), **one),
        pl.BlockSpec((SSD_GPS, rb, M_R), lambda s, g: (g, s + r0, 0), **one),
        pl.BlockSpec((None, M_CONV, gw), lambda s, g: (j, 0, g)),
        pl.BlockSpec((None, M_CONV, gn), lambda s, g: (j, 0, b0 + g)),
        pl.BlockSpec((None, M_CONV, gn), lambda s, g: (j, 0, c0 + g)),
        pl.BlockSpec((None, 1, gw), lambda s, g: (j, 0, g)),
        pl.BlockSpec((None, 1, gn), lambda s, g: (j, 0, b0 + g)),
        pl.BlockSpec((None, 1, gn), lambda s, g: (j, 0, c0 + g)),
        pl.BlockSpec((None, SSD_GPS, 3, M_R), lambda s, g: (j, g, 0, 0)),
        pl.BlockSpec((None, 1, gw), lambda s, g: (j, 0, g)),
    ]
    args = [z, xbc, xbc, xbc, dtg, m_conv_w, m_conv_w, m_conv_w, conv_b3, conv_b3, conv_b3, hp, m_norm_g3]
    if has_state:
        in_specs += [
            pl.BlockSpec((None, nb, SUBLANES, gw), lambda s, g: (j, s, 0, g)),
            pl.BlockSpec((None, nb, SUBLANES, gn), lambda s, g: (j, s, 0, b0 + g)),
            pl.BlockSpec((None, nb, SUBLANES, gn), lambda s, g: (j, s, 0, c0 + g)),
            pl.BlockSpec((None, nb, gr, M_HEADDIM, M_D_STATE), lambda s, g: (j, s, g, 0, 0)),
        ]
        args += [conv_state, conv_state, conv_state, ssm_state]
    body, aliases = _write_into(functools.partial(_ssd_body, t=t, nb=nb, has_state=has_state),
                                in_specs, args, (y_into, state_into))
    return pl.pallas_call(
        body,
        grid=(nseq // nb, M_GROUPS // SSD_GPS),
        in_specs=in_specs,
        out_specs=[pl.BlockSpec((rb, gw), lambda s, g: (s + r0, g)),
                   pl.BlockSpec((None, nb, gr, M_HEADDIM, M_D_STATE), lambda s, g: (j, s, g, 0, 0))],
        out_shape=[jax.ShapeDtypeStruct((z.shape[0], M_D_INNER), BF16),
                   jax.ShapeDtypeStruct((hp.shape[0], nseq, M_HEADS, M_HEADDIM, M_D_STATE), F32)],
        input_output_aliases=aliases,
        compiler_params=_cparams(2),
        name="ssd_sample" if has_state else "ssd_prompt",
    )(*args)


def _mamba_layer(j, u, groups, m_w_in, m_conv_w, m_conv_b, m_dt_bias, m_a_log, m_d, m_norm_g, m_w_out,
                 state_conv_pad, state_ssm, ssm_into=(None, None)):
    m = u.shape[0]
    gp, gs = groups
    w_in_t = jnp.swapaxes(m_w_in, 1, 2)
    z = _mm(u, w_in_t, (j,), 0, M_D_INNER, w_t=True, name="ssd_in_z")
    xbc = _mm(u, w_in_t, (j,), M_D_INNER, M_CONV_DIM, w_t=True, name="ssd_in_xbc")
    dt = _mm(u, w_in_t, (j,), M_D_INNER + M_CONV_DIM, LANES, tn=LANES, w_t=True, name="ssd_in_dt")[:, :M_HEADS]
    dtg = dt.reshape(m, M_GROUPS, M_R).transpose(1, 0, 2)
    hp = jnp.stack([m_dt_bias, m_a_log, m_d], axis=1).reshape(-1, 3, M_GROUPS, M_R).transpose(0, 2, 1, 3)
    conv_b3 = m_conv_b[:, None, :]
    ng3 = m_norm_g[:, None, :]
    y, sp = _ssd(j, z, xbc, dtg, m_conv_w, conv_b3, hp, ng3, gp, state_into=ssm_into[0])
    y, ss = _ssd(j, z, xbc, dtg, m_conv_w, conv_b3, hp, ng3, gs, state_conv_pad, state_ssm,
                 y_into=y, state_into=ssm_into[1])
    y = _mm(y, m_w_out, (j,), tm=928, name="ssd_out")
    keep = M_CONV - 1
    xp = jnp.stack([xbc[(b + 1) * gp[1] - keep:(b + 1) * gp[1]] for b in range(gp[0])])
    xs = xbc[gp[0] * gp[1]:].reshape(gs[0], gs[1], M_CONV_DIM)[:, gs[1] - keep:]
    return y, (sp, xp), (ss, xs)


R_TOK = 64
R_PAIRS = 4


def _rwkv_chunk(nb, nv, toks, sbs, prm):
    n = R_TOK // nb
    ntok = R_TOK
    r, kr, v, wl, al, g = (_pad_rows(t_, ntok) for t_ in toks)
    w0, a0, k_k, k_a, r_k, ln_g, ln_b = prm
    m0 = _iota((1, LANES), 1) < R_HEADSIZE
    ones_blk = ((_iota((LANES, LANES), 0) < R_HEADSIZE) == (_iota((LANES, LANES), 1) < R_HEADSIZE)).astype(BF16)

    w = -_softplus(-(w0 + wl)) - 0.5
    lw = -jnp.exp(w)
    a = _sigmoid(a0 + al)
    kk = kr * k_k
    k = kr * (1.0 + (a - 1.0) * k_a)
    if nv < n:
        valid = _iota((ntok, 1), 0) < nv
        lw = jnp.where(valid, lw, 0.0)
        kk = jnp.where(valid, kk, 0.0)
        k = jnp.where(valid, k, 0.0)
        v = jnp.where(valid, v, 0.0)
    same, causal = _seq_masks(ntok, n)
    head_sums = _xdot(jnp.concatenate([kk * kk, r * k * r_k], axis=0), ones_blk, terms=2)
    lam = _xdot_left(causal.astype(BF16), lw)
    lam_tot = lam[ntok - 1:ntok] if nb == 1 else _xdot_left(same.astype(BF16), lw)
    yield
    bonus = head_sums[ntok:] * v
    kk = kk / jnp.maximum(jnp.sqrt(head_sums[:ntok]), 1e-12)
    beta = kk * a
    e_in = jnp.exp(lam)
    e_out = jnp.exp(-lam)
    e_end = jnp.exp(lam_tot - lam)

    def stack(x):
        x3 = x.reshape(nb, n, LANES)
        s = jnp.concatenate([jnp.where(m0, x3, 0.0), jnp.where(m0, 0.0, x3)], axis=1)
        return s.reshape(2 * ntok, LANES).astype(BF16)

    am, rm = stack(kk * jnp.exp(lam - lw)), stack(r * e_in)
    khm, bhm = stack(k * e_out), stack(beta * e_out)
    vst = stack(v)
    n2 = 2 * ntok
    rr, cc = _iota((n2, n2), 0), _iota((n2, n2), 1)
    same_blk = (rr >> _log2(n)) == (cc >> _log2(n))
    strict = same_blk & ((rr & (n - 1)) > (cc & (n - 1)))
    incl = same_blk & ((rr & (n - 1)) >= (cc & (n - 1)))
    qm = jnp.concatenate([am, rm], axis=0)
    pmat = _dot_nt(qm, jnp.concatenate([khm, bhm], axis=0))
    w2 = 2 * n
    if nb == 1:
        qs = [_dot_nt(qm, sbs[0])]
    else:
        qs = [_dot_nt(jnp.concatenate([am[b * w2:(b + 1) * w2], rm[b * w2:(b + 1) * w2]], axis=0), sbs[b])
              for b in range(nb)]
    ktm, btm = stack(k * e_end), stack(beta * e_end)
    yield
    a_ak = jnp.where(strict, pmat[:n2, :n2], 0.0)
    nmat = jnp.where(strict, -pmat[:n2, n2:], 0.0)
    a_rk = jnp.where(incl, pmat[n2:, :n2], 0.0)
    a_rb = jnp.where(incl, pmat[n2:, n2:], 0.0)
    rhs_s = qs[0][:w2] if nb == 1 else jnp.concatenate([q[:w2] for q in qs], axis=0)
    ys = qs[0][w2:] if nb == 1 else jnp.concatenate([q[w2:] for q in qs], axis=0)

    tm = (rr == cc).astype(F32) + nmat
    npow = _dot(nmat, nmat)
    av = _dot(jnp.concatenate([a_ak, a_rk], axis=0), vst)
    yield
    rhs = rhs_s + av[:n2]
    ys = ys + av[n2:]
    for i in range(_log2(n) - 1):
        if i + 2 < _log2(n):
            both = _dot(jnp.concatenate([tm, npow], axis=0), npow)
            prod, npow = both[:n2], both[n2:]
        else:
            prod = _dot(tm, npow)
        yield
        tm = tm + prod
    u = _dot(tm, rhs)
    yield
    yst = (ys - _dot(a_rb, u)).reshape(nb, w2, LANES)
    nu = (-u).astype(BF16)
    new_sbs = []
    for b in range(nb):
        rows = slice(b * w2, (b + 1) * w2)
        lt = lam_tot if nb == 1 else lam_tot[b * n:b * n + 1]
        new_sbs.append(sbs[b] * jnp.exp(lt)
                       + _dot_tn(jnp.concatenate([vst[rows], nu[rows]], axis=0),
                                 jnp.concatenate([ktm[rows], btm[rows]], axis=0)))
    yield
    y = (yst[:, :n] + yst[:, n:]).reshape(ntok, LANES)
    inv = 1.0 / R_HEADSIZE
    mean = _xdot(y, ones_blk, terms=2) * inv
    yield
    yc = y - mean
    var = _xdot(yc * yc, ones_blk, terms=2) * inv
    yield
    yn = yc * lax.rsqrt(var + R_LN_EPS) * ln_g + ln_b
    return ((yn + bonus) * g)[:nb * nv], new_sbs


def _rwkv_body(*refs, t, nb, has_state):
    if has_state:
        (r_ref, k_ref, v_ref, wl_ref, al_ref, g_ref, prm_ref, sin_ref, y_ref, sout_ref) = refs
    else:
        (r_ref, k_ref, v_ref, wl_ref, al_ref, g_ref, prm_ref, y_ref, sout_ref) = refs
    tok_refs = (r_ref, k_ref, v_ref, wl_ref, al_ref, g_ref)
    zeros = jnp.zeros((R_HEADSIZE, R_HEADSIZE), F32)

    def lanes(p):
        return slice(p * LANES, (p + 1) * LANES)

    def run(rows, nv, states):
        res = _lockstep([
            _rwkv_chunk(nb, nv, tuple(ref[rows, lanes(p)] for ref in tok_refs), states[p],
                        tuple(prm_ref[i:i + 1, lanes(p)] for i in range(7)))
            for p in range(R_PAIRS)])
        for p, (out, _) in enumerate(res):
            y_ref[rows, lanes(p)] = out.astype(y_ref.dtype)
        return tuple(tuple(sbs) for _, sbs in res)

    if has_state:
        states = tuple(
            tuple(jnp.concatenate([jnp.concatenate([sin_ref[b, 2 * p], zeros], axis=1),
                                   jnp.concatenate([zeros, sin_ref[b, 2 * p + 1]], axis=1)], axis=0)
                  for b in range(nb))
            for p in range(R_PAIRS))
    else:
        states = tuple(tuple(jnp.zeros((LANES, LANES), F32) for _ in range(nb)) for _ in range(R_PAIRS))

    if nb == 1:
        n_full = t // R_TOK
        states = lax.fori_loop(
            0, n_full, lambda c, st: run(pl.ds(pl.multiple_of(c * R_TOK, R_TOK), R_TOK), R_TOK, st), states)
        if t % R_TOK:
            states = run(pl.ds(n_full * R_TOK, t % R_TOK), t % R_TOK, states)
    else:
        assert nb * t == R_TOK
        states = run(slice(None), t, states)
    for p in range(R_PAIRS):
        for b in range(nb):
            sout_ref[b, 2 * p] = states[p][b][:R_HEADSIZE, :R_HEADSIZE]
            sout_ref[b, 2 * p + 1] = states[p][b][R_HEADSIZE:, R_HEADSIZE:]


def _rwkv(r, k, v, wl, al, g, prm, group, wkv_state=None, y_into=None):
    nseq, t, row0, nb = group
    rb = nb * t
    r0 = row0 // rb
    assert row0 % rb == 0 and nseq % nb == 0
    has_state = wkv_state is not None
    gl = R_PAIRS * LANES
    tok = pl.BlockSpec((rb, gl), lambda s, hp_: (s + r0, hp_), **_single(not has_state))
    in_specs = [tok] * 6 + [pl.BlockSpec((SUBLANES, gl), lambda s, hp_: (0, hp_))]
    args = [r, k, v, wl, al, g, prm]
    if has_state:
        in_specs.append(pl.BlockSpec((None, nb, 2 * R_PAIRS, R_HEADSIZE, R_HEADSIZE),
                                     lambda s, hp_: (0, s, hp_, 0, 0)))
        args.append(wkv_state)
    body, aliases = _write_into(functools.partial(_rwkv_body, t=t, nb=nb, has_state=has_state),
                                in_specs, args, (y_into,))
    return pl.pallas_call(
        body,
        grid=(nseq // nb, R_HEADS // (2 * R_PAIRS)),
        in_specs=in_specs,
        out_specs=[pl.BlockSpec((rb, gl), lambda s, hp_: (s + r0, hp_)),
                   pl.BlockSpec((nb, 2 * R_PAIRS, R_HEADSIZE, R_HEADSIZE), lambda s, hp_: (s, hp_, 0, 0))],
        out_shape=[jax.ShapeDtypeStruct((r.shape[0], D_MODEL), BF16),
                   jax.ShapeDtypeStruct((nseq, R_HEADS, R_HEADSIZE, R_HEADSIZE), F32)],
        input_output_aliases=aliases,
        compiler_params=_cparams(2),
        name="rwkv_sample" if has_state else "rwkv_prompt",
    )(*args)


MIX_COLS = 512


def _mix_body(*refs, t, nb, has_state):
    if has_state:
        u_ref, mu_ref, sh_ref = refs[:3]
    else:
        u_ref, mu_ref = refs[:2]
    o_refs = refs[-6:]
    u = u_ref[...]
    rows = nb * t
    tpos = _iota((rows, 1), 0) & (t - 1) if nb > 1 else _iota((rows, 1), 0)
    prev = pltpu.roll(u, 1, 0)
    if has_state:
        pick = (_iota((rows, BF16_ROWS), 0) >> _log2(t)) == _iota((rows, BF16_ROWS), 1)
        first = _xdot_left(pick.astype(BF16), _pad_rows(sh_ref[...], BF16_ROWS))
    else:
        first = 0.0
    dx = jnp.where(tpos == 0, first, prev) - u
    for i, o_ref in enumerate(o_refs):
        o_ref[...] = (u + dx * mu_ref[i:i + 1, :]).astype(o_ref.dtype)


def _rwkv_mix(u, r_mu, group, shift_state=None, into=None):
    nseq, t, row0, nb = group
    m, d = u.shape
    rb = nb * t
    r0 = row0 // rb
    has_state = shift_state is not None
    assert row0 % rb == 0 and nseq % nb == 0 and (nb == 1 or nb <= BF16_ROWS)
    blk = pl.BlockSpec((rb, MIX_COLS), lambda s, c: (s + r0, c))
    in_specs = [blk, pl.BlockSpec((None, 6, MIX_COLS), lambda s, c: (0, 0, c))]
    args = [u, r_mu]
    if has_state:
        in_specs.append(pl.BlockSpec((None, nb, MIX_COLS), lambda s, c: (0, s, c)))
        args.append(shift_state)
    body, aliases = _write_into(functools.partial(_mix_body, t=t, nb=nb, has_state=has_state),
                                in_specs, args, into or (None,) * 6)
    return pl.pallas_call(
        body,
        grid=(nseq // nb, d // MIX_COLS),
        in_specs=in_specs,
        out_specs=[blk] * 6,
        out_shape=[jax.ShapeDtypeStruct((m, d), BF16)] * 6,
        input_output_aliases=aliases,
        compiler_params=_cparams(2),
        name="rwkv_mix_sample" if has_state else "rwkv_mix_prompt",
    )(*args)


def _rwkv_layer(u, groups, state_wkv, state_shift, r_mu, r_w_r, r_w_k, r_w_v, r_w0, r_w1, r_w2, r_a0, r_a1,
                r_a2, r_g1, r_g2, r_k_k, r_k_a, r_r_k, r_ln_g, r_ln_b, r_w_o):
    gp, gs = groups
    mixed = _rwkv_mix(u, r_mu, gp)
    xr, xw, xk, xv, xa, xg = _rwkv_mix(u, r_mu, gs, state_shift, into=mixed)
    r = _mm(xr, r_w_r, (0,), name="rwkv_r")
    k = _mm(xk, r_w_k, (0,), name="rwkv_k")
    v = _mm(xv, r_w_v, (0,), name="rwkv_v")
    wl = _mm(_mm(xw, r_w1, (0,), out_dtype=BF16, act="tanh", name="rwkv_w1"), r_w2, (0,), name="rwkv_w2")
    al = _mm(_mm(xa, r_a1, (0,), out_dtype=BF16, name="rwkv_a1"), r_a2, (0,), name="rwkv_a2")
    g = _mm(_mm(xg, r_g1, (0,), out_dtype=BF16, act="sigmoid", name="rwkv_g1"), r_g2, (0,), name="rwkv_g2")
    prm = jnp.concatenate([r_w0, r_a0, r_k_k, r_k_a, r_r_k.reshape(1, D_MODEL), r_ln_g, r_ln_b,
                           jnp.zeros((1, D_MODEL), F32)], axis=0)
    y, sp = _rwkv(r, k, v, wl, al, g, prm, gp)
    y, ss = _rwkv(r, k, v, wl, al, g, prm, gs, state_wkv, y_into=y)
    y = _mm(y, r_w_o, (0,), name="rwkv_out")
    last_p = jnp.stack([u[(b + 1) * gp[1] - 1] for b in range(gp[0])])
    last_s = u[gp[0] * gp[1]:].reshape(gs[0], gs[1], D_MODEL)[:, -1]
    return y, (sp, last_p), (ss, last_s)


H_CHUNK = 128
H_SUB = 16
H_HPS = 2


def _gla_diag(q3, g3, k3, v3):
    sub = q3.shape[1]
    tpos = _iota((1, sub, 1), 1)
    y3 = jnp.zeros(v3.shape, F32)
    for s in range(sub):
        arg = jnp.where(tpos >= s, g3 - g3[:, s:s + 1, :], -jnp.inf)
        a_s = jnp.sum(q3 * k3[:, s:s + 1, :] * jnp.exp(arg), axis=-1, keepdims=True)
        yield
        y3 = y3 + a_s * v3[:, s:s + 1, :]
    return y3


def _gla_prompt_chunk(nv, q, logf, k, v, st):
    n = -(-nv // H_SUB) * H_SUB
    q, logf, k, v = (_pad_rows(t_, n) for t_ in (q, logf, k, v))
    causal = _iota((n, n), 0) >= _iota((n, n), 1)
    gcum = _xdot_left(causal.astype(BF16), logf)
    yield
    y = _dot_nt(q * jnp.exp(gcum), st)
    g_n = gcum[n - 1:n]
    upd = _dot_tn(v, k * jnp.exp(g_n - gcum))
    atts = []
    for i in range(1, n // H_SUB):
        lo = i * H_SUB
        gs = gcum[lo - 1:lo]
        atts.append(_dot_nt(q[lo:lo + H_SUB] * jnp.exp(gcum[lo:lo + H_SUB] - gs),
                            k[:lo] * jnp.exp(gs - gcum[:lo])))
    yield
    st = st * jnp.exp(g_n) + upd
    offs = [jnp.zeros((H_SUB, LANES), F32)] + [_dot(att, v[:(i + 1) * H_SUB]) for i, att in enumerate(atts)]
    shape3 = (n // H_SUB, H_SUB, LANES)
    y3 = yield from _gla_diag(q.reshape(shape3), gcum.reshape(shape3), k.reshape(shape3), v.reshape(shape3))
    off = offs[0] if len(offs) == 1 else jnp.concatenate(offs, axis=0)
    return (y + off + y3.reshape(n, LANES))[:nv], st


def _gla_sample_chunk(nb, q, logf, k, v, sts):
    n = SUBLANES
    ntok = nb * n
    _, causal = _seq_masks(ntok, n)
    gcum = _xdot_left(causal.astype(BF16), logf)
    yield
    shape3 = (nb, n, LANES)
    g3 = gcum.reshape(shape3)
    g_tot = g3[:, n - 1:n, :]
    qe = q * jnp.exp(gcum)
    kd = (k.reshape(shape3) * jnp.exp(g_tot - g3)).reshape(ntok, LANES)
    ys = [_dot_nt(_pad_rows(qe[b * n:(b + 1) * n], BF16_ROWS), sts[b])[:n] for b in range(nb)]
    upds = [_dot_tn(_pad_rows(v[b * n:(b + 1) * n], BF16_ROWS), _pad_rows(kd[b * n:(b + 1) * n], BF16_ROWS))
            for b in range(nb)]
    y3 = yield from _gla_diag(q.reshape(shape3), g3, k.reshape(shape3), v.reshape(shape3))
    new = [sts[b] * jnp.exp(g_tot[b]) + upds[b] for b in range(nb)]
    return y3.reshape(ntok, LANES) + jnp.concatenate(ys, axis=0), new


def _hgrn_body(*refs, t, nb, has_state, layer):
    if has_state:
        (q_ref, f_ref, i_ref, g_ref, lb_ref, ng_ref, sin_ref, y_ref, sout_ref) = refs
    else:
        (q_ref, f_ref, i_ref, g_ref, lb_ref, ng_ref, y_ref, sout_ref) = refs
    hl = lb_ref[...]
    e = jnp.exp(hl - jnp.max(hl, axis=0, keepdims=True))
    sm = e / jnp.sum(e, axis=0, keepdims=True)
    lb_all = (jnp.sum(sm[1:layer + 1], axis=0, keepdims=True) if layer > 0
              else jnp.zeros((1, H_HPS * LANES), F32))
    ng = ng_ref[...]

    def lanes(hi):
        return slice(hi * LANES, (hi + 1) * LANES)

    def gates(rows, hi):
        lb = lb_all[:, lanes(hi)]
        q = _silu(q_ref[rows, lanes(hi)])
        f = lb + (1.0 - lb) * _sigmoid(f_ref[rows, lanes(hi)])
        return q, jnp.log(f), 1.0 - f, i_ref[rows, lanes(hi)]

    def finish(rows, hi, yy):
        yy = yy * lax.rsqrt(jnp.mean(yy * yy, axis=-1, keepdims=True) + EPS) * ng
        y_ref[rows, lanes(hi)] = (yy * _silu(g_ref[rows, lanes(hi)])).astype(y_ref.dtype)

    if has_state:
        rows = slice(None)
        res = _lockstep([_gla_sample_chunk(nb, *gates(rows, hi), [sin_ref[b, hi].T for b in range(nb)])
                         for hi in range(H_HPS)])
        for hi, (yy, sts) in enumerate(res):
            finish(rows, hi, yy)
            for b in range(nb):
                sout_ref[b, hi] = sts[b].T
        return

    def run(start, nv, sts):
        rows = pl.ds(pl.multiple_of(start, SUBLANES), nv)
        res = _lockstep([_gla_prompt_chunk(nv, *gates(rows, hi), sts[hi]) for hi in range(H_HPS)])
        for hi, (yy, _) in enumerate(res):
            finish(rows, hi, yy)
        return tuple(st for _, st in res)

    sts = tuple(jnp.zeros((H_VDIM, H_EXPAND), F32) for _ in range(H_HPS))
    n_full = t // H_CHUNK
    sts = lax.fori_loop(0, n_full, lambda c, s: run(c * H_CHUNK, H_CHUNK, s), sts)
    if t % H_CHUNK:
        sts = run(n_full * H_CHUNK, t % H_CHUNK, sts)
    for hi in range(H_HPS):
        sout_ref[0, hi] = sts[hi].T


def _hgrn(layer, proj, h_lb, h_norm_g, group, state=None, y_into=None):
    nseq, t, row0, nb = group
    rb = nb * t
    r0 = row0 // rb
    assert row0 % rb == 0 and nseq % nb == 0
    has_state = state is not None
    nhb = H_HEADS // H_HPS
    hw = H_HPS * LANES

    def tok(section):
        return pl.BlockSpec((rb, hw), lambda s, h: (s + r0, h + section * nhb))

    in_specs = [tok(0), tok(1), tok(2), tok(3),
                pl.BlockSpec((DEPTH, hw), lambda s, h: (0, h)),
                pl.BlockSpec((1, LANES), lambda s, h: (0, 0))]
    args = [proj, proj, proj, proj, h_lb, h_norm_g]
    if has_state:
        in_specs.append(pl.BlockSpec((None, nb, H_HPS, H_EXPAND, H_VDIM), lambda s, h: (0, s, h, 0, 0)))
        args.append(state)
    body, aliases = _write_into(functools.partial(_hgrn_body, t=t, nb=nb, has_state=has_state, layer=layer),
                                in_specs, args, (y_into,))
    return pl.pallas_call(
        body,
        grid=(nseq // nb, nhb),
        in_specs=in_specs,
        out_specs=[pl.BlockSpec((rb, hw), lambda s, h: (s + r0, h)),
                   pl.BlockSpec((nb, H_HPS, H_EXPAND, H_VDIM), lambda s, h: (s, h, 0, 0))],
        out_shape=[jax.ShapeDtypeStruct((proj.shape[0], D_MODEL), BF16),
                   jax.ShapeDtypeStruct((nseq, H_HEADS, H_EXPAND, H_VDIM), F32)],
        input_output_aliases=aliases,
        compiler_params=_cparams(2),
        name="hgrn_sample" if has_state else "hgrn_prompt",
    )(*args)


def _hgrn_layer(layer, u, groups, state_hgrn, h_w_in, h_lb, h_norm_g, h_w_out):
    proj = _mm(u, h_w_in, (0,), name="hgrn_in")
    y, sp = _hgrn(layer, proj, h_lb, h_norm_g, groups[0])
    y, ss = _hgrn(layer, proj, h_lb, h_norm_g, groups[1], state_hgrn, y_into=y)
    y = _mm(y, h_w_out, (0,), name="hgrn_out")
    return y, sp, ss


def kernel(x_prompt, x_sample, state_ssm, state_conv, state_wkv, state_shift, state_hgrn, meta_tokens, norm_g, ffn_w_gate, ffn_w_up, ffn_w_down, m_w_in, m_conv_w, m_conv_b, m_dt_bias, m_a_log, m_d, m_norm_g, m_w_out, r_mu, r_w_r, r_w_k, r_w_v, r_w0, r_w1, r_w2, r_a0, r_a1, r_a2, r_g1, r_g2, r_k_k, r_k_a, r_r_k, r_ln_g, r_ln_b, r_w_o, h_w_in, h_lb, h_norm_g, h_w_out):
    pieces = [p for b in range(BATCH) for p in (meta_tokens, x_prompt[b])] + [x_sample.reshape(M_S, D_MODEL)]
    h = jnp.concatenate(pieces, axis=0)
    conv_pad = jnp.pad(state_conv, ((0, 0), (0, 0), (SUBLANES - (M_CONV - 1), 0), (0, 0)))

    p_conv, s_conv = [], []
    ssm = (None, None)
    u = _norm(h, norm_g, 0, 0)
    for l in range(DEPTH):
        kind, j = l % 3, l // 3
        y = _mm(_swiglu_up(u, ffn_w_gate, ffn_w_up, (l, 0)), ffn_w_down, (l, 0), tm=464, name="ffn_down")
        h, u = _resnorm(h, y, norm_g, 0.5, l, 1, l, 2, u_dtype=F32 if kind == 1 else BF16)
        if kind == 0:
            y, (sp, cp), (ss, cs) = _mamba_layer(j, u, GROUPS, m_w_in, m_conv_w, m_conv_b, m_dt_bias, m_a_log,
                                                 m_d, m_norm_g, m_w_out, conv_pad, state_ssm, ssm)
            ssm = (sp, ss)
            p_conv.append(cp)
            s_conv.append(cs)
        elif kind == 1:
            y, (p_wkv, p_shift), (s_wkv, s_shift) = _rwkv_layer(
                u, GROUPS, state_wkv, state_shift, r_mu, r_w_r, r_w_k, r_w_v, r_w0, r_w1, r_w2, r_a0, r_a1,
                r_a2, r_g1, r_g2, r_k_k, r_k_a, r_r_k, r_ln_g, r_ln_b, r_w_o)
        else:
            y, p_hgrn, s_hgrn = _hgrn_layer(l, u, GROUPS, state_hgrn, h_w_in, h_lb, h_norm_g, h_w_out)
        h, u = _resnorm(h, y, norm_g, 1.0, l, 3, l, 4)
        y = _mm(_swiglu_up(u, ffn_w_gate, ffn_w_up, (l, 1)), ffn_w_down, (l, 1), tm=464, name="ffn_down")
        if l + 1 < DEPTH:
            h, u = _resnorm(h, y, norm_g, 0.5, l, 5, l + 1, 0)
        else:
            h, _ = _resnorm(h, y, norm_g, 0.5, l, 5)

    y_prompt = jnp.stack([h[b * T_P + N_META:(b + 1) * T_P] for b in range(BATCH)])
    y_sample = h[M_P:].reshape(DEC_BATCH, T_S, D_MODEL)
    return (y_prompt, y_sample,
            ssm[0], jnp.stack(p_conv), p_wkv[None], p_shift[None], p_hgrn[None],
            ssm[1], jnp.stack(s_conv), s_wkv[None], s_shift[None], s_hgrn[None])
```

```python
import functools

import jax
import jax.numpy as jnp
from jax import lax
from jax.experimental import pallas as pl
from jax.experimental.pallas import tpu as pltpu

F32 = jnp.float32
BF16 = jnp.bfloat16

D_MODEL = 2048
BATCH = 4
SEQ = 2048
DEPTH = 4
DEC_BATCH = 128
DEC_SEQ = 8
N_META = 16
EPS = 1e-6
D_FF = 5632

M_D_INNER = 2 * D_MODEL
M_HEADDIM = 64
M_HEADS = M_D_INNER // M_HEADDIM
M_GROUPS = 8
M_D_STATE = 128
M_CONV = 4
M_CONV_DIM = M_D_INNER + 2 * M_GROUPS * M_D_STATE
M_NORM_EPS = 1e-5
M_GW = M_D_INNER // M_GROUPS
M_R = M_HEADS // M_GROUPS

R_HEADSIZE = 64
R_HEADS = D_MODEL // R_HEADSIZE
R_LN_EPS = 64e-5

H_HEADS = 16
H_EXPAND = 128
H_FDIM = H_HEADS * H_EXPAND
H_VDIM = D_MODEL // H_HEADS

T_P = N_META + SEQ
T_S = DEC_SEQ
M_P = BATCH * T_P
M_S = DEC_BATCH * T_S
M_TOK = M_P + M_S
S_BS = 8
LANES = 128
SUBLANES = 8
BF16_ROWS = 16
VMEM_LIMIT = 56 * 1024 * 1024
LOG2_E = 1.4426950408889634
GROUPS = ((BATCH, T_P, 0, 1), (DEC_BATCH, T_S, M_P, S_BS))


def _cparams(n_axes):
    return pltpu.CompilerParams(dimension_semantics=("arbitrary",) * n_axes,
                                vmem_limit_bytes=VMEM_LIMIT)


def _silu(x):
    h = 0.5 * x
    return h + h * jnp.tanh(h)


def _sigmoid(x):
    return 1.0 / (1.0 + jnp.exp(-x))


def _softplus(x):
    return jnp.maximum(x, 0.0) + jnp.log1p(jnp.exp(-jnp.abs(x)))


def _iota(shape, dim):
    return lax.broadcasted_iota(jnp.int32, shape, dim)


def _log2(n):
    assert n & (n - 1) == 0
    return n.bit_length() - 1


def _dot(a, b):
    return jnp.dot(a.astype(BF16), b.astype(BF16), preferred_element_type=F32)


def _dot_nt(a, b):
    return lax.dot_general(a.astype(BF16), b.astype(BF16), (((1,), (1,)), ((), ())),
                           preferred_element_type=F32)


def _dot_tn(a, b):
    return lax.dot_general(a.astype(BF16), b.astype(BF16), (((0,), (0,)), ((), ())),
                           preferred_element_type=F32)


def _split(x, terms):
    parts = []
    rem = x
    for _ in range(terms):
        hi = rem.astype(BF16)
        parts.append(hi)
        rem = rem - hi.astype(F32)
    return parts


def _xdot(x, sel, terms=3):
    return sum(jnp.dot(p, sel, preferred_element_type=F32) for p in _split(x, terms))


def _xdot_left(sel, x, terms=3):
    return sum(jnp.dot(sel, p, preferred_element_type=F32) for p in _split(x, terms))


def _xdot_nt(sel, x, terms=3):
    return sum(lax.dot_general(sel, p, (((1,), (1,)), ((), ())), preferred_element_type=F32)
               for p in _split(x, terms))


def _pad_rows(x, n):
    if x.shape[0] >= n:
        return x
    return jnp.concatenate([x, jnp.zeros((n - x.shape[0],) + x.shape[1:], x.dtype)], axis=0)


def _seq_masks(ntok, n):
    r, c = _iota((ntok, ntok), 0), _iota((ntok, ntok), 1)
    same = (r >> _log2(n)) == (c >> _log2(n))
    return same, same & (r >= c)


def _lockstep(gens):
    results = [None] * len(gens)
    live = list(enumerate(gens))
    while live:
        still = []
        for i, gen in live:
            try:
                next(gen)
                still.append((i, gen))
            except StopIteration as done:
                results[i] = done.value
        live = still
    return results


def _write_into(body, in_specs, args, targets):
    n_in = len(args)
    aliases = {}
    for k, arr in enumerate(targets):
        if arr is not None:
            aliases[len(args)] = k
            in_specs.append(pl.BlockSpec(memory_space=pl.ANY))
            args.append(arr)
    n_extra = len(args) - n_in

    def wrapped(*refs):
        return body(*refs[:n_in], *refs[n_in + n_extra:])

    return (wrapped if n_extra else body), aliases


def _row_tile(m, want):
    return next(d for d in range(want - want % BF16_ROWS, 0, -BF16_ROWS) if m % d == 0)


def _single(mode):
    return dict(pipeline_mode=pl.Buffered(1)) if mode else {}


def _mm_body(x_ref, w_ref, o_ref, wbf_ref, *, act, valid_cols, w_t):
    @pl.when(pl.program_id(1) == 0)
    def _():
        w = w_ref[...].T if w_t else w_ref[...]
        if valid_cols is not None:
            w = jnp.where(_iota(w.shape, 1) < valid_cols, w, 0.0)
        wbf_ref[...] = w.astype(BF16)

    acc = jnp.dot(x_ref[...].astype(BF16), wbf_ref[...], preferred_element_type=F32)
    if act == "tanh":
        acc = jnp.tanh(acc)
    elif act == "sigmoid":
        acc = _sigmoid(acc)
    o_ref[...] = acc.astype(o_ref.dtype)


def _mm(x, w, lead=(), col0=0, ncols=None, tn=512, tm=1856, out_dtype=F32, act=None, w_t=False, name="mm"):
    m, k = x.shape
    n_all = w.shape[-2] if w_t else w.shape[-1]
    ncols = n_all if ncols is None else ncols
    tn = min(tn, ncols)
    tm = _row_tile(m, tm)
    assert ncols % tn == 0 and col0 % tn == 0 and m % tm == 0 and w.shape[-1 if w_t else -2] == k
    valid_cols = None
    if col0 + ncols > n_all:
        assert ncols == tn
        valid_cols = n_all - col0
    jb = col0 // tn
    if w_t:
        wspec = pl.BlockSpec((None,) * len(lead) + (tn, k), lambda j, i: lead + (j + jb, 0))
    else:
        wspec = pl.BlockSpec((None,) * len(lead) + (k, tn), lambda j, i: lead + (0, j + jb))
    return pl.pallas_call(
        functools.partial(_mm_body, act=act, valid_cols=valid_cols, w_t=w_t),
        grid=(ncols // tn, m // tm),
        in_specs=[pl.BlockSpec((tm, k), lambda j, i: (i, 0)), wspec],
        out_specs=pl.BlockSpec((tm, tn), lambda j, i: (i, j)),
        out_shape=jax.ShapeDtypeStruct((m, ncols), out_dtype),
        scratch_shapes=[pltpu.VMEM((k, tn), BF16)],
        compiler_params=_cparams(2),
        name=name,
    )(x, w)


def _swiglu_up_body(x_ref, wg_ref, wu_ref, wd_ref, o_ref, wd_bf_ref, wg_bf, wu_bf):
    @pl.when(pl.program_id(1) == 0)
    def _():
        wg_bf[...] = wg_ref[...].astype(BF16)
        wu_bf[...] = wu_ref[...].astype(BF16)

    x = x_ref[...]
    g = jnp.dot(x, wg_bf[...], preferred_element_type=F32)
    u = jnp.dot(x, wu_bf[...], preferred_element_type=F32)
    o_ref[...] = (g * _sigmoid(g) * u).astype(o_ref.dtype)
    wd_bf_ref[...] = wd_ref[...].astype(BF16)


def _swiglu_up(x, w_gate, w_up, w_down, lead, tn=512, tm=1856):
    m, k = x.shape
    n = w_gate.shape[-1]
    d = w_down.shape[-1]
    tm = _row_tile(m, tm)
    nl = (None,) * len(lead)
    wspec = pl.BlockSpec(nl + (k, tn), lambda j, i: lead + (0, j))
    nsub = min(4, m // tm)
    wd_rows = tn // nsub
    assert tn % nsub == 0 and wd_rows % BF16_ROWS == 0

    def wd_block(j, i):
        return j * nsub + jnp.minimum(i, nsub - 1)

    return pl.pallas_call(
        _swiglu_up_body,
        grid=(n // tn, m // tm),
        in_specs=[pl.BlockSpec((tm, k), lambda j, i: (i, 0)), wspec, wspec,
                  pl.BlockSpec(nl + (wd_rows, d), lambda j, i: lead + (wd_block(j, i), 0))],
        out_specs=[pl.BlockSpec((tm, tn), lambda j, i: (i, j)),
                   pl.BlockSpec((wd_rows, d), lambda j, i: (wd_block(j, i), 0))],
        out_shape=[jax.ShapeDtypeStruct((m, n), BF16), jax.ShapeDtypeStruct((n, d), BF16)],
        scratch_shapes=[pltpu.VMEM((k, tn), BF16), pltpu.VMEM((k, tn), BF16)],
        compiler_params=_cparams(2),
        name="swiglu_up",
    )(x, w_gate, w_up, w_down)


def _rms(x, g):
    return x * lax.rsqrt(jnp.mean(x * x, axis=-1, keepdims=True) + EPS) * g


def _norm_body(h_ref, g_ref, u_ref, *, gi):
    u_ref[...] = _rms(h_ref[...], g_ref[gi:gi + 1, :]).astype(u_ref.dtype)


def _norm(h, norm_g, layer, gi, out_dtype=BF16, tm=464):
    m, d = h.shape
    tm = _row_tile(m, tm)
    return pl.pallas_call(
        functools.partial(_norm_body, gi=gi),
        grid=(m // tm,),
        in_specs=[pl.BlockSpec((tm, d), lambda i: (i, 0)),
                  pl.BlockSpec((None, 6, d), lambda i: (layer, 0, 0))],
        out_specs=pl.BlockSpec((tm, d), lambda i: (i, 0)),
        out_shape=jax.ShapeDtypeStruct((m, d), out_dtype),
        compiler_params=_cparams(1),
        name="norm",
    )(h, norm_g)


def _resnorm_body(h_ref, y_ref, ga_ref, gb_ref, hn_ref, u_ref, *, scale, ia, ib):
    hn = h_ref[...] + scale * _rms(y_ref[...], ga_ref[ia:ia + 1, :])
    hn_ref[...] = hn
    if u_ref is not None:
        u_ref[...] = _rms(hn, gb_ref[ib:ib + 1, :]).astype(u_ref.dtype)


def _down_resnorm_body(x_ref, w_ref, h_ref, ga_ref, gb_ref, hn_ref, *rest, scale, ia, ib):
    u_ref, y_scr = rest if len(rest) == 2 else (None, rest[0])
    j = pl.program_id(1)
    y_scr[j] = jnp.dot(x_ref[...], w_ref[...], preferred_element_type=F32)

    @pl.when(j == pl.num_programs(1) - 1)
    def _():
        y = jnp.concatenate([y_scr[c] for c in range(y_scr.shape[0])], axis=1)
        hn = h_ref[...] + scale * _rms(y, ga_ref[ia:ia + 1, :])
        hn_ref[...] = hn
        if u_ref is not None:
            u_ref[...] = _rms(hn, gb_ref[ib:ib + 1, :]).astype(u_ref.dtype)


def _down_resnorm(x, w_bf, h, norm_g, scale, la, ia, lb=None, ib=None, u_dtype=BF16, tn=512, tm=464):
    m, k = x.shape
    d = w_bf.shape[-1]
    tm = _row_tile(m, tm)
    row = pl.BlockSpec((tm, d), lambda i, j: (i, 0))
    with_u = lb is not None
    if not with_u:
        lb = la
    out_shape = [jax.ShapeDtypeStruct((m, d), F32)]
    out_specs = [row]
    if with_u:
        out_shape.append(jax.ShapeDtypeStruct((m, d), u_dtype))
        out_specs.append(row)
    res = pl.pallas_call(
        functools.partial(_down_resnorm_body, scale=scale, ia=ia, ib=ib),
        grid=(m // tm, d // tn),
        in_specs=[pl.BlockSpec((tm, k), lambda i, j: (i, 0)),
                  pl.BlockSpec((k, tn), lambda i, j: (0, j)),
                  row,
                  pl.BlockSpec((None, 6, d), lambda i, j: (la, 0, 0)),
                  pl.BlockSpec((None, 6, d), lambda i, j: (lb, 0, 0))],
        out_specs=out_specs,
        out_shape=out_shape,
        scratch_shapes=[pltpu.VMEM((d // tn, tm, tn), F32)],
        compiler_params=_cparams(2),
        name="down_resnorm",
    )(x, w_bf, h, norm_g, norm_g)
    return res if with_u else (res[0], None)


def _resnorm(h, y, norm_g, scale, la, ia, lb=None, ib=None, u_dtype=BF16, tm=464):
    m, d = h.shape
    tm = _row_tile(m, tm)
    row = pl.BlockSpec((tm, d), lambda i: (i, 0))
    with_u = lb is not None
    body = functools.partial(_resnorm_body, scale=scale, ia=ia, ib=ib)
    if not with_u:
        body = functools.partial(_resnorm_body, u_ref=None, scale=scale, ia=ia, ib=ib)
        lb = la
    out_shape = [jax.ShapeDtypeStruct((m, d), F32)]
    out_specs = [row]
    if with_u:
        out_shape.append(jax.ShapeDtypeStruct((m, d), u_dtype))
        out_specs.append(row)
    res = pl.pallas_call(
        body,
        grid=(m // tm,),
        in_specs=[row, row,
                  pl.BlockSpec((None, 6, d), lambda i: (la, 0, 0)),
                  pl.BlockSpec((None, 6, d), lambda i: (lb, 0, 0))],
        out_specs=out_specs,
        out_shape=out_shape,
        compiler_params=_cparams(1),
        name="resnorm",
    )(h, y, norm_g, norm_g)
    return res if with_u else (res[0], None)


SSD_CHUNK = 128
SSD_GPS = 2


def _conv_taps(cur, shifted, w, b):
    out = b + cur * w[M_CONV - 1:M_CONV]
    for k in range(1, M_CONV):
        out = out + shifted(k) * w[M_CONV - 1 - k:M_CONV - k]
    return _silu(out)


def _conv_silu_seq(cur, prv, w, b):
    row = _iota((SUBLANES, cur.shape[1]), 0)

    def shifted(k):
        sh = pltpu.roll(cur, k, 0)
        top = jnp.where(row < k, pltpu.roll(prv, k, 0), sh[:SUBLANES])
        return jnp.concatenate([top, sh[SUBLANES:]], axis=0)

    return _conv_taps(cur, shifted, w, b)


def _conv_silu_batch(cur, stt, w, b):
    ntok = cur.shape[0]
    tpos = _iota((ntok, 1), 0) & (SUBLANES - 1)

    def shifted(k):
        return jnp.where(tpos < k, pltpu.roll(stt, (k - SUBLANES) % ntok, 0), pltpu.roll(cur, k, 0))

    return _conv_taps(cur, shifted, w, b)


def _expand_heads(rows):
    head_of_lane = _iota((M_R, M_GW), 1) >> _log2(M_HEADDIM)
    sel = (head_of_lane == _iota((M_R, M_GW), 0)).astype(BF16)
    return _xdot(rows, sel, terms=2)


def _ssd_common(x, bm, cm, dtr, hp, causal, same, nv, extra_rows):
    ntok = x.shape[0]
    bias, a = hp[0:1], -jnp.exp(hp[1:2])
    dt = _softplus(dtr + bias)
    if nv < ntok and same is None:
        dt = jnp.where(_iota((ntok, 1), 0) < nv, dt, 0.0)
    da = dt * a
    cum = _xdot_left(causal.astype(BF16), da)
    cum_tot = cum[ntok - 1:ntok] if same is None else _xdot_left(same.astype(BF16), da)
    cbm = _dot_nt(cm, bm)
    eye8 = (_iota((M_R, M_R), 0) == _iota((M_R, M_R), 1)).astype(BF16)
    dt_t = _xdot_nt(eye8, dt)
    even = (_iota((1, M_GW), 1) & M_HEADDIM) == 0
    x_e = jnp.where(even, x, 0.0).astype(BF16)
    x_o = jnp.where(even, 0.0, x).astype(BF16)
    yield
    cum2 = cum * LOG2_E
    cum2_t = _xdot_nt(eye8, cum2)
    tail = jnp.exp(cum_tot - cum) * dt
    ex = _expand_heads(jnp.concatenate([jnp.exp(cum), tail, extra_rows(cum_tot)], axis=0))
    yield
    parts = []
    for q in range(M_R // 2):
        acc = None
        for r, x_m in ((2 * q, x_e), (2 * q + 1, x_o)):
            dec = jnp.exp2(jnp.where(causal, cum2[:, r:r + 1] - cum2_t[r:r + 1, :], -jnp.inf))
            sc = (cbm * dec * dt_t[r:r + 1, :]).astype(BF16)
            yy = jnp.dot(sc, x_m[:, q * LANES:(q + 1) * LANES], preferred_element_type=F32)
            acc = yy if acc is None else acc + yy
        parts.append(acc)
    return jnp.concatenate(parts, axis=1), ex, cum_tot


def _ssd_finish(y, x, zc, d_row, ng):
    yv = (y + d_row * x) * _silu(zc)
    return yv * lax.rsqrt(jnp.mean(yv * yv, axis=-1, keepdims=True) + M_NORM_EPS) * ng


def _ssd_prompt_chunk(nv, zc, xr, br, cr, dtr, prev, st, cw, cb, hp, ng):
    n = SSD_CHUNK
    new_prev = (xr[nv - SUBLANES:], br[nv - SUBLANES:], cr[nv - SUBLANES:])
    zc, xr, br, cr, dtr = (_pad_rows(v, n) for v in (zc, xr, br, cr, dtr))
    x = _conv_silu_seq(xr, prev[0], cw[0], cb[0])
    bm = _conv_silu_seq(br, prev[1], cw[1], cb[1])
    cm = _conv_silu_seq(cr, prev[2], cw[2], cb[2])
    causal = _iota((n, n), 0) >= _iota((n, n), 1)
    row = _iota((SUBLANES, M_R), 0)

    def extra_rows(cum_tot):
        return jnp.where(row == 0, jnp.exp(cum_tot), jnp.where(row == 1, hp[2:3], 0.0))

    y_state = _dot(cm, st)
    y, ex, _ = yield from _ssd_common(x, bm, cm, dtr, hp, causal, None, nv, extra_rows)
    upd = _dot_tn(bm, ex[n:2 * n] * x)
    yield
    y = y + y_state * ex[:n]
    st = st * ex[2 * n:2 * n + 1] + upd
    return _ssd_finish(y, x, zc, ex[2 * n + 1:2 * n + 2], ng)[:nv], new_prev, st


def _ssd_sample_chunk(nb, zc, xr, br, cr, dtr, stt, states, cw, cb, hp, ng):
    n = SUBLANES
    ntok = nb * n
    x = _conv_silu_batch(xr, stt[0], cw[0], cb[0])
    bm = _conv_silu_batch(br, stt[1], cw[1], cb[1])
    cm = _conv_silu_batch(cr, stt[2], cw[2], cb[2])
    same, causal = _seq_masks(ntok, n)
    row = _iota((SUBLANES, M_R), 0)
    ys = [_dot_nt(_pad_rows(cm[b * n:(b + 1) * n], BF16_ROWS), states[b])[:n] for b in range(nb)]
    y, ex, cum_tot = yield from _ssd_common(x, bm, cm, dtr, hp, causal, same, n,
                                            lambda _: jnp.where(row == 0, hp[2:3], 0.0))
    xt = ex[ntok:2 * ntok] * x
    upds = [_dot_tn(_pad_rows(xt[b * n:(b + 1) * n], BF16_ROWS), _pad_rows(bm[b * n:(b + 1) * n], BF16_ROWS))
            for b in range(nb)]
    yield
    etot = jnp.exp(cum_tot)
    new_states = []
    for b in range(nb):
        decay = jnp.concatenate([jnp.broadcast_to(etot[b * n:b * n + 1, r:r + 1], (M_HEADDIM, M_D_STATE))
                                 for r in range(M_R)], axis=0)
        new_states.append(states[b] * decay + upds[b])
    y = y + jnp.concatenate(ys, axis=0) * ex[:ntok]
    return _ssd_finish(y, x, zc, ex[2 * ntok:2 * ntok + 1], ng), new_states


def _ssd_body(*refs, t, nb, has_state):
    if has_state:
        (z_ref, x_ref, b_ref, c_ref, dt_ref, cwx_ref, cwb_ref, cwc_ref, cbx_ref, cbb_ref, cbc_ref,
         hp_ref, ng_ref, csx_ref, csb_ref, csc_ref, sin_ref, y_ref, sout_ref) = refs
    else:
        (z_ref, x_ref, b_ref, c_ref, dt_ref, cwx_ref, cwb_ref, cwc_ref, cbx_ref, cbb_ref, cbc_ref,
         hp_ref, ng_ref, y_ref, sout_ref) = refs

    def consts(gi):
        wide = slice(gi * M_GW, (gi + 1) * M_GW)
        nar = slice(gi * M_D_STATE, (gi + 1) * M_D_STATE)
        cw = (cwx_ref[:, wide], cwb_ref[:, nar], cwc_ref[:, nar])
        cb = (cbx_ref[:, wide], cbb_ref[:, nar], cbc_ref[:, nar])
        return wide, nar, cw, cb, hp_ref[gi], ng_ref[:, wide]

    if has_state:
        ntok = nb * t
        gens = []
        for gi in range(SSD_GPS):
            wide, nar, cw, cb, hp, ng = consts(gi)
            stt = (csx_ref[:, :, wide].reshape(ntok, M_GW), csb_ref[:, :, nar].reshape(ntok, M_D_STATE),
                   csc_ref[:, :, nar].reshape(ntok, M_D_STATE))
            states = [sin_ref[b, gi * M_R:(gi + 1) * M_R].reshape(M_GW, M_D_STATE) for b in range(nb)]
            gens.append(_ssd_sample_chunk(nb, z_ref[:, wide], x_ref[:, wide], b_ref[:, nar], c_ref[:, nar],
                                          dt_ref[gi], stt, states, cw, cb, hp, ng))
        for gi, (out, states) in enumerate(_lockstep(gens)):
            y_ref[:, gi * M_GW:(gi + 1) * M_GW] = out.astype(y_ref.dtype)
            for b in range(nb):
                sout_ref[b, gi * M_R:(gi + 1) * M_R] = states[b].reshape(M_R, M_HEADDIM, M_D_STATE)
        return

    def run(start, nv, carry):
        rows = pl.ds(pl.multiple_of(start, SUBLANES), nv)
        gens = []
        for gi in range(SSD_GPS):
            wide, nar, cw, cb, hp, ng = consts(gi)
            prev, st = carry[gi]
            gens.append(_ssd_prompt_chunk(nv, z_ref[rows, wide], x_ref[rows, wide], b_ref[rows, nar],
                                          c_ref[rows, nar], dt_ref[gi, rows, :], prev, st, cw, cb, hp, ng))
        new = []
        for gi, (out, prev, st) in enumerate(_lockstep(gens)):
            y_ref[rows, gi * M_GW:(gi + 1) * M_GW] = out.astype(y_ref.dtype)
            new.append((prev, st))
        return tuple(new)

    zero_prev = (jnp.zeros((SUBLANES, M_GW), F32), jnp.zeros((SUBLANES, M_D_STATE), F32),
                 jnp.zeros((SUBLANES, M_D_STATE), F32))
    carry = tuple((zero_prev, jnp.zeros((M_D_STATE, M_GW), F32)) for _ in range(SSD_GPS))
    n_full = t // SSD_CHUNK
    carry = lax.fori_loop(0, n_full, lambda c, cr: run(c * SSD_CHUNK, SSD_CHUNK, cr), carry)
    if t % SSD_CHUNK:
        carry = run(n_full * SSD_CHUNK, t % SSD_CHUNK, carry)
    for gi in range(SSD_GPS):
        sout_ref[0, gi * M_R:(gi + 1) * M_R] = carry[gi][1].T.reshape(M_R, M_HEADDIM, M_D_STATE)


def _ssd(j, z, xbc, dtg, m_conv_w, conv_b3, hp, m_norm_g3, group, conv_state=None, ssm_state=None,
         y_into=None, state_into=None):
    nseq, t, row0, nb = group
    rb = nb * t
    r0 = row0 // rb
    assert row0 % rb == 0 and nseq % nb == 0 and M_GROUPS % SSD_GPS == 0
    has_state = ssm_state is not None
    gw, gn, gr = SSD_GPS * M_GW, SSD_GPS * M_D_STATE, SSD_GPS * M_R
    b0 = M_D_INNER // gn
    c0 = b0 + M_GROUPS // SSD_GPS
    one = _single(not has_state)
    in_specs = [
        pl.BlockSpec((rb, gw), lambda s, g: (s + r0, g), **one),
        pl.BlockSpec((rb, gw), lambda s, g: (s + r0, g), **one),
        pl.BlockSpec((rb, gn), lambda s, g: (s + r0, b0 + g), **one),
        pl.BlockSpec((rb, gn), lambda s, g: (s + r0, c0 + g), **one),
        pl.BlockSpec((SSD_GPS, rb, M_R), lambda s, g: (g, s + r0, 0), **one),
        pl.BlockSpec((None, M_CONV, gw), lambda s, g: (j, 0, g)),
        pl.BlockSpec((None, M_CONV, gn), lambda s, g: (j, 0, b0 + g)),
        pl.BlockSpec((None, M_CONV, gn), lambda s, g: (j, 0, c0 + g)),
        pl.BlockSpec((None, 1, gw), lambda s, g: (j, 0, g)),
        pl.BlockSpec((None, 1, gn), lambda s, g: (j, 0, b0 + g)),
        pl.BlockSpec((None, 1, gn), lambda s, g: (j, 0, c0 + g)),
        pl.BlockSpec((None, SSD_GPS, 3, M_R), lambda s, g: (j, g, 0, 0)),
        pl.BlockSpec((None, 1, gw), lambda s, g: (j, 0, g)),
    ]
    args = [z, xbc, xbc, xbc, dtg, m_conv_w, m_conv_w, m_conv_w, conv_b3, conv_b3, conv_b3, hp, m_norm_g3]
    if has_state:
        in_specs += [
            pl.BlockSpec((None, nb, SUBLANES, gw), lambda s, g: (j, s, 0, g)),
            pl.BlockSpec((None, nb, SUBLANES, gn), lambda s, g: (j, s, 0, b0 + g)),
            pl.BlockSpec((None, nb, SUBLANES, gn), lambda s, g: (j, s, 0, c0 + g)),
            pl.BlockSpec((None, nb, gr, M_HEADDIM, M_D_STATE), lambda s, g: (j, s, g, 0, 0)),
        ]
        args += [conv_state, conv_state, conv_state, ssm_state]
    body, aliases = _write_into(functools.partial(_ssd_body, t=t, nb=nb, has_state=has_state),
                                in_specs, args, (y_into, state_into))
    return pl.pallas_call(
        body,
        grid=(nseq // nb, M_GROUPS // SSD_GPS),
        in_specs=in_specs,
        out_specs=[pl.BlockSpec((rb, gw), lambda s, g: (s + r0, g)),
                   pl.BlockSpec((None, nb, gr, M_HEADDIM, M_D_STATE), lambda s, g: (j, s, g, 0, 0))],
        out_shape=[jax.ShapeDtypeStruct((z.shape[0], M_D_INNER), BF16),
                   jax.ShapeDtypeStruct((hp.shape[0], nseq, M_HEADS, M_HEADDIM, M_D_STATE), F32)],
        input_output_aliases=aliases,
        compiler_params=_cparams(2),
        name="ssd_sample" if has_state else "ssd_prompt",
    )(*args)


def _mamba_layer(j, u, groups, m_w_in, m_conv_w, m_conv_b, m_dt_bias, m_a_log, m_d, m_norm_g, m_w_out,
                 state_conv_pad, state_ssm, ssm_into=(None, None)):
    m = u.shape[0]
    gp, gs = groups
    w_in_t = jnp.swapaxes(m_w_in, 1, 2)
    z = _mm(u, w_in_t, (j,), 0, M_D_INNER, w_t=True, name="ssd_in_z")
    xbc = _mm(u, w_in_t, (j,), M_D_INNER, M_CONV_DIM, w_t=True, name="ssd_in_xbc")
    dt = _mm(u, w_in_t, (j,), M_D_INNER + M_CONV_DIM, LANES, tn=LANES, w_t=True, name="ssd_in_dt")[:, :M_HEADS]
    dtg = dt.reshape(m, M_GROUPS, M_R).transpose(1, 0, 2)
    hp = jnp.stack([m_dt_bias, m_a_log, m_d], axis=1).reshape(-1, 3, M_GROUPS, M_R).transpose(0, 2, 1, 3)
    conv_b3 = m_conv_b[:, None, :]
    ng3 = m_norm_g[:, None, :]
    y, sp = _ssd(j, z, xbc, dtg, m_conv_w, conv_b3, hp, ng3, gp, state_into=ssm_into[0])
    y, ss = _ssd(j, z, xbc, dtg, m_conv_w, conv_b3, hp, ng3, gs, state_conv_pad, state_ssm,
                 y_into=y, state_into=ssm_into[1])
    y = _mm(y, m_w_out, (j,), tm=928, name="ssd_out")
    keep = M_CONV - 1
    xp = jnp.stack([xbc[(b + 1) * gp[1] - keep:(b + 1) * gp[1]] for b in range(gp[0])])
    xs = xbc[gp[0] * gp[1]:].reshape(gs[0], gs[1], M_CONV_DIM)[:, gs[1] - keep:]
    return y, (sp, xp), (ss, xs)


R_TOK = 64
R_PAIRS = 4


def _rwkv_chunk(nb, nv, toks, sbs, prm):
    n = R_TOK // nb
    ntok = R_TOK
    r, kr, v, wl, al, g = (_pad_rows(t_, ntok) for t_ in toks)
    w0, a0, k_k, k_a, r_k, ln_g, ln_b = prm
    m0 = _iota((1, LANES), 1) < R_HEADSIZE
    ones_blk = ((_iota((LANES, LANES), 0) < R_HEADSIZE) == (_iota((LANES, LANES), 1) < R_HEADSIZE)).astype(BF16)

    w = -_softplus(-(w0 + wl)) - 0.5
    lw = -jnp.exp(w)
    a = _sigmoid(a0 + al)
    kk = kr * k_k
    k = kr * (1.0 + (a - 1.0) * k_a)
    if nv < n:
        valid = _iota((ntok, 1), 0) < nv
        lw = jnp.where(valid, lw, 0.0)
        kk = jnp.where(valid, kk, 0.0)
        k = jnp.where(valid, k, 0.0)
        v = jnp.where(valid, v, 0.0)
    same, causal = _seq_masks(ntok, n)
    head_sums = _xdot(jnp.concatenate([kk * kk, r * k * r_k], axis=0), ones_blk, terms=2)
    lam = _xdot_left(causal.astype(BF16), lw)
    lam_tot = lam[ntok - 1:ntok] if nb == 1 else _xdot_left(same.astype(BF16), lw)
    yield
    bonus = head_sums[ntok:] * v
    kk = kk / jnp.maximum(jnp.sqrt(head_sums[:ntok]), 1e-12)
    beta = kk * a
    e_in = jnp.exp(lam)
    e_out = jnp.exp(-lam)
    e_end = jnp.exp(lam_tot - lam)

    def stack(x):
        x3 = x.reshape(nb, n, LANES)
        s = jnp.concatenate([jnp.where(m0, x3, 0.0), jnp.where(m0, 0.0, x3)], axis=1)
        return s.reshape(2 * ntok, LANES).astype(BF16)

    am, rm = stack(kk * jnp.exp(lam - lw)), stack(r * e_in)
    khm, bhm = stack(k * e_out), stack(beta * e_out)
    vst = stack(v)
    n2 = 2 * ntok
    rr, cc = _iota((n2, n2), 0), _iota((n2, n2), 1)
    same_blk = (rr >> _log2(n)) == (cc >> _log2(n))
    strict = same_blk & ((rr & (n - 1)) > (cc & (n - 1)))
    incl = same_blk & ((rr & (n - 1)) >= (cc & (n - 1)))
    qm = jnp.concatenate([am, rm], axis=0)
    pmat = _dot_nt(qm, jnp.concatenate([khm, bhm], axis=0))
    w2 = 2 * n
    if nb == 1:
        qs = [_dot_nt(qm, sbs[0])]
    else:
        qs = [_dot_nt(jnp.concatenate([am[b * w2:(b + 1) * w2], rm[b * w2:(b + 1) * w2]], axis=0), sbs[b])
              for b in range(nb)]
    ktm, btm = stack(k * e_end), stack(beta * e_end)
    yield
    a_ak = jnp.where(strict, pmat[:n2, :n2], 0.0)
    nmat = jnp.where(strict, -pmat[:n2, n2:], 0.0)
    a_rk = jnp.where(incl, pmat[n2:, :n2], 0.0)
    a_rb = jnp.where(incl, pmat[n2:, n2:], 0.0)
    rhs_s = qs[0][:w2] if nb == 1 else jnp.concatenate([q[:w2] for q in qs], axis=0)
    ys = qs[0][w2:] if nb == 1 else jnp.concatenate([q[w2:] for q in qs], axis=0)

    tm = (rr == cc).astype(F32) + nmat
    npow = _dot(nmat, nmat)
    av = _dot(jnp.concatenate([a_ak, a_rk], axis=0), vst)
    yield
    rhs = rhs_s + av[:n2]
    ys = ys + av[n2:]
    for i in range(_log2(n) - 1):
        if i + 2 < _log2(n):
            both = _dot(jnp.concatenate([tm, npow], axis=0), npow)
            prod, npow = both[:n2], both[n2:]
        else:
            prod = _dot(tm, npow)
        yield
        tm = tm + prod
    u = _dot(tm, rhs)
    yield
    yst = (ys - _dot(a_rb, u)).reshape(nb, w2, LANES)
    nu = (-u).astype(BF16)
    new_sbs = []
    for b in range(nb):
        rows = slice(b * w2, (b + 1) * w2)
        lt = lam_tot if nb == 1 else lam_tot[b * n:b * n + 1]
        new_sbs.append(sbs[b] * jnp.exp(lt)
                       + _dot_tn(jnp.concatenate([vst[rows], nu[rows]], axis=0),
                                 jnp.concatenate([ktm[rows], btm[rows]], axis=0)))
    yield
    y = (yst[:, :n] + yst[:, n:]).reshape(ntok, LANES)
    inv = 1.0 / R_HEADSIZE
    mean = _xdot(y, ones_blk, terms=2) * inv
    yield
    yc = y - mean
    var = _xdot(yc * yc, ones_blk, terms=2) * inv
    yield
    yn = yc * lax.rsqrt(var + R_LN_EPS) * ln_g + ln_b
    return ((yn + bonus) * g)[:nb * nv], new_sbs


def _rwkv_body(*refs, t, nb, has_state):
    if has_state:
        (r_ref, k_ref, v_ref, wl_ref, al_ref, g_ref, prm_ref, sin_ref, y_ref, sout_ref) = refs
    else:
        (r_ref, k_ref, v_ref, wl_ref, al_ref, g_ref, prm_ref, y_ref, sout_ref) = refs
    tok_refs = (r_ref, k_ref, v_ref, wl_ref, al_ref, g_ref)
    zeros = jnp.zeros((R_HEADSIZE, R_HEADSIZE), F32)

    def lanes(p):
        return slice(p * LANES, (p + 1) * LANES)

    def run(rows, nv, states):
        res = _lockstep([
            _rwkv_chunk(nb, nv, tuple(ref[rows, lanes(p)] for ref in tok_refs), states[p],
                        tuple(prm_ref[i:i + 1, lanes(p)] for i in range(7)))
            for p in range(R_PAIRS)])
        for p, (out, _) in enumerate(res):
            y_ref[rows, lanes(p)] = out.astype(y_ref.dtype)
        return tuple(tuple(sbs) for _, sbs in res)

    if has_state:
        states = tuple(
            tuple(jnp.concatenate([jnp.concatenate([sin_ref[b, 2 * p], zeros], axis=1),
                                   jnp.concatenate([zeros, sin_ref[b, 2 * p + 1]], axis=1)], axis=0)
                  for b in range(nb))
            for p in range(R_PAIRS))
    else:
        states = tuple(tuple(jnp.zeros((LANES, LANES), F32) for _ in range(nb)) for _ in range(R_PAIRS))

    if nb == 1:
        n_full = t // R_TOK
        states = lax.fori_loop(
            0, n_full, lambda c, st: run(pl.ds(pl.multiple_of(c * R_TOK, R_TOK), R_TOK), R_TOK, st), states)
        if t % R_TOK:
            states = run(pl.ds(n_full * R_TOK, t % R_TOK), t % R_TOK, states)
    else:
        assert nb * t == R_TOK
        states = run(slice(None), t, states)
    for p in range(R_PAIRS):
        for b in range(nb):
            sout_ref[b, 2 * p] = states[p][b][:R_HEADSIZE, :R_HEADSIZE]
            sout_ref[b, 2 * p + 1] = states[p][b][R_HEADSIZE:, R_HEADSIZE:]


def _rwkv(r, k, v, wl, al, g, prm, group, wkv_state=None, y_into=None):
    nseq, t, row0, nb = group
    rb = nb * t
    r0 = row0 // rb
    assert row0 % rb == 0 and nseq % nb == 0
    has_state = wkv_state is not None
    gl = R_PAIRS * LANES
    tok = pl.BlockSpec((rb, gl), lambda s, hp_: (s + r0, hp_), **_single(not has_state))
    in_specs = [tok] * 6 + [pl.BlockSpec((SUBLANES, gl), lambda s, hp_: (0, hp_))]
    args = [r, k, v, wl, al, g, prm]
    if has_state:
        in_specs.append(pl.BlockSpec((None, nb, 2 * R_PAIRS, R_HEADSIZE, R_HEADSIZE),
                                     lambda s, hp_: (0, s, hp_, 0, 0)))
        args.append(wkv_state)
    body, aliases = _write_into(functools.partial(_rwkv_body, t=t, nb=nb, has_state=has_state),
                                in_specs, args, (y_into,))
    return pl.pallas_call(
        body,
        grid=(nseq // nb, R_HEADS // (2 * R_PAIRS)),
        in_specs=in_specs,
        out_specs=[pl.BlockSpec((rb, gl), lambda s, hp_: (s + r0, hp_)),
                   pl.BlockSpec((nb, 2 * R_PAIRS, R_HEADSIZE, R_HEADSIZE), lambda s, hp_: (s, hp_, 0, 0))],
        out_shape=[jax.ShapeDtypeStruct((r.shape[0], D_MODEL), BF16),
                   jax.ShapeDtypeStruct((nseq, R_HEADS, R_HEADSIZE, R_HEADSIZE), F32)],
        input_output_aliases=aliases,
        compiler_params=_cparams(2),
        name="rwkv_sample" if has_state else "rwkv_prompt",
    )(*args)


MIX_COLS = 512


def _mix_body(*refs, t, nb, has_state):
    if has_state:
        u_ref, mu_ref, sh_ref = refs[:3]
    else:
        u_ref, mu_ref = refs[:2]
    o_refs = refs[-6:]
    u = u_ref[...]
    rows = nb * t
    tpos = _iota((rows, 1), 0) & (t - 1) if nb > 1 else _iota((rows, 1), 0)
    prev = pltpu.roll(u, 1, 0)
    if has_state:
        pick = (_iota((rows, BF16_ROWS), 0) >> _log2(t)) == _iota((rows, BF16_ROWS), 1)
        first = _xdot_left(pick.astype(BF16), _pad_rows(sh_ref[...], BF16_ROWS))
    else:
        first = 0.0
    dx = jnp.where(tpos == 0, first, prev) - u
    for i, o_ref in enumerate(o_refs):
        o_ref[...] = (u + dx * mu_ref[i:i + 1, :]).astype(o_ref.dtype)


def _rwkv_mix(u, r_mu, group, shift_state=None, into=None):
    nseq, t, row0, nb = group
    m, d = u.shape
    rb = nb * t
    r0 = row0 // rb
    has_state = shift_state is not None
    assert row0 % rb == 0 and nseq % nb == 0 and (nb == 1 or nb <= BF16_ROWS)
    blk = pl.BlockSpec((rb, MIX_COLS), lambda s, c: (s + r0, c))
    in_specs = [blk, pl.BlockSpec((None, 6, MIX_COLS), lambda s, c: (0, 0, c))]
    args = [u, r_mu]
    if has_state:
        in_specs.append(pl.BlockSpec((None, nb, MIX_COLS), lambda s, c: (0, s, c)))
        args.append(shift_state)
    body, aliases = _write_into(functools.partial(_mix_body, t=t, nb=nb, has_state=has_state),
                                in_specs, args, into or (None,) * 6)
    return pl.pallas_call(
        body,
        grid=(nseq // nb, d // MIX_COLS),
        in_specs=in_specs,
        out_specs=[blk] * 6,
        out_shape=[jax.ShapeDtypeStruct((m, d), BF16)] * 6,
        input_output_aliases=aliases,
        compiler_params=_cparams(2),
        name="rwkv_mix_sample" if has_state else "rwkv_mix_prompt",
    )(*args)


def _rwkv_layer(u, groups, state_wkv, state_shift, r_mu, r_w_r, r_w_k, r_w_v, r_w0, r_w1, r_w2, r_a0, r_a1,
                r_a2, r_g1, r_g2, r_k_k, r_k_a, r_r_k, r_ln_g, r_ln_b, r_w_o):
    gp, gs = groups
    mixed = _rwkv_mix(u, r_mu, gp)
    xr, xw, xk, xv, xa, xg = _rwkv_mix(u, r_mu, gs, state_shift, into=mixed)
    r = _mm(xr, r_w_r, (0,), name="rwkv_r")
    k = _mm(xk, r_w_k, (0,), name="rwkv_k")
    v = _mm(xv, r_w_v, (0,), name="rwkv_v")
    wl = _mm(_mm(xw, r_w1, (0,), out_dtype=BF16, act="tanh", name="rwkv_w1"), r_w2, (0,), name="rwkv_w2")
    al = _mm(_mm(xa, r_a1, (0,), out_dtype=BF16, name="rwkv_a1"), r_a2, (0,), name="rwkv_a2")
    g = _mm(_mm(xg, r_g1, (0,), out_dtype=BF16, act="sigmoid", name="rwkv_g1"), r_g2, (0,), name="rwkv_g2")
    prm = jnp.concatenate([r_w0, r_a0, r_k_k, r_k_a, r_r_k.reshape(1, D_MODEL), r_ln_g, r_ln_b,
                           jnp.zeros((1, D_MODEL), F32)], axis=0)
    y, sp = _rwkv(r, k, v, wl, al, g, prm, gp)
    y, ss = _rwkv(r, k, v, wl, al, g, prm, gs, state_wkv, y_into=y)
    y = _mm(y, r_w_o, (0,), name="rwkv_out")
    last_p = jnp.stack([u[(b + 1) * gp[1] - 1] for b in range(gp[0])])
    last_s = u[gp[0] * gp[1]:].reshape(gs[0], gs[1], D_MODEL)[:, -1]
    return y, (sp, last_p), (ss, last_s)


H_CHUNK = 128
H_SUB = 16
H_HPS = 4


def _gla_diag(q3, g3, k3, v3):
    sub = q3.shape[1]
    pieces = [slice(p, p + SUBLANES) for p in range(0, sub, SUBLANES)]
    tpos = _iota((1, SUBLANES, 1), 1)
    g3 = g3 * LOG2_E
    ys = [jnp.zeros(v3[:, p].shape, F32) for p in pieces]
    for s in range(sub):
        terms = []
        for i, p in enumerate(pieces):
            if p.stop <= s:
                continue
            arg = g3[:, p] - g3[:, s:s + 1, :]
            if p.start < s:
                arg = jnp.where(tpos + p.start >= s, arg, -jnp.inf)
            terms.append((i, jnp.sum(q3[:, p] * k3[:, s:s + 1, :] * jnp.exp2(arg), axis=-1, keepdims=True)))
        yield
        for i, a_s in terms:
            ys[i] = ys[i] + a_s * v3[:, s:s + 1, :]
    return ys[0] if len(ys) == 1 else jnp.concatenate(ys, axis=1)


def _gla_prompt_chunk(nv, q, logf, k, v, st):
    n = -(-nv // H_SUB) * H_SUB
    q, logf, k, v = (_pad_rows(t_, n) for t_ in (q, logf, k, v))
    causal = _iota((n, n), 0) >= _iota((n, n), 1)
    gcum = _xdot_left(causal.astype(BF16), logf)
    yield
    y = _dot_nt(q * jnp.exp(gcum), st)
    g_n = gcum[n - 1:n]
    upd = _dot_tn(v, k * jnp.exp(g_n - gcum))
    atts = []
    for i in range(1, n // H_SUB):
        lo = i * H_SUB
        gs = gcum[lo - 1:lo]
        atts.append(_dot_nt(q[lo:lo + H_SUB] * jnp.exp(gcum[lo:lo + H_SUB] - gs),
                            k[:lo] * jnp.exp(gs - gcum[:lo])))
    yield
    st = st * jnp.exp(g_n) + upd
    offs = [jnp.zeros((H_SUB, LANES), F32)] + [_dot(att, v[:(i + 1) * H_SUB]) for i, att in enumerate(atts)]
    shape3 = (n // H_SUB, H_SUB, LANES)
    y3 = yield from _gla_diag(q.reshape(shape3), gcum.reshape(shape3), k.reshape(shape3), v.reshape(shape3))
    off = offs[0] if len(offs) == 1 else jnp.concatenate(offs, axis=0)
    return (y + off + y3.reshape(n, LANES))[:nv], st


def _gla_sample_chunk(nb, q, logf, k, v, sts):
    n = SUBLANES
    ntok = nb * n
    _, causal = _seq_masks(ntok, n)
    gcum = _xdot_left(causal.astype(BF16), logf)
    yield
    shape3 = (nb, n, LANES)
    g3 = gcum.reshape(shape3)
    g_tot = g3[:, n - 1:n, :]
    qe = q * jnp.exp(gcum)
    kd = (k.reshape(shape3) * jnp.exp(g_tot - g3)).reshape(ntok, LANES)
    ys = [_dot_nt(_pad_rows(qe[b * n:(b + 1) * n], BF16_ROWS), sts[b])[:n] for b in range(nb)]
    upds = [_dot_tn(_pad_rows(v[b * n:(b + 1) * n], BF16_ROWS), _pad_rows(kd[b * n:(b + 1) * n], BF16_ROWS))
            for b in range(nb)]
    y3 = yield from _gla_diag(q.reshape(shape3), g3, k.reshape(shape3), v.reshape(shape3))
    new = [sts[b] * jnp.exp(g_tot[b]) + upds[b] for b in range(nb)]
    return y3.reshape(ntok, LANES) + jnp.concatenate(ys, axis=0), new


def _hgrn_body(*refs, t, nb, has_state, layer):
    if has_state:
        (q_ref, f_ref, i_ref, g_ref, lb_ref, ng_ref, sin_ref, y_ref, sout_ref) = refs
    else:
        (q_ref, f_ref, i_ref, g_ref, lb_ref, ng_ref, y_ref, sout_ref) = refs
    hl = lb_ref[...]
    e = jnp.exp(hl - jnp.max(hl, axis=0, keepdims=True))
    sm = e / jnp.sum(e, axis=0, keepdims=True)
    lb_all = (jnp.sum(sm[1:layer + 1], axis=0, keepdims=True) if layer > 0
              else jnp.zeros((1, H_HPS * LANES), F32))
    ng = ng_ref[...]

    def lanes(hi):
        return slice(hi * LANES, (hi + 1) * LANES)

    def gates(rows, hi):
        lb = lb_all[:, lanes(hi)]
        q = _silu(q_ref[rows, lanes(hi)])
        f = lb + (1.0 - lb) * _sigmoid(f_ref[rows, lanes(hi)])
        return q, jnp.log(f), 1.0 - f, i_ref[rows, lanes(hi)]

    def finish(rows, hi, yy):
        yy = yy * lax.rsqrt(jnp.mean(yy * yy, axis=-1, keepdims=True) + EPS) * ng
        y_ref[rows, lanes(hi)] = (yy * _silu(g_ref[rows, lanes(hi)])).astype(y_ref.dtype)

    if has_state:
        rows = slice(None)
        res = _lockstep([_gla_sample_chunk(nb, *gates(rows, hi), [sin_ref[b, hi].T for b in range(nb)])
                         for hi in range(H_HPS)])
        for hi, (yy, sts) in enumerate(res):
            finish(rows, hi, yy)
            for b in range(nb):
                sout_ref[b, hi] = sts[b].T
        return

    def run(start, nv, sts):
        rows = pl.ds(pl.multiple_of(start, SUBLANES), nv)
        res = _lockstep([_gla_prompt_chunk(nv, *gates(rows, hi), sts[hi]) for hi in range(H_HPS)])
        for hi, (yy, _) in enumerate(res):
            finish(rows, hi, yy)
        return tuple(st for _, st in res)

    sts = tuple(jnp.zeros((H_VDIM, H_EXPAND), F32) for _ in range(H_HPS))
    n_full = t // H_CHUNK
    sts = lax.fori_loop(0, n_full, lambda c, s: run(c * H_CHUNK, H_CHUNK, s), sts)
    if t % H_CHUNK:
        sts = run(n_full * H_CHUNK, t % H_CHUNK, sts)
    for hi in range(H_HPS):
        sout_ref[0, hi] = sts[hi].T


def _hgrn(layer, proj, h_lb, h_norm_g, group, state=None, y_into=None):
    nseq, t, row0, nb = group
    rb = nb * t
    r0 = row0 // rb
    assert row0 % rb == 0 and nseq % nb == 0
    has_state = state is not None
    nhb = H_HEADS // H_HPS
    hw = H_HPS * LANES

    def tok(section):
        return pl.BlockSpec((rb, hw), lambda s, h: (s + r0, h + section * nhb))

    in_specs = [tok(0), tok(1), tok(2), tok(3),
                pl.BlockSpec((DEPTH, hw), lambda s, h: (0, h)),
                pl.BlockSpec((1, LANES), lambda s, h: (0, 0))]
    args = [proj, proj, proj, proj, h_lb, h_norm_g]
    if has_state:
        in_specs.append(pl.BlockSpec((None, nb, H_HPS, H_EXPAND, H_VDIM), lambda s, h: (0, s, h, 0, 0)))
        args.append(state)
    body, aliases = _write_into(functools.partial(_hgrn_body, t=t, nb=nb, has_state=has_state, layer=layer),
                                in_specs, args, (y_into,))
    return pl.pallas_call(
        body,
        grid=(nseq // nb, nhb),
        in_specs=in_specs,
        out_specs=[pl.BlockSpec((rb, hw), lambda s, h: (s + r0, h)),
                   pl.BlockSpec((nb, H_HPS, H_EXPAND, H_VDIM), lambda s, h: (s, h, 0, 0))],
        out_shape=[jax.ShapeDtypeStruct((proj.shape[0], D_MODEL), BF16),
                   jax.ShapeDtypeStruct((nseq, H_HEADS, H_EXPAND, H_VDIM), F32)],
        input_output_aliases=aliases,
        compiler_params=_cparams(2),
        name="hgrn_sample" if has_state else "hgrn_prompt",
    )(*args)


def _hgrn_layer(layer, u, groups, state_hgrn, h_w_in, h_lb, h_norm_g, h_w_out):
    proj = _mm(u, h_w_in, (0,), name="hgrn_in")
    y, sp = _hgrn(layer, proj, h_lb, h_norm_g, groups[0])
    y, ss = _hgrn(layer, proj, h_lb, h_norm_g, groups[1], state_hgrn, y_into=y)
    y = _mm(y, h_w_out, (0,), name="hgrn_out")
    return y, sp, ss


def kernel(x_prompt, x_sample, state_ssm, state_conv, state_wkv, state_shift, state_hgrn, meta_tokens, norm_g, ffn_w_gate, ffn_w_up, ffn_w_down, m_w_in, m_conv_w, m_conv_b, m_dt_bias, m_a_log, m_d, m_norm_g, m_w_out, r_mu, r_w_r, r_w_k, r_w_v, r_w0, r_w1, r_w2, r_a0, r_a1, r_a2, r_g1, r_g2, r_k_k, r_k_a, r_r_k, r_ln_g, r_ln_b, r_w_o, h_w_in, h_lb, h_norm_g, h_w_out):
    pieces = [p for b in range(BATCH) for p in (meta_tokens, x_prompt[b])] + [x_sample.reshape(M_S, D_MODEL)]
    h = jnp.concatenate(pieces, axis=0)
    conv_pad = jnp.pad(state_conv, ((0, 0), (0, 0), (SUBLANES - (M_CONV - 1), 0), (0, 0)))

    p_conv, s_conv = [], []
    ssm = (None, None)
    u = _norm(h, norm_g, 0, 0)
    for l in range(DEPTH):
        kind, j = l % 3, l // 3
        act, w_down = _swiglu_up(u, ffn_w_gate, ffn_w_up, ffn_w_down, (l, 0))
        h, u = _down_resnorm(act, w_down, h, norm_g, 0.5, l, 1, l, 2, u_dtype=F32 if kind == 1 else BF16)
        if kind == 0:
            y, (sp, cp), (ss, cs) = _mamba_layer(j, u, GROUPS, m_w_in, m_conv_w, m_conv_b, m_dt_bias, m_a_log,
                                                 m_d, m_norm_g, m_w_out, conv_pad, state_ssm, ssm)
            ssm = (sp, ss)
            p_conv.append(cp)
            s_conv.append(cs)
        elif kind == 1:
            y, (p_wkv, p_shift), (s_wkv, s_shift) = _rwkv_layer(
                u, GROUPS, state_wkv, state_shift, r_mu, r_w_r, r_w_k, r_w_v, r_w0, r_w1, r_w2, r_a0, r_a1,
                r_a2, r_g1, r_g2, r_k_k, r_k_a, r_r_k, r_ln_g, r_ln_b, r_w_o)
        else:
            y, p_hgrn, s_hgrn = _hgrn_layer(l, u, GROUPS, state_hgrn, h_w_in, h_lb, h_norm_g, h_w_out)
        h, u = _resnorm(h, y, norm_g, 1.0, l, 3, l, 4)
        act, w_down = _swiglu_up(u, ffn_w_gate, ffn_w_up, ffn_w_down, (l, 1))
        if l + 1 < DEPTH:
            h, u = _down_resnorm(act, w_down, h, norm_g, 0.5, l, 5, l + 1, 0)
        else:
            h, _ = _down_resnorm(act, w_down, h, norm_g, 0.5, l, 5)

    y_prompt = jnp.stack([h[b * T_P + N_META:(b + 1) * T_P] for b in range(BATCH)])
    y_sample = h[M_P:].reshape(DEC_BATCH, T_S, D_MODEL)
    return (y_prompt, y_sample,
            ssm[0], jnp.stack(p_conv), p_wkv[None], p_shift[None], p_hgrn[None],
            ssm[1], jnp.stack(s_conv), s_wkv[None], s_shift[None], s_hgrn[None])
```

```python
import functools

import jax
import jax.numpy as jnp
from jax import lax
from jax.experimental import pallas as pl
from jax.experimental.pallas import tpu as pltpu

F32 = jnp.float32
BF16 = jnp.bfloat16

D_MODEL = 2048
BATCH = 4
SEQ = 2048
DEPTH = 4
DEC_BATCH = 128
DEC_SEQ = 8
N_META = 16
EPS = 1e-6
D_FF = 5632

M_D_INNER = 2 * D_MODEL
M_HEADDIM = 64
M_HEADS = M_D_INNER // M_HEADDIM
M_GROUPS = 8
M_D_STATE = 128
M_CONV = 4
M_CONV_DIM = M_D_INNER + 2 * M_GROUPS * M_D_STATE
M_NORM_EPS = 1e-5
M_GW = M_D_INNER // M_GROUPS
M_R = M_HEADS // M_GROUPS

R_HEADSIZE = 64
R_HEADS = D_MODEL // R_HEADSIZE
R_LN_EPS = 64e-5

H_HEADS = 16
H_EXPAND = 128
H_FDIM = H_HEADS * H_EXPAND
H_VDIM = D_MODEL // H_HEADS

T_P = N_META + SEQ
T_S = DEC_SEQ
M_P = BATCH * T_P
M_S = DEC_BATCH * T_S
M_TOK = M_P + M_S
S_BS = 8
LANES = 128
SUBLANES = 8
BF16_ROWS = 16
VMEM_LIMIT = 56 * 1024 * 1024
LOG2_E = 1.4426950408889634
GROUPS = ((BATCH, T_P, 0, 1), (DEC_BATCH, T_S, M_P, S_BS))


def _cparams(n_axes):
    return pltpu.CompilerParams(dimension_semantics=("arbitrary",) * n_axes,
                                vmem_limit_bytes=VMEM_LIMIT)


def _silu(x):
    h = 0.5 * x
    return h + h * jnp.tanh(h)


def _sigmoid(x):
    return 1.0 / (1.0 + jnp.exp(-x))


def _softplus(x):
    return jnp.maximum(x, 0.0) + jnp.log1p(jnp.exp(-jnp.abs(x)))


def _iota(shape, dim):
    return lax.broadcasted_iota(jnp.int32, shape, dim)


def _log2(n):
    assert n & (n - 1) == 0
    return n.bit_length() - 1


def _dot(a, b):
    return jnp.dot(a.astype(BF16), b.astype(BF16), preferred_element_type=F32)


def _dot_nt(a, b):
    return lax.dot_general(a.astype(BF16), b.astype(BF16), (((1,), (1,)), ((), ())),
                           preferred_element_type=F32)


def _dot_tn(a, b):
    return lax.dot_general(a.astype(BF16), b.astype(BF16), (((0,), (0,)), ((), ())),
                           preferred_element_type=F32)


def _split(x, terms):
    parts = []
    rem = x
    for _ in range(terms):
        hi = rem.astype(BF16)
        parts.append(hi)
        rem = rem - hi.astype(F32)
    return parts


def _xdot(x, sel, terms=3):
    return sum(jnp.dot(p, sel, preferred_element_type=F32) for p in _split(x, terms))


def _xdot_left(sel, x, terms=3):
    return sum(jnp.dot(sel, p, preferred_element_type=F32) for p in _split(x, terms))


def _xdot_nt(sel, x, terms=3):
    return sum(lax.dot_general(sel, p, (((1,), (1,)), ((), ())), preferred_element_type=F32)
               for p in _split(x, terms))


def _pad_rows(x, n):
    if x.shape[0] >= n:
        return x
    return jnp.concatenate([x, jnp.zeros((n - x.shape[0],) + x.shape[1:], x.dtype)], axis=0)


def _seq_masks(ntok, n):
    r, c = _iota((ntok, ntok), 0), _iota((ntok, ntok), 1)
    same = (r >> _log2(n)) == (c >> _log2(n))
    return same, same & (r >= c)


def _lockstep(gens):
    results = [None] * len(gens)
    live = list(enumerate(gens))
    while live:
        still = []
        for i, gen in live:
            try:
                next(gen)
                still.append((i, gen))
            except StopIteration as done:
                results[i] = done.value
        live = still
    return results


def _write_into(body, in_specs, args, targets):
    n_in = len(args)
    aliases = {}
    for k, arr in enumerate(targets):
        if arr is not None:
            aliases[len(args)] = k
            in_specs.append(pl.BlockSpec(memory_space=pl.ANY))
            args.append(arr)
    n_extra = len(args) - n_in

    def wrapped(*refs):
        return body(*refs[:n_in], *refs[n_in + n_extra:])

    return (wrapped if n_extra else body), aliases


def _row_tile(m, want):
    return next(d for d in range(want - want % BF16_ROWS, 0, -BF16_ROWS) if m % d == 0)


def _single(mode):
    return dict(pipeline_mode=pl.Buffered(1)) if mode else {}


def _side_cast_specs(side, nj, ni):
    w, lead = side
    rows, d = w.shape[-2:]
    nsub = min(4, ni)
    sr = rows // (nj * nsub)
    assert rows % (nj * nsub) == 0 and sr % BF16_ROWS == 0

    def block(j, i):
        return j * nsub + jnp.minimum(i, nsub - 1)

    in_spec = pl.BlockSpec((None,) * len(lead) + (sr, d), lambda j, i: lead + (block(j, i), 0))
    out_spec = pl.BlockSpec((sr, d), lambda j, i: (block(j, i), 0))
    return in_spec, out_spec, jax.ShapeDtypeStruct((rows, d), BF16)


def _mm_body(x_ref, w_ref, *rest, act, valid_cols, w_t):
    if len(rest) == 4:
        side_ref, o_ref, side_bf_ref, wbf_ref = rest
    else:
        (o_ref, wbf_ref), side_ref = rest, None

    @pl.when(pl.program_id(1) == 0)
    def _():
        w = w_ref[...].T if w_t else w_ref[...]
        if valid_cols is not None:
            w = jnp.where(_iota(w.shape, 1) < valid_cols, w, 0.0)
        wbf_ref[...] = w.astype(BF16)

    acc = jnp.dot(x_ref[...].astype(BF16), wbf_ref[...], preferred_element_type=F32)
    if act == "tanh":
        acc = jnp.tanh(acc)
    elif act == "sigmoid":
        acc = _sigmoid(acc)
    o_ref[...] = acc.astype(o_ref.dtype)
    if side_ref is not None:
        side_bf_ref[...] = side_ref[...].astype(BF16)


def _mm(x, w, lead=(), col0=0, ncols=None, tn=512, tm=1856, out_dtype=F32, act=None, w_t=False, side=None,
        name="mm"):
    m, k = x.shape
    n_all = w.shape[-2] if w_t else w.shape[-1]
    ncols = n_all if ncols is None else ncols
    tn = min(tn, ncols)
    tm = _row_tile(m, tm)
    assert ncols % tn == 0 and col0 % tn == 0 and m % tm == 0 and w.shape[-1 if w_t else -2] == k
    valid_cols = None
    if col0 + ncols > n_all:
        assert ncols == tn
        valid_cols = n_all - col0
    jb = col0 // tn
    if w_t:
        wspec = pl.BlockSpec((None,) * len(lead) + (tn, k), lambda j, i: lead + (j + jb, 0))
    else:
        wspec = pl.BlockSpec((None,) * len(lead) + (k, tn), lambda j, i: lead + (0, j + jb))
    grid = (ncols // tn, m // tm)
    in_specs = [pl.BlockSpec((tm, k), lambda j, i: (i, 0)), wspec]
    out_specs = [pl.BlockSpec((tm, tn), lambda j, i: (i, j))]
    out_shape = [jax.ShapeDtypeStruct((m, ncols), out_dtype)]
    args = [x, w]
    if side is not None:
        side_in, side_out, side_shape = _side_cast_specs(side, *grid)
        in_specs.append(side_in)
        out_specs.append(side_out)
        out_shape.append(side_shape)
        args.append(side[0])
    res = pl.pallas_call(
        functools.partial(_mm_body, act=act, valid_cols=valid_cols, w_t=w_t),
        grid=grid,
        in_specs=in_specs,
        out_specs=out_specs,
        out_shape=out_shape,
        scratch_shapes=[pltpu.VMEM((k, tn), BF16)],
        compiler_params=_cparams(2),
        name=name,
    )(*args)
    return res[0] if side is None else res


def _swiglu_up_body(x_ref, wg_ref, wu_ref, wd_ref, o_ref, wd_bf_ref, wg_bf, wu_bf):
    @pl.when(pl.program_id(1) == 0)
    def _():
        wg_bf[...] = wg_ref[...].astype(BF16)
        wu_bf[...] = wu_ref[...].astype(BF16)

    x = x_ref[...]
    g = jnp.dot(x, wg_bf[...], preferred_element_type=F32)
    u = jnp.dot(x, wu_bf[...], preferred_element_type=F32)
    o_ref[...] = (g * _sigmoid(g) * u).astype(o_ref.dtype)
    wd_bf_ref[...] = wd_ref[...].astype(BF16)


def _swiglu_up(x, w_gate, w_up, w_down, lead, tn=512, tm=1856):
    m, k = x.shape
    n = w_gate.shape[-1]
    tm = _row_tile(m, tm)
    wspec = pl.BlockSpec((None,) * len(lead) + (k, tn), lambda j, i: lead + (0, j))
    grid = (n // tn, m // tm)
    side_in, side_out, side_shape = _side_cast_specs((w_down, lead), *grid)
    return pl.pallas_call(
        _swiglu_up_body,
        grid=grid,
        in_specs=[pl.BlockSpec((tm, k), lambda j, i: (i, 0)), wspec, wspec, side_in],
        out_specs=[pl.BlockSpec((tm, tn), lambda j, i: (i, j)), side_out],
        out_shape=[jax.ShapeDtypeStruct((m, n), BF16), side_shape],
        scratch_shapes=[pltpu.VMEM((k, tn), BF16), pltpu.VMEM((k, tn), BF16)],
        compiler_params=_cparams(2),
        name="swiglu_up",
    )(x, w_gate, w_up, w_down)


def _rms(x, g):
    return x * lax.rsqrt(jnp.mean(x * x, axis=-1, keepdims=True) + EPS) * g


def _norm_body(h_ref, g_ref, u_ref, *, gi):
    u_ref[...] = _rms(h_ref[...], g_ref[gi:gi + 1, :]).astype(u_ref.dtype)


def _norm(h, norm_g, layer, gi, out_dtype=BF16, tm=464):
    m, d = h.shape
    tm = _row_tile(m, tm)
    return pl.pallas_call(
        functools.partial(_norm_body, gi=gi),
        grid=(m // tm,),
        in_specs=[pl.BlockSpec((tm, d), lambda i: (i, 0)),
                  pl.BlockSpec((None, 6, d), lambda i: (layer, 0, 0))],
        out_specs=pl.BlockSpec((tm, d), lambda i: (i, 0)),
        out_shape=jax.ShapeDtypeStruct((m, d), out_dtype),
        compiler_params=_cparams(1),
        name="norm",
    )(h, norm_g)


def _down_resnorm_body(x_ref, w_ref, h_ref, ga_ref, gb_ref, hn_ref, *rest, scale, ia, ib):
    u_ref, y_scr = rest if len(rest) == 2 else (None, rest[0])
    j = pl.program_id(1)
    y_scr[j] = jnp.dot(x_ref[...], w_ref[...], preferred_element_type=F32)

    @pl.when(j == pl.num_programs(1) - 1)
    def _():
        y = jnp.concatenate([y_scr[c] for c in range(y_scr.shape[0])], axis=1)
        hn = h_ref[...] + scale * _rms(y, ga_ref[ia:ia + 1, :])
        hn_ref[...] = hn
        if u_ref is not None:
            u_ref[...] = _rms(hn, gb_ref[ib:ib + 1, :]).astype(u_ref.dtype)


def _down_resnorm(x, w_bf, h, norm_g, scale, la, ia, lb=None, ib=None, u_dtype=BF16, tn=512, tm=464):
    m, k = x.shape
    d = w_bf.shape[-1]
    tm = _row_tile(m, tm)
    row = pl.BlockSpec((tm, d), lambda i, j: (i, 0))
    with_u = lb is not None
    if not with_u:
        lb = la
    out_shape = [jax.ShapeDtypeStruct((m, d), F32)]
    out_specs = [row]
    if with_u:
        out_shape.append(jax.ShapeDtypeStruct((m, d), u_dtype))
        out_specs.append(row)
    res = pl.pallas_call(
        functools.partial(_down_resnorm_body, scale=scale, ia=ia, ib=ib),
        grid=(m // tm, d // tn),
        in_specs=[pl.BlockSpec((tm, k), lambda i, j: (i, 0)),
                  pl.BlockSpec((k, tn), lambda i, j: (0, j)),
                  row,
                  pl.BlockSpec((None, 6, d), lambda i, j: (la, 0, 0)),
                  pl.BlockSpec((None, 6, d), lambda i, j: (lb, 0, 0))],
        out_specs=out_specs,
        out_shape=out_shape,
        scratch_shapes=[pltpu.VMEM((d // tn, tm, tn), F32)],
        compiler_params=_cparams(2),
        name="down_resnorm",
    )(x, w_bf, h, norm_g, norm_g)
    return res if with_u else (res[0], None)


SSD_CHUNK = 128
SSD_GPS = 2


def _conv_taps(cur, shifted, w, b):
    out = b + cur * w[M_CONV - 1:M_CONV]
    for k in range(1, M_CONV):
        out = out + shifted(k) * w[M_CONV - 1 - k:M_CONV - k]
    return _silu(out)


def _conv_silu_seq(cur, prv, w, b):
    row = _iota((SUBLANES, cur.shape[1]), 0)

    def shifted(k):
        sh = pltpu.roll(cur, k, 0)
        top = jnp.where(row < k, pltpu.roll(prv, k, 0), sh[:SUBLANES])
        return jnp.concatenate([top, sh[SUBLANES:]], axis=0)

    return _conv_taps(cur, shifted, w, b)


def _conv_silu_batch(cur, stt, w, b):
    ntok = cur.shape[0]
    tpos = _iota((ntok, 1), 0) & (SUBLANES - 1)

    def shifted(k):
        return jnp.where(tpos < k, pltpu.roll(stt, (k - SUBLANES) % ntok, 0), pltpu.roll(cur, k, 0))

    return _conv_taps(cur, shifted, w, b)


def _expand_heads(rows):
    head_of_lane = _iota((M_R, M_GW), 1) >> _log2(M_HEADDIM)
    sel = (head_of_lane == _iota((M_R, M_GW), 0)).astype(BF16)
    return _xdot(rows, sel, terms=2)


def _ssd_common(x, bm, cm, dtr, hp, causal, same, nv, extra_rows):
    ntok = x.shape[0]
    bias, a = hp[0:1], -jnp.exp(hp[1:2])
    dt = _softplus(dtr + bias)
    if nv < ntok and same is None:
        dt = jnp.where(_iota((ntok, 1), 0) < nv, dt, 0.0)
    da = dt * a
    cum = _xdot_left(causal.astype(BF16), da)
    cum_tot = cum[ntok - 1:ntok] if same is None else _xdot_left(same.astype(BF16), da)
    cbm = _dot_nt(cm, bm)
    eye8 = (_iota((M_R, M_R), 0) == _iota((M_R, M_R), 1)).astype(BF16)
    dt_t = _xdot_nt(eye8, dt)
    even = (_iota((1, M_GW), 1) & M_HEADDIM) == 0
    x_e = jnp.where(even, x, 0.0).astype(BF16)
    x_o = jnp.where(even, 0.0, x).astype(BF16)
    yield
    cum2 = cum * LOG2_E
    cum2_t = _xdot_nt(eye8, cum2)
    tail = jnp.exp(cum_tot - cum) * dt
    ex = _expand_heads(jnp.concatenate([jnp.exp(cum), tail, extra_rows(cum_tot)], axis=0))
    yield
    parts = []
    for q in range(M_R // 2):
        acc = None
        for r, x_m in ((2 * q, x_e), (2 * q + 1, x_o)):
            dec = jnp.exp2(jnp.where(causal, cum2[:, r:r + 1] - cum2_t[r:r + 1, :], -jnp.inf))
            sc = (cbm * dec * dt_t[r:r + 1, :]).astype(BF16)
            yy = jnp.dot(sc, x_m[:, q * LANES:(q + 1) * LANES], preferred_element_type=F32)
            acc = yy if acc is None else acc + yy
        parts.append(acc)
    return jnp.concatenate(parts, axis=1), ex, cum_tot


def _ssd_finish(y, x, zc, d_row, ng):
    yv = (y + d_row * x) * _silu(zc)
    return yv * lax.rsqrt(jnp.mean(yv * yv, axis=-1, keepdims=True) + M_NORM_EPS) * ng


def _ssd_prompt_chunk(nv, zc, xr, br, cr, dtr, prev, st, cw, cb, hp, ng):
    n = SSD_CHUNK
    new_prev = (xr[nv - SUBLANES:], br[nv - SUBLANES:], cr[nv - SUBLANES:])
    zc, xr, br, cr, dtr = (_pad_rows(v, n) for v in (zc, xr, br, cr, dtr))
    x = _conv_silu_seq(xr, prev[0], cw[0], cb[0])
    bm = _conv_silu_seq(br, prev[1], cw[1], cb[1])
    cm = _conv_silu_seq(cr, prev[2], cw[2], cb[2])
    causal = _iota((n, n), 0) >= _iota((n, n), 1)
    row = _iota((SUBLANES, M_R), 0)

    def extra_rows(cum_tot):
        return jnp.where(row == 0, jnp.exp(cum_tot), jnp.where(row == 1, hp[2:3], 0.0))

    y_state = _dot(cm, st)
    y, ex, _ = yield from _ssd_common(x, bm, cm, dtr, hp, causal, None, nv, extra_rows)
    upd = _dot_tn(bm, ex[n:2 * n] * x)
    yield
    y = y + y_state * ex[:n]
    st = st * ex[2 * n:2 * n + 1] + upd
    return _ssd_finish(y, x, zc, ex[2 * n + 1:2 * n + 2], ng)[:nv], new_prev, st


def _ssd_sample_chunk(nb, zc, xr, br, cr, dtr, stt, states, cw, cb, hp, ng):
    n = SUBLANES
    ntok = nb * n
    x = _conv_silu_batch(xr, stt[0], cw[0], cb[0])
    bm = _conv_silu_batch(br, stt[1], cw[1], cb[1])
    cm = _conv_silu_batch(cr, stt[2], cw[2], cb[2])
    same, causal = _seq_masks(ntok, n)
    row = _iota((SUBLANES, M_R), 0)
    ys = [_dot_nt(_pad_rows(cm[b * n:(b + 1) * n], BF16_ROWS), states[b])[:n] for b in range(nb)]
    y, ex, cum_tot = yield from _ssd_common(x, bm, cm, dtr, hp, causal, same, n,
                                            lambda _: jnp.where(row == 0, hp[2:3], 0.0))
    xt = ex[ntok:2 * ntok] * x
    upds = [_dot_tn(_pad_rows(xt[b * n:(b + 1) * n], BF16_ROWS), _pad_rows(bm[b * n:(b + 1) * n], BF16_ROWS))
            for b in range(nb)]
    yield
    etot = jnp.exp(cum_tot)
    new_states = []
    for b in range(nb):
        decay = jnp.concatenate([jnp.broadcast_to(etot[b * n:b * n + 1, r:r + 1], (M_HEADDIM, M_D_STATE))
                                 for r in range(M_R)], axis=0)
        new_states.append(states[b] * decay + upds[b])
    y = y + jnp.concatenate(ys, axis=0) * ex[:ntok]
    return _ssd_finish(y, x, zc, ex[2 * ntok:2 * ntok + 1], ng), new_states


def _ssd_body(*refs, t, nb, has_state):
    if has_state:
        (z_ref, x_ref, b_ref, c_ref, dt_ref, cwx_ref, cwb_ref, cwc_ref, cbx_ref, cbb_ref, cbc_ref,
         hp_ref, ng_ref, csx_ref, csb_ref, csc_ref, sin_ref, y_ref, sout_ref) = refs
    else:
        (z_ref, x_ref, b_ref, c_ref, dt_ref, cwx_ref, cwb_ref, cwc_ref, cbx_ref, cbb_ref, cbc_ref,
         hp_ref, ng_ref, y_ref, sout_ref) = refs

    def consts(gi):
        wide = slice(gi * M_GW, (gi + 1) * M_GW)
        nar = slice(gi * M_D_STATE, (gi + 1) * M_D_STATE)
        cw = (cwx_ref[:, wide], cwb_ref[:, nar], cwc_ref[:, nar])
        cb = (cbx_ref[:, wide], cbb_ref[:, nar], cbc_ref[:, nar])
        return wide, nar, cw, cb, hp_ref[gi], ng_ref[:, wide]

    if has_state:
        ntok = nb * t
        gens = []
        for gi in range(SSD_GPS):
            wide, nar, cw, cb, hp, ng = consts(gi)
            stt = (csx_ref[:, :, wide].reshape(ntok, M_GW), csb_ref[:, :, nar].reshape(ntok, M_D_STATE),
                   csc_ref[:, :, nar].reshape(ntok, M_D_STATE))
            states = [sin_ref[b, gi * M_R:(gi + 1) * M_R].reshape(M_GW, M_D_STATE) for b in range(nb)]
            gens.append(_ssd_sample_chunk(nb, z_ref[:, wide], x_ref[:, wide], b_ref[:, nar], c_ref[:, nar],
                                          dt_ref[gi], stt, states, cw, cb, hp, ng))
        for gi, (out, states) in enumerate(_lockstep(gens)):
            y_ref[:, gi * M_GW:(gi + 1) * M_GW] = out.astype(y_ref.dtype)
            for b in range(nb):
                sout_ref[b, gi * M_R:(gi + 1) * M_R] = states[b].reshape(M_R, M_HEADDIM, M_D_STATE)
        return

    def run(start, nv, carry):
        rows = pl.ds(pl.multiple_of(start, SUBLANES), nv)
        gens = []
        for gi in range(SSD_GPS):
            wide, nar, cw, cb, hp, ng = consts(gi)
            prev, st = carry[gi]
            gens.append(_ssd_prompt_chunk(nv, z_ref[rows, wide], x_ref[rows, wide], b_ref[rows, nar],
                                          c_ref[rows, nar], dt_ref[gi, rows, :], prev, st, cw, cb, hp, ng))
        new = []
        for gi, (out, prev, st) in enumerate(_lockstep(gens)):
            y_ref[rows, gi * M_GW:(gi + 1) * M_GW] = out.astype(y_ref.dtype)
            new.append((prev, st))
        return tuple(new)

    zero_prev = (jnp.zeros((SUBLANES, M_GW), F32), jnp.zeros((SUBLANES, M_D_STATE), F32),
                 jnp.zeros((SUBLANES, M_D_STATE), F32))
    carry = tuple((zero_prev, jnp.zeros((M_D_STATE, M_GW), F32)) for _ in range(SSD_GPS))
    n_full = t // SSD_CHUNK
    carry = lax.fori_loop(0, n_full, lambda c, cr: run(c * SSD_CHUNK, SSD_CHUNK, cr), carry)
    if t % SSD_CHUNK:
        carry = run(n_full * SSD_CHUNK, t % SSD_CHUNK, carry)
    for gi in range(SSD_GPS):
        sout_ref[0, gi * M_R:(gi + 1) * M_R] = carry[gi][1].T.reshape(M_R, M_HEADDIM, M_D_STATE)


def _ssd(j, z, xbc, dtg, m_conv_w, conv_b3, hp, m_norm_g3, group, conv_state=None, ssm_state=None,
         y_into=None, state_into=None):
    nseq, t, row0, nb = group
    rb = nb * t
    r0 = row0 // rb
    assert row0 % rb == 0 and nseq % nb == 0 and M_GROUPS % SSD_GPS == 0
    has_state = ssm_state is not None
    gw, gn, gr = SSD_GPS * M_GW, SSD_GPS * M_D_STATE, SSD_GPS * M_R
    b0 = M_D_INNER // gn
    c0 = b0 + M_GROUPS // SSD_GPS
    one = _single(not has_state)
    in_specs = [
        pl.BlockSpec((rb, gw), lambda s, g: (s + r0, g)),
        pl.BlockSpec((rb, gw), lambda s, g: (s + r0, g)),
        pl.BlockSpec((rb, gn), lambda s, g: (s + r0, b0 + g), **one),
        pl.BlockSpec((rb, gn), lambda s, g: (s + r0, c0 + g), **one),
        pl.BlockSpec((SSD_GPS, rb, M_R), lambda s, g: (g, s + r0, 0), **one),
        pl.BlockSpec((None, M_CONV, gw), lambda s, g: (j, 0, g)),
        pl.BlockSpec((None, M_CONV, gn), lambda s, g: (j, 0, b0 + g)),
        pl.BlockSpec((None, M_CONV, gn), lambda s, g: (j, 0, c0 + g)),
        pl.BlockSpec((None, 1, gw), lambda s, g: (j, 0, g)),
        pl.BlockSpec((None, 1, gn), lambda s, g: (j, 0, b0 + g)),
        pl.BlockSpec((None, 1, gn), lambda s, g: (j, 0, c0 + g)),
        pl.BlockSpec((None, SSD_GPS, 3, M_R), lambda s, g: (j, g, 0, 0)),
        pl.BlockSpec((None, 1, gw), lambda s, g: (j, 0, g)),
    ]
    args = [z, xbc, xbc, xbc, dtg, m_conv_w, m_conv_w, m_conv_w, conv_b3, conv_b3, conv_b3, hp, m_norm_g3]
    if has_state:
        in_specs += [
            pl.BlockSpec((None, nb, SUBLANES, gw), lambda s, g: (j, s, 0, g)),
            pl.BlockSpec((None, nb, SUBLANES, gn), lambda s, g: (j, s, 0, b0 + g)),
            pl.BlockSpec((None, nb, SUBLANES, gn), lambda s, g: (j, s, 0, c0 + g)),
            pl.BlockSpec((None, nb, gr, M_HEADDIM, M_D_STATE), lambda s, g: (j, s, g, 0, 0)),
        ]
        args += [conv_state, conv_state, conv_state, ssm_state]
    body, aliases = _write_into(functools.partial(_ssd_body, t=t, nb=nb, has_state=has_state),
                                in_specs, args, (y_into, state_into))
    return pl.pallas_call(
        body,
        grid=(nseq // nb, M_GROUPS // SSD_GPS),
        in_specs=in_specs,
        out_specs=[pl.BlockSpec((rb, gw), lambda s, g: (s + r0, g)),
                   pl.BlockSpec((None, nb, gr, M_HEADDIM, M_D_STATE), lambda s, g: (j, s, g, 0, 0))],
        out_shape=[jax.ShapeDtypeStruct((z.shape[0], M_D_INNER), BF16),
                   jax.ShapeDtypeStruct((hp.shape[0], nseq, M_HEADS, M_HEADDIM, M_D_STATE), F32)],
        input_output_aliases=aliases,
        compiler_params=_cparams(2),
        name="ssd_sample" if has_state else "ssd_prompt",
    )(*args)


def _mamba_layer(j, u, groups, m_w_in, m_conv_w, m_conv_b, m_dt_bias, m_a_log, m_d, m_norm_g, m_w_out,
                 state_conv_pad, state_ssm, ssm_into=(None, None)):
    m = u.shape[0]
    gp, gs = groups
    w_in_t = jnp.swapaxes(m_w_in, 1, 2)
    z, w_out_bf = _mm(u, w_in_t, (j,), 0, M_D_INNER, w_t=True, side=(m_w_out, (j,)), name="ssd_in_z")
    xbc = _mm(u, w_in_t, (j,), M_D_INNER, M_CONV_DIM, w_t=True, name="ssd_in_xbc")
    dt = _mm(u, w_in_t, (j,), M_D_INNER + M_CONV_DIM, LANES, tn=LANES, w_t=True, name="ssd_in_dt")[:, :M_HEADS]
    dtg = dt.reshape(m, M_GROUPS, M_R).transpose(1, 0, 2)
    hp = jnp.stack([m_dt_bias, m_a_log, m_d], axis=1).reshape(-1, 3, M_GROUPS, M_R).transpose(0, 2, 1, 3)
    conv_b3 = m_conv_b[:, None, :]
    ng3 = m_norm_g[:, None, :]
    y, sp = _ssd(j, z, xbc, dtg, m_conv_w, conv_b3, hp, ng3, gp, state_into=ssm_into[0])
    y, ss = _ssd(j, z, xbc, dtg, m_conv_w, conv_b3, hp, ng3, gs, state_conv_pad, state_ssm,
                 y_into=y, state_into=ssm_into[1])
    keep = M_CONV - 1
    xp = jnp.stack([xbc[(b + 1) * gp[1] - keep:(b + 1) * gp[1]] for b in range(gp[0])])
    xs = xbc[gp[0] * gp[1]:].reshape(gs[0], gs[1], M_CONV_DIM)[:, gs[1] - keep:]
    return (y, w_out_bf), (sp, xp), (ss, xs)


R_TOK = 64
R_PAIRS = 4


def _rwkv_chunk(nb, nv, toks, sbs, prm):
    n = R_TOK // nb
    ntok = R_TOK
    r, kr, v, wl, al, g = (_pad_rows(t_, ntok) for t_ in toks)
    w0, a0, k_k, k_a, r_k, ln_g, ln_b = prm
    m0 = _iota((1, LANES), 1) < R_HEADSIZE
    ones_blk = ((_iota((LANES, LANES), 0) < R_HEADSIZE) == (_iota((LANES, LANES), 1) < R_HEADSIZE)).astype(BF16)

    w = -_softplus(-(w0 + wl)) - 0.5
    lw = -jnp.exp(w)
    a = _sigmoid(a0 + al)
    kk = kr * k_k
    k = kr * (1.0 + (a - 1.0) * k_a)
    if nv < n:
        valid = _iota((ntok, 1), 0) < nv
        lw = jnp.where(valid, lw, 0.0)
        kk = jnp.where(valid, kk, 0.0)
        k = jnp.where(valid, k, 0.0)
        v = jnp.where(valid, v, 0.0)
    same, causal = _seq_masks(ntok, n)
    head_sums = _xdot(jnp.concatenate([kk * kk, r * k * r_k], axis=0), ones_blk, terms=2)
    lam = _xdot_left(causal.astype(BF16), lw)
    lam_tot = lam[ntok - 1:ntok] if nb == 1 else _xdot_left(same.astype(BF16), lw)
    yield
    bonus = head_sums[ntok:] * v
    kk = kk / jnp.maximum(jnp.sqrt(head_sums[:ntok]), 1e-12)
    beta = kk * a
    e_in = jnp.exp(lam)
    e_out = jnp.exp(-lam)
    e_end = jnp.exp(lam_tot - lam)

    def stack(x):
        x3 = x.reshape(nb, n, LANES)
        s = jnp.concatenate([jnp.where(m0, x3, 0.0), jnp.where(m0, 0.0, x3)], axis=1)
        return s.reshape(2 * ntok, LANES).astype(BF16)

    am, rm = stack(kk * jnp.exp(lam - lw)), stack(r * e_in)
    khm, bhm = stack(k * e_out), stack(beta * e_out)
    vst = stack(v)
    n2 = 2 * ntok
    rr, cc = _iota((n2, n2), 0), _iota((n2, n2), 1)
    same_blk = (rr >> _log2(n)) == (cc >> _log2(n))
    strict = same_blk & ((rr & (n - 1)) > (cc & (n - 1)))
    incl = same_blk & ((rr & (n - 1)) >= (cc & (n - 1)))
    qm = jnp.concatenate([am, rm], axis=0)
    pmat = _dot_nt(qm, jnp.concatenate([khm, bhm], axis=0))
    w2 = 2 * n
    if nb == 1:
        qs = [_dot_nt(qm, sbs[0])]
    else:
        qs = [_dot_nt(jnp.concatenate([am[b * w2:(b + 1) * w2], rm[b * w2:(b + 1) * w2]], axis=0), sbs[b])
              for b in range(nb)]
    ktm, btm = stack(k * e_end), stack(beta * e_end)
    yield
    a_ak = jnp.where(strict, pmat[:n2, :n2], 0.0)
    nmat = jnp.where(strict, -pmat[:n2, n2:], 0.0)
    a_rk = jnp.where(incl, pmat[n2:, :n2], 0.0)
    a_rb = jnp.where(incl, pmat[n2:, n2:], 0.0)
    rhs_s = qs[0][:w2] if nb == 1 else jnp.concatenate([q[:w2] for q in qs], axis=0)
    ys = qs[0][w2:] if nb == 1 else jnp.concatenate([q[w2:] for q in qs], axis=0)

    tm = (rr == cc).astype(F32) + nmat
    npow = _dot(nmat, nmat)
    av = _dot(jnp.concatenate([a_ak, a_rk], axis=0), vst)
    yield
    rhs = rhs_s + av[:n2]
    ys = ys + av[n2:]
    for i in range(_log2(n) - 1):
        if i + 2 < _log2(n):
            both = _dot(jnp.concatenate([tm, npow], axis=0), npow)
            prod, npow = both[:n2], both[n2:]
        else:
            prod = _dot(tm, npow)
        yield
        tm = tm + prod
    u = _dot(tm, rhs)
    yield
    yst = (ys - _dot(a_rb, u)).reshape(nb, w2, LANES)
    nu = (-u).astype(BF16)
    new_sbs = []
    for b in range(nb):
        rows = slice(b * w2, (b + 1) * w2)
        lt = lam_tot if nb == 1 else lam_tot[b * n:b * n + 1]
        new_sbs.append(sbs[b] * jnp.exp(lt)
                       + _dot_tn(jnp.concatenate([vst[rows], nu[rows]], axis=0),
                                 jnp.concatenate([ktm[rows], btm[rows]], axis=0)))
    yield
    y = (yst[:, :n] + yst[:, n:]).reshape(ntok, LANES)
    inv = 1.0 / R_HEADSIZE
    mean = _xdot(y, ones_blk, terms=2) * inv
    yield
    yc = y - mean
    var = _xdot(yc * yc, ones_blk, terms=2) * inv
    yield
    yn = yc * lax.rsqrt(var + R_LN_EPS) * ln_g + ln_b
    return ((yn + bonus) * g)[:nb * nv], new_sbs


def _rwkv_body(*refs, t, nb, has_state):
    if has_state:
        (r_ref, k_ref, v_ref, wl_ref, al_ref, g_ref, prm_ref, sin_ref, y_ref, sout_ref) = refs
    else:
        (r_ref, k_ref, v_ref, wl_ref, al_ref, g_ref, prm_ref, y_ref, sout_ref) = refs
    tok_refs = (r_ref, k_ref, v_ref, wl_ref, al_ref, g_ref)
    zeros = jnp.zeros((R_HEADSIZE, R_HEADSIZE), F32)

    def lanes(p):
        return slice(p * LANES, (p + 1) * LANES)

    def run(rows, nv, states):
        res = _lockstep([
            _rwkv_chunk(nb, nv, tuple(ref[rows, lanes(p)] for ref in tok_refs), states[p],
                        tuple(prm_ref[i:i + 1, lanes(p)] for i in range(7)))
            for p in range(R_PAIRS)])
        for p, (out, _) in enumerate(res):
            y_ref[rows, lanes(p)] = out.astype(y_ref.dtype)
        return tuple(tuple(sbs) for _, sbs in res)

    if has_state:
        states = tuple(
            tuple(jnp.concatenate([jnp.concatenate([sin_ref[b, 2 * p], zeros], axis=1),
                                   jnp.concatenate([zeros, sin_ref[b, 2 * p + 1]], axis=1)], axis=0)
                  for b in range(nb))
            for p in range(R_PAIRS))
    else:
        states = tuple(tuple(jnp.zeros((LANES, LANES), F32) for _ in range(nb)) for _ in range(R_PAIRS))

    if nb == 1:
        n_full = t // R_TOK
        states = lax.fori_loop(
            0, n_full, lambda c, st: run(pl.ds(pl.multiple_of(c * R_TOK, R_TOK), R_TOK), R_TOK, st), states)
        if t % R_TOK:
            states = run(pl.ds(n_full * R_TOK, t % R_TOK), t % R_TOK, states)
    else:
        assert nb * t == R_TOK
        states = run(slice(None), t, states)
    for p in range(R_PAIRS):
        for b in range(nb):
            sout_ref[b, 2 * p] = states[p][b][:R_HEADSIZE, :R_HEADSIZE]
            sout_ref[b, 2 * p + 1] = states[p][b][R_HEADSIZE:, R_HEADSIZE:]


def _rwkv(r, k, v, wl, al, g, prm, group, wkv_state=None, y_into=None):
    nseq, t, row0, nb = group
    rb = nb * t
    r0 = row0 // rb
    assert row0 % rb == 0 and nseq % nb == 0
    has_state = wkv_state is not None
    gl = R_PAIRS * LANES
    tok = pl.BlockSpec((rb, gl), lambda s, hp_: (s + r0, hp_))
    tok1 = pl.BlockSpec((rb, gl), lambda s, hp_: (s + r0, hp_), **_single(not has_state))
    in_specs = [tok] * 5 + [tok1] + [pl.BlockSpec((SUBLANES, gl), lambda s, hp_: (0, hp_))]
    args = [r, k, v, wl, al, g, prm]
    if has_state:
        in_specs.append(pl.BlockSpec((None, nb, 2 * R_PAIRS, R_HEADSIZE, R_HEADSIZE),
                                     lambda s, hp_: (0, s, hp_, 0, 0)))
        args.append(wkv_state)
    body, aliases = _write_into(functools.partial(_rwkv_body, t=t, nb=nb, has_state=has_state),
                                in_specs, args, (y_into,))
    return pl.pallas_call(
        body,
        grid=(nseq // nb, R_HEADS // (2 * R_PAIRS)),
        in_specs=in_specs,
        out_specs=[pl.BlockSpec((rb, gl), lambda s, hp_: (s + r0, hp_)),
                   pl.BlockSpec((nb, 2 * R_PAIRS, R_HEADSIZE, R_HEADSIZE), lambda s, hp_: (s, hp_, 0, 0))],
        out_shape=[jax.ShapeDtypeStruct((r.shape[0], D_MODEL), BF16),
                   jax.ShapeDtypeStruct((nseq, R_HEADS, R_HEADSIZE, R_HEADSIZE), F32)],
        input_output_aliases=aliases,
        compiler_params=_cparams(2),
        name="rwkv_sample" if has_state else "rwkv_prompt",
    )(*args)


MIX_COLS = 512


def _mix_body(*refs, t, nb, has_state):
    if has_state:
        u_ref, mu_ref, sh_ref = refs[:3]
    else:
        u_ref, mu_ref = refs[:2]
    o_refs = refs[-6:]
    u = u_ref[...]
    rows = nb * t
    tpos = _iota((rows, 1), 0) & (t - 1) if nb > 1 else _iota((rows, 1), 0)
    prev = pltpu.roll(u, 1, 0)
    if has_state:
        pick = (_iota((rows, BF16_ROWS), 0) >> _log2(t)) == _iota((rows, BF16_ROWS), 1)
        first = _xdot_left(pick.astype(BF16), _pad_rows(sh_ref[...], BF16_ROWS))
    else:
        first = 0.0
    dx = jnp.where(tpos == 0, first, prev) - u
    for i, o_ref in enumerate(o_refs):
        o_ref[...] = (u + dx * mu_ref[i:i + 1, :]).astype(o_ref.dtype)


def _rwkv_mix(u, r_mu, group, shift_state=None, into=None):
    nseq, t, row0, nb = group
    m, d = u.shape
    rb = nb * t
    r0 = row0 // rb
    has_state = shift_state is not None
    assert row0 % rb == 0 and nseq % nb == 0 and (nb == 1 or nb <= BF16_ROWS)
    blk = pl.BlockSpec((rb, MIX_COLS), lambda s, c: (s + r0, c))
    in_specs = [blk, pl.BlockSpec((None, 6, MIX_COLS), lambda s, c: (0, 0, c))]
    args = [u, r_mu]
    if has_state:
        in_specs.append(pl.BlockSpec((None, nb, MIX_COLS), lambda s, c: (0, s, c)))
        args.append(shift_state)
    body, aliases = _write_into(functools.partial(_mix_body, t=t, nb=nb, has_state=has_state),
                                in_specs, args, into or (None,) * 6)
    return pl.pallas_call(
        body,
        grid=(nseq // nb, d // MIX_COLS),
        in_specs=in_specs,
        out_specs=[blk] * 6,
        out_shape=[jax.ShapeDtypeStruct((m, d), BF16)] * 6,
        input_output_aliases=aliases,
        compiler_params=_cparams(2),
        name="rwkv_mix_sample" if has_state else "rwkv_mix_prompt",
    )(*args)


def _rwkv_layer(u, groups, state_wkv, state_shift, r_mu, r_w_r, r_w_k, r_w_v, r_w0, r_w1, r_w2, r_a0, r_a1,
                r_a2, r_g1, r_g2, r_k_k, r_k_a, r_r_k, r_ln_g, r_ln_b, r_w_o):
    gp, gs = groups
    mixed = _rwkv_mix(u, r_mu, gp)
    xr, xw, xk, xv, xa, xg = _rwkv_mix(u, r_mu, gs, state_shift, into=mixed)
    r, w_o_bf = _mm(xr, r_w_r, (0,), side=(r_w_o, (0,)), name="rwkv_r")
    k = _mm(xk, r_w_k, (0,), name="rwkv_k")
    v = _mm(xv, r_w_v, (0,), name="rwkv_v")
    wl = _mm(_mm(xw, r_w1, (0,), out_dtype=BF16, act="tanh", name="rwkv_w1"), r_w2, (0,), name="rwkv_w2")
    al = _mm(_mm(xa, r_a1, (0,), out_dtype=BF16, name="rwkv_a1"), r_a2, (0,), name="rwkv_a2")
    g = _mm(_mm(xg, r_g1, (0,), out_dtype=BF16, act="sigmoid", name="rwkv_g1"), r_g2, (0,), name="rwkv_g2")
    prm = jnp.concatenate([r_w0, r_a0, r_k_k, r_k_a, r_r_k.reshape(1, D_MODEL), r_ln_g, r_ln_b,
                           jnp.zeros((1, D_MODEL), F32)], axis=0)
    y, sp = _rwkv(r, k, v, wl, al, g, prm, gp)
    y, ss = _rwkv(r, k, v, wl, al, g, prm, gs, state_wkv, y_into=y)
    last_p = jnp.stack([u[(b + 1) * gp[1] - 1] for b in range(gp[0])])
    last_s = u[gp[0] * gp[1]:].reshape(gs[0], gs[1], D_MODEL)[:, -1]
    return (y, w_o_bf), (sp, last_p), (ss, last_s)


H_CHUNK = 128
H_SUB = 16
H_HPS = 4


def _gla_diag(q3, g3, k3, v3):
    sub = q3.shape[1]
    pieces = [slice(p, p + SUBLANES) for p in range(0, sub, SUBLANES)]
    tpos = _iota((1, SUBLANES, 1), 1)
    g3 = g3 * LOG2_E
    ys = [jnp.zeros(v3[:, p].shape, F32) for p in pieces]
    for s in range(sub):
        terms = []
        for i, p in enumerate(pieces):
            if p.stop <= s:
                continue
            arg = g3[:, p] - g3[:, s:s + 1, :]
            if p.start < s:
                arg = jnp.where(tpos + p.start >= s, arg, -jnp.inf)
            terms.append((i, jnp.sum(q3[:, p] * k3[:, s:s + 1, :] * jnp.exp2(arg), axis=-1, keepdims=True)))
        yield
        for i, a_s in terms:
            ys[i] = ys[i] + a_s * v3[:, s:s + 1, :]
    return ys[0] if len(ys) == 1 else jnp.concatenate(ys, axis=1)


def _gla_prompt_chunk(nv, q, logf, k, v, st):
    n = -(-nv // H_SUB) * H_SUB
    q, logf, k, v = (_pad_rows(t_, n) for t_ in (q, logf, k, v))
    causal = _iota((n, n), 0) >= _iota((n, n), 1)
    gcum = _xdot_left(causal.astype(BF16), logf)
    yield
    y = _dot_nt(q * jnp.exp(gcum), st)
    g_n = gcum[n - 1:n]
    upd = _dot_tn(v, k * jnp.exp(g_n - gcum))
    atts = []
    for i in range(1, n // H_SUB):
        lo = i * H_SUB
        gs = gcum[lo - 1:lo]
        atts.append(_dot_nt(q[lo:lo + H_SUB] * jnp.exp(gcum[lo:lo + H_SUB] - gs),
                            k[:lo] * jnp.exp(gs - gcum[:lo])))
    yield
    st = st * jnp.exp(g_n) + upd
    offs = [jnp.zeros((H_SUB, LANES), F32)] + [_dot(att, v[:(i + 1) * H_SUB]) for i, att in enumerate(atts)]
    shape3 = (n // H_SUB, H_SUB, LANES)
    y3 = yield from _gla_diag(q.reshape(shape3), gcum.reshape(shape3), k.reshape(shape3), v.reshape(shape3))
    off = offs[0] if len(offs) == 1 else jnp.concatenate(offs, axis=0)
    return (y + off + y3.reshape(n, LANES))[:nv], st


def _gla_sample_chunk(nb, q, logf, k, v, sts):
    n = SUBLANES
    ntok = nb * n
    _, causal = _seq_masks(ntok, n)
    gcum = _xdot_left(causal.astype(BF16), logf)
    yield
    shape3 = (nb, n, LANES)
    g3 = gcum.reshape(shape3)
    g_tot = g3[:, n - 1:n, :]
    qe = q * jnp.exp(gcum)
    kd = (k.reshape(shape3) * jnp.exp(g_tot - g3)).reshape(ntok, LANES)
    ys = [_dot_nt(_pad_rows(qe[b * n:(b + 1) * n], BF16_ROWS), sts[b])[:n] for b in range(nb)]
    upds = [_dot_tn(_pad_rows(v[b * n:(b + 1) * n], BF16_ROWS), _pad_rows(kd[b * n:(b + 1) * n], BF16_ROWS))
            for b in range(nb)]
    y3 = yield from _gla_diag(q.reshape(shape3), g3, k.reshape(shape3), v.reshape(shape3))
    new = [sts[b] * jnp.exp(g_tot[b]) + upds[b] for b in range(nb)]
    return y3.reshape(ntok, LANES) + jnp.concatenate(ys, axis=0), new


def _hgrn_body(*refs, t, nb, has_state, layer):
    if has_state:
        (q_ref, f_ref, i_ref, g_ref, lb_ref, ng_ref, sin_ref, y_ref, sout_ref) = refs
    else:
        (q_ref, f_ref, i_ref, g_ref, lb_ref, ng_ref, y_ref, sout_ref) = refs
    hl = lb_ref[...]
    e = jnp.exp(hl - jnp.max(hl, axis=0, keepdims=True))
    sm = e / jnp.sum(e, axis=0, keepdims=True)
    lb_all = (jnp.sum(sm[1:layer + 1], axis=0, keepdims=True) if layer > 0
              else jnp.zeros((1, H_HPS * LANES), F32))
    ng = ng_ref[...]

    def lanes(hi):
        return slice(hi * LANES, (hi + 1) * LANES)

    def gates(rows, hi):
        lb = lb_all[:, lanes(hi)]
        q = _silu(q_ref[rows, lanes(hi)])
        f = lb + (1.0 - lb) * _sigmoid(f_ref[rows, lanes(hi)])
        return q, jnp.log(f), 1.0 - f, i_ref[rows, lanes(hi)]

    def finish(rows, hi, yy):
        yy = yy * lax.rsqrt(jnp.mean(yy * yy, axis=-1, keepdims=True) + EPS) * ng
        y_ref[rows, lanes(hi)] = (yy * _silu(g_ref[rows, lanes(hi)])).astype(y_ref.dtype)

    if has_state:
        rows = slice(None)
        res = _lockstep([_gla_sample_chunk(nb, *gates(rows, hi), [sin_ref[b, hi].T for b in range(nb)])
                         for hi in range(H_HPS)])
        for hi, (yy, sts) in enumerate(res):
            finish(rows, hi, yy)
            for b in range(nb):
                sout_ref[b, hi] = sts[b].T
        return

    def run(start, nv, sts):
        rows = pl.ds(pl.multiple_of(start, SUBLANES), nv)
        res = _lockstep([_gla_prompt_chunk(nv, *gates(rows, hi), sts[hi]) for hi in range(H_HPS)])
        for hi, (yy, _) in enumerate(res):
            finish(rows, hi, yy)
        return tuple(st for _, st in res)

    sts = tuple(jnp.zeros((H_VDIM, H_EXPAND), F32) for _ in range(H_HPS))
    n_full = t // H_CHUNK
    sts = lax.fori_loop(0, n_full, lambda c, s: run(c * H_CHUNK, H_CHUNK, s), sts)
    if t % H_CHUNK:
        sts = run(n_full * H_CHUNK, t % H_CHUNK, sts)
    for hi in range(H_HPS):
        sout_ref[0, hi] = sts[hi].T


def _hgrn(layer, proj, h_lb, h_norm_g, group, state=None, y_into=None):
    nseq, t, row0, nb = group
    rb = nb * t
    r0 = row0 // rb
    assert row0 % rb == 0 and nseq % nb == 0
    has_state = state is not None
    nhb = H_HEADS // H_HPS
    hw = H_HPS * LANES

    def tok(section):
        return pl.BlockSpec((rb, hw), lambda s, h: (s + r0, h + section * nhb))

    in_specs = [tok(0), tok(1), tok(2), tok(3),
                pl.BlockSpec((DEPTH, hw), lambda s, h: (0, h)),
                pl.BlockSpec((1, LANES), lambda s, h: (0, 0))]
    args = [proj, proj, proj, proj, h_lb, h_norm_g]
    if has_state:
        in_specs.append(pl.BlockSpec((None, nb, H_HPS, H_EXPAND, H_VDIM), lambda s, h: (0, s, h, 0, 0)))
        args.append(state)
    body, aliases = _write_into(functools.partial(_hgrn_body, t=t, nb=nb, has_state=has_state, layer=layer),
                                in_specs, args, (y_into,))
    return pl.pallas_call(
        body,
        grid=(nseq // nb, nhb),
        in_specs=in_specs,
        out_specs=[pl.BlockSpec((rb, hw), lambda s, h: (s + r0, h)),
                   pl.BlockSpec((nb, H_HPS, H_EXPAND, H_VDIM), lambda s, h: (s, h, 0, 0))],
        out_shape=[jax.ShapeDtypeStruct((proj.shape[0], D_MODEL), BF16),
                   jax.ShapeDtypeStruct((nseq, H_HEADS, H_EXPAND, H_VDIM), F32)],
        input_output_aliases=aliases,
        compiler_params=_cparams(2),
        name="hgrn_sample" if has_state else "hgrn_prompt",
    )(*args)


def _hgrn_layer(layer, u, groups, state_hgrn, h_w_in, h_lb, h_norm_g, h_w_out):
    proj, w_out_bf = _mm(u, h_w_in, (0,), side=(h_w_out, (0,)), name="hgrn_in")
    y, sp = _hgrn(layer, proj, h_lb, h_norm_g, groups[0])
    y, ss = _hgrn(layer, proj, h_lb, h_norm_g, groups[1], state_hgrn, y_into=y)
    return (y, w_out_bf), sp, ss


def kernel(x_prompt, x_sample, state_ssm, state_conv, state_wkv, state_shift, state_hgrn, meta_tokens, norm_g, ffn_w_gate, ffn_w_up, ffn_w_down, m_w_in, m_conv_w, m_conv_b, m_dt_bias, m_a_log, m_d, m_norm_g, m_w_out, r_mu, r_w_r, r_w_k, r_w_v, r_w0, r_w1, r_w2, r_a0, r_a1, r_a2, r_g1, r_g2, r_k_k, r_k_a, r_r_k, r_ln_g, r_ln_b, r_w_o, h_w_in, h_lb, h_norm_g, h_w_out):
    pieces = [p for b in range(BATCH) for p in (meta_tokens, x_prompt[b])] + [x_sample.reshape(M_S, D_MODEL)]
    h = jnp.concatenate(pieces, axis=0)
    conv_pad = jnp.pad(state_conv, ((0, 0), (0, 0), (SUBLANES - (M_CONV - 1), 0), (0, 0)))

    p_conv, s_conv = [], []
    ssm = (None, None)
    u = _norm(h, norm_g, 0, 0)
    for l in range(DEPTH):
        kind, j = l % 3, l // 3
        act, w_down = _swiglu_up(u, ffn_w_gate, ffn_w_up, ffn_w_down, (l, 0))
        h, u = _down_resnorm(act, w_down, h, norm_g, 0.5, l, 1, l, 2, u_dtype=F32 if kind == 1 else BF16)
        if kind == 0:
            y, (sp, cp), (ss, cs) = _mamba_layer(j, u, GROUPS, m_w_in, m_conv_w, m_conv_b, m_dt_bias, m_a_log,
                                                 m_d, m_norm_g, m_w_out, conv_pad, state_ssm, ssm)
            ssm = (sp, ss)
            p_conv.append(cp)
            s_conv.append(cs)
        elif kind == 1:
            y, (p_wkv, p_shift), (s_wkv, s_shift) = _rwkv_layer(
                u, GROUPS, state_wkv, state_shift, r_mu, r_w_r, r_w_k, r_w_v, r_w0, r_w1, r_w2, r_a0, r_a1,
                r_a2, r_g1, r_g2, r_k_k, r_k_a, r_r_k, r_ln_g, r_ln_b, r_w_o)
        else:
            y, p_hgrn, s_hgrn = _hgrn_layer(l, u, GROUPS, state_hgrn, h_w_in, h_lb, h_norm_g, h_w_out)
        h, u = _down_resnorm(*y, h, norm_g, 1.0, l, 3, l, 4)
        act, w_down = _swiglu_up(u, ffn_w_gate, ffn_w_up, ffn_w_down, (l, 1))
        if l + 1 < DEPTH:
            h, u = _down_resnorm(act, w_down, h, norm_g, 0.5, l, 5, l + 1, 0)
        else:
            h, _ = _down_resnorm(act, w_down, h, norm_g, 0.5, l, 5)

    y_prompt = jnp.stack([h[b * T_P + N_META:(b + 1) * T_P] for b in range(BATCH)])
    y_sample = h[M_P:].reshape(DEC_BATCH, T_S, D_MODEL)
    return (y_prompt, y_sample,
            ssm[0], jnp.stack(p_conv), p_wkv[None], p_shift[None], p_hgrn[None],
            ssm[1], jnp.stack(s_conv), s_wkv[None], s_shift[None], s_hgrn[None])
```

```python
import functools

import jax
import jax.numpy as jnp
from jax import lax
from jax.experimental import pallas as pl
from jax.experimental.pallas import tpu as pltpu

F32 = jnp.float32
BF16 = jnp.bfloat16

D_MODEL = 2048
BATCH = 4
SEQ = 2048
DEPTH = 4
DEC_BATCH = 128
DEC_SEQ = 8
N_META = 16
EPS = 1e-6

M_D_INNER = 2 * D_MODEL
M_HEADDIM = 64
M_HEADS = M_D_INNER // M_HEADDIM
M_GROUPS = 8
M_D_STATE = 128
M_CONV = 4
M_CONV_DIM = M_D_INNER + 2 * M_GROUPS * M_D_STATE
M_NORM_EPS = 1e-5
M_GW = M_D_INNER // M_GROUPS
M_R = M_HEADS // M_GROUPS

R_HEADSIZE = 64
R_HEADS = D_MODEL // R_HEADSIZE
R_LN_EPS = 64e-5

H_HEADS = 16
H_EXPAND = 128
H_VDIM = D_MODEL // H_HEADS

T_P = N_META + SEQ
T_S = DEC_SEQ
M_P = BATCH * T_P
M_S = DEC_BATCH * T_S
M_TOK = M_P + M_S
S_BS = 8
LANES = 128
SUBLANES = 8
BF16_ROWS = 16
VMEM_LIMIT = 56 * 1024 * 1024
LOG2_E = 1.4426950408889634
GROUPS = ((BATCH, T_P, 0, 1), (DEC_BATCH, T_S, M_P, S_BS))


def _cparams(n_axes):
    return pltpu.CompilerParams(dimension_semantics=("arbitrary",) * n_axes,
                                vmem_limit_bytes=VMEM_LIMIT)


def _silu(x):
    h = 0.5 * x
    return h + h * jnp.tanh(h)


def _sigmoid(x):
    return 1.0 / (1.0 + jnp.exp(-x))


def _softplus(x):
    return jnp.maximum(x, 0.0) + jnp.log1p(jnp.exp(-jnp.abs(x)))


def _iota(shape, dim):
    return lax.broadcasted_iota(jnp.int32, shape, dim)


def _log2(n):
    assert n & (n - 1) == 0
    return n.bit_length() - 1


def _dot(a, b):
    return jnp.dot(a.astype(BF16), b.astype(BF16), preferred_element_type=F32)


def _dot_nt(a, b):
    return lax.dot_general(a.astype(BF16), b.astype(BF16), (((1,), (1,)), ((), ())),
                           preferred_element_type=F32)


def _dot_tn(a, b):
    return lax.dot_general(a.astype(BF16), b.astype(BF16), (((0,), (0,)), ((), ())),
                           preferred_element_type=F32)


def _split(x, terms):
    parts = []
    rem = x
    for _ in range(terms):
        hi = rem.astype(BF16)
        parts.append(hi)
        rem = rem - hi.astype(F32)
    return parts


def _xdot(x, sel, terms=3):
    return sum(jnp.dot(p, sel, preferred_element_type=F32) for p in _split(x, terms))


def _xdot_left(sel, x, terms=3):
    return sum(jnp.dot(sel, p, preferred_element_type=F32) for p in _split(x, terms))


def _xdot_nt(sel, x, terms=3):
    return sum(lax.dot_general(sel, p, (((1,), (1,)), ((), ())), preferred_element_type=F32)
               for p in _split(x, terms))


def _pad_rows(x, n):
    if x.shape[0] >= n:
        return x
    return jnp.concatenate([x, jnp.zeros((n - x.shape[0],) + x.shape[1:], x.dtype)], axis=0)


def _seq_masks(ntok, n):
    r, c = _iota((ntok, ntok), 0), _iota((ntok, ntok), 1)
    same = (r >> _log2(n)) == (c >> _log2(n))
    return same, same & (r >= c)


def _lockstep(gens):
    results = [None] * len(gens)
    live = list(enumerate(gens))
    while live:
        still = []
        for i, gen in live:
            try:
                next(gen)
                still.append((i, gen))
            except StopIteration as done:
                results[i] = done.value
        live = still
    return results


def _write_into(body, in_specs, args, targets):
    n_in = len(args)
    aliases = {}
    for k, arr in enumerate(targets):
        if arr is not None:
            aliases[len(args)] = k
            in_specs.append(pl.BlockSpec(memory_space=pl.ANY))
            args.append(arr)
    n_extra = len(args) - n_in

    def wrapped(*refs):
        return body(*refs[:n_in], *refs[n_in + n_extra:])

    return (wrapped if n_extra else body), aliases


def _row_tile(m, want):
    return next(d for d in range(want - want % BF16_ROWS, 0, -BF16_ROWS) if m % d == 0)


def _single(mode):
    return dict(pipeline_mode=pl.Buffered(1)) if mode else {}


SIDE_SLICES = 4


def _side_cast_specs(side, nj, ni):
    w, lead = side
    rows, d = w.shape[-2:]
    nsub = min(SIDE_SLICES, ni)
    sr = rows // (nj * nsub)
    assert rows % (nj * nsub) == 0 and sr % BF16_ROWS == 0

    def block(j, i):
        return j * nsub + jnp.minimum(i, nsub - 1)

    in_spec = pl.BlockSpec((None,) * len(lead) + (sr, d), lambda j, i: lead + (block(j, i), 0))
    out_spec = pl.BlockSpec((sr, d), lambda j, i: (block(j, i), 0))
    return in_spec, out_spec, jax.ShapeDtypeStruct((rows, d), BF16)


def _mm_body(x_ref, w_ref, *rest, act, valid_cols, w_t):
    if len(rest) == 4:
        side_ref, o_ref, side_bf_ref, wbf_ref = rest
    else:
        (o_ref, wbf_ref), side_ref = rest, None

    @pl.when(pl.program_id(1) == 0)
    def _():
        w = w_ref[...].T if w_t else w_ref[...]
        if valid_cols is not None:
            w = jnp.where(_iota(w.shape, 1) < valid_cols, w, 0.0)
        wbf_ref[...] = w.astype(BF16)

    acc = jnp.dot(x_ref[...].astype(BF16), wbf_ref[...], preferred_element_type=F32)
    if act == "tanh":
        acc = jnp.tanh(acc)
    elif act == "sigmoid":
        acc = _sigmoid(acc)
    o_ref[...] = acc.astype(o_ref.dtype)
    if side_ref is not None:
        side_bf_ref[...] = side_ref[...].astype(BF16)


def _mm(x, w, lead=(), col0=0, ncols=None, tn=512, tm=1856, out_dtype=F32, act=None, w_t=False, side=None,
        name="mm"):
    m, k = x.shape
    n_all = w.shape[-2] if w_t else w.shape[-1]
    ncols = n_all if ncols is None else ncols
    tn = min(tn, ncols)
    tm = _row_tile(m, tm)
    assert ncols % tn == 0 and col0 % tn == 0 and m % tm == 0 and w.shape[-1 if w_t else -2] == k
    valid_cols = None
    if col0 + ncols > n_all:
        assert ncols == tn
        valid_cols = n_all - col0
    jb = col0 // tn
    if w_t:
        wspec = pl.BlockSpec((None,) * len(lead) + (tn, k), lambda j, i: lead + (j + jb, 0))
    else:
        wspec = pl.BlockSpec((None,) * len(lead) + (k, tn), lambda j, i: lead + (0, j + jb))
    grid = (ncols // tn, m // tm)
    in_specs = [pl.BlockSpec((tm, k), lambda j, i: (i, 0)), wspec]
    out_specs = [pl.BlockSpec((tm, tn), lambda j, i: (i, j))]
    out_shape = [jax.ShapeDtypeStruct((m, ncols), out_dtype)]
    args = [x, w]
    if side is not None:
        side_in, side_out, side_shape = _side_cast_specs(side, *grid)
        in_specs.append(side_in)
        out_specs.append(side_out)
        out_shape.append(side_shape)
        args.append(side[0])
    res = pl.pallas_call(
        functools.partial(_mm_body, act=act, valid_cols=valid_cols, w_t=w_t),
        grid=grid,
        in_specs=in_specs,
        out_specs=out_specs,
        out_shape=out_shape,
        scratch_shapes=[pltpu.VMEM((k, tn), BF16)],
        compiler_params=_cparams(2),
        name=name,
    )(*args)
    return res[0] if side is None else res


def _swiglu_up_body(x_ref, wg_ref, wu_ref, wd_ref, o_ref, wd_bf_ref, wg_bf, wu_bf):
    @pl.when(pl.program_id(1) == 0)
    def _():
        wg_bf[...] = wg_ref[...].astype(BF16)
        wu_bf[...] = wu_ref[...].astype(BF16)

    x = x_ref[...]
    g = jnp.dot(x, wg_bf[...], preferred_element_type=F32)
    u = jnp.dot(x, wu_bf[...], preferred_element_type=F32)
    o_ref[...] = (g * _sigmoid(g) * u).astype(o_ref.dtype)
    wd_bf_ref[...] = wd_ref[...].astype(BF16)


def _swiglu_up(x, w_gate, w_up, w_down, lead, tn=512, tm=1856):
    m, k = x.shape
    n = w_gate.shape[-1]
    tm = _row_tile(m, tm)
    wspec = pl.BlockSpec((None,) * len(lead) + (k, tn), lambda j, i: lead + (0, j))
    grid = (n // tn, m // tm)
    side_in, side_out, side_shape = _side_cast_specs((w_down, lead), *grid)
    return pl.pallas_call(
        _swiglu_up_body,
        grid=grid,
        in_specs=[pl.BlockSpec((tm, k), lambda j, i: (i, 0)), wspec, wspec, side_in],
        out_specs=[pl.BlockSpec((tm, tn), lambda j, i: (i, j)), side_out],
        out_shape=[jax.ShapeDtypeStruct((m, n), BF16), side_shape],
        scratch_shapes=[pltpu.VMEM((k, tn), BF16), pltpu.VMEM((k, tn), BF16)],
        compiler_params=_cparams(2),
        name="swiglu_up",
    )(x, w_gate, w_up, w_down)


def _rms(x, g):
    return x * lax.rsqrt(jnp.mean(x * x, axis=-1, keepdims=True) + EPS) * g


def _norm_body(h_ref, g_ref, u_ref, *, gi):
    u_ref[...] = _rms(h_ref[...], g_ref[gi:gi + 1, :]).astype(u_ref.dtype)


def _norm(h, norm_g, layer, gi, out_dtype=BF16, tm=464):
    m, d = h.shape
    tm = _row_tile(m, tm)
    return pl.pallas_call(
        functools.partial(_norm_body, gi=gi),
        grid=(m // tm,),
        in_specs=[pl.BlockSpec((tm, d), lambda i: (i, 0)),
                  pl.BlockSpec((None, 6, d), lambda i: (layer, 0, 0))],
        out_specs=pl.BlockSpec((tm, d), lambda i: (i, 0)),
        out_shape=jax.ShapeDtypeStruct((m, d), out_dtype),
        compiler_params=_cparams(1),
        name="norm",
    )(h, norm_g)


def _down_resnorm_body(x_ref, w_ref, h_ref, ga_ref, gb_ref, hn_ref, *rest, scale, ia, ib):
    u_ref, y_scr = rest if len(rest) == 2 else (None, rest[0])
    j = pl.program_id(1)
    y_scr[j] = jnp.dot(x_ref[...], w_ref[...], preferred_element_type=F32)

    @pl.when(j == pl.num_programs(1) - 1)
    def _():
        y = jnp.concatenate([y_scr[c] for c in range(y_scr.shape[0])], axis=1)
        hn = h_ref[...] + scale * _rms(y, ga_ref[ia:ia + 1, :])
        hn_ref[...] = hn
        if u_ref is not None:
            u_ref[...] = _rms(hn, gb_ref[ib:ib + 1, :]).astype(u_ref.dtype)


RESIDENT_W_BYTES = 8 * 1024 * 1024


def _down_resnorm(x, w_bf, h, norm_g, scale, la, ia, lb=None, ib=None, u_dtype=BF16, tm=464):
    m, k = x.shape
    d = w_bf.shape[-1]
    tn = d if k * d * w_bf.dtype.itemsize <= RESIDENT_W_BYTES else 512
    tm = _row_tile(m, tm)
    row = pl.BlockSpec((tm, d), lambda i, j: (i, 0))
    with_u = lb is not None
    if not with_u:
        lb = la
    out_shape = [jax.ShapeDtypeStruct((m, d), F32)]
    out_specs = [row]
    if with_u:
        out_shape.append(jax.ShapeDtypeStruct((m, d), u_dtype))
        out_specs.append(row)
    res = pl.pallas_call(
        functools.partial(_down_resnorm_body, scale=scale, ia=ia, ib=ib),
        grid=(m // tm, d // tn),
        in_specs=[pl.BlockSpec((tm, k), lambda i, j: (i, 0)),
                  pl.BlockSpec((k, tn), lambda i, j: (0, j)),
                  row,
                  pl.BlockSpec((None, 6, d), lambda i, j: (la, 0, 0)),
                  pl.BlockSpec((None, 6, d), lambda i, j: (lb, 0, 0))],
        out_specs=out_specs,
        out_shape=out_shape,
        scratch_shapes=[pltpu.VMEM((d // tn, tm, tn), F32)],
        compiler_params=_cparams(2),
        name="down_resnorm",
    )(x, w_bf, h, norm_g, norm_g)
    return res if with_u else (res[0], None)


SSD_CHUNK = 128
SSD_GPS = 2


def _conv_taps(cur, shifted, w, b):
    out = b + cur * w[M_CONV - 1:M_CONV]
    for k in range(1, M_CONV):
        out = out + shifted(k) * w[M_CONV - 1 - k:M_CONV - k]
    return _silu(out)


def _conv_silu_seq(cur, prv, w, b):
    row = _iota((SUBLANES, cur.shape[1]), 0)

    def shifted(k):
        sh = pltpu.roll(cur, k, 0)
        top = jnp.where(row < k, pltpu.roll(prv, k, 0), sh[:SUBLANES])
        return jnp.concatenate([top, sh[SUBLANES:]], axis=0)

    return _conv_taps(cur, shifted, w, b)


def _conv_silu_batch(cur, stt, w, b):
    ntok = cur.shape[0]
    tpos = _iota((ntok, 1), 0) & (SUBLANES - 1)

    def shifted(k):
        return jnp.where(tpos < k, pltpu.roll(stt, (k - SUBLANES) % ntok, 0), pltpu.roll(cur, k, 0))

    return _conv_taps(cur, shifted, w, b)


def _expand_heads(rows):
    head_of_lane = _iota((M_R, M_GW), 1) >> _log2(M_HEADDIM)
    sel = (head_of_lane == _iota((M_R, M_GW), 0)).astype(BF16)
    return _xdot(rows, sel, terms=2)


def _ssd_common(x, bm, cm, dtr, hp, causal, same, nv, extra_rows):
    ntok = x.shape[0]
    bias, a = hp[0:1], -jnp.exp(hp[1:2])
    dt = _softplus(dtr + bias)
    if nv < ntok and same is None:
        dt = jnp.where(_iota((ntok, 1), 0) < nv, dt, 0.0)
    da = dt * a
    cum = _xdot_left(causal.astype(BF16), da)
    cum_tot = cum[ntok - 1:ntok] if same is None else _xdot_left(same.astype(BF16), da)
    cbm = _dot_nt(cm, bm)
    eye8 = (_iota((M_R, M_R), 0) == _iota((M_R, M_R), 1)).astype(BF16)
    dt_t = _xdot_nt(eye8, dt)
    even = (_iota((1, M_GW), 1) & M_HEADDIM) == 0
    x_e = jnp.where(even, x, 0.0).astype(BF16)
    x_o = jnp.where(even, 0.0, x).astype(BF16)
    yield
    cum2 = cum * LOG2_E
    cum2_t = _xdot_nt(eye8, cum2)
    tail = jnp.exp(cum_tot - cum) * dt
    ex = _expand_heads(jnp.concatenate([jnp.exp(cum), tail, extra_rows(cum_tot)], axis=0))
    yield
    parts = []
    for q in range(M_R // 2):
        acc = None
        for r, x_m in ((2 * q, x_e), (2 * q + 1, x_o)):
            dec = jnp.exp2(jnp.where(causal, cum2[:, r:r + 1] - cum2_t[r:r + 1, :], -jnp.inf))
            sc = (cbm * dec * dt_t[r:r + 1, :]).astype(BF16)
            yy = jnp.dot(sc, x_m[:, q * LANES:(q + 1) * LANES], preferred_element_type=F32)
            acc = yy if acc is None else acc + yy
        parts.append(acc)
    return jnp.concatenate(parts, axis=1), ex, cum_tot


def _ssd_finish(y, x, zc, d_row, ng):
    yv = (y + d_row * x) * _silu(zc)
    return yv * lax.rsqrt(jnp.mean(yv * yv, axis=-1, keepdims=True) + M_NORM_EPS) * ng


def _ssd_prompt_chunk(nv, zc, xr, br, cr, dtr, prev, st, cw, cb, hp, ng):
    n = SSD_CHUNK
    new_prev = (xr[nv - SUBLANES:], br[nv - SUBLANES:], cr[nv - SUBLANES:])
    zc, xr, br, cr, dtr = (_pad_rows(v, n) for v in (zc, xr, br, cr, dtr))
    x = _conv_silu_seq(xr, prev[0], cw[0], cb[0])
    bm = _conv_silu_seq(br, prev[1], cw[1], cb[1])
    cm = _conv_silu_seq(cr, prev[2], cw[2], cb[2])
    causal = _iota((n, n), 0) >= _iota((n, n), 1)
    row = _iota((SUBLANES, M_R), 0)

    def extra_rows(cum_tot):
        return jnp.where(row == 0, jnp.exp(cum_tot), jnp.where(row == 1, hp[2:3], 0.0))

    y_state = _dot(cm, st)
    y, ex, _ = yield from _ssd_common(x, bm, cm, dtr, hp, causal, None, nv, extra_rows)
    upd = _dot_tn(bm, ex[n:2 * n] * x)
    yield
    y = y + y_state * ex[:n]
    st = st * ex[2 * n:2 * n + 1] + upd
    return _ssd_finish(y, x, zc, ex[2 * n + 1:2 * n + 2], ng)[:nv], new_prev, st


def _ssd_sample_chunk(nb, zc, xr, br, cr, dtr, stt, states, cw, cb, hp, ng):
    n = SUBLANES
    ntok = nb * n
    x = _conv_silu_batch(xr, stt[0], cw[0], cb[0])
    bm = _conv_silu_batch(br, stt[1], cw[1], cb[1])
    cm = _conv_silu_batch(cr, stt[2], cw[2], cb[2])
    same, causal = _seq_masks(ntok, n)
    row = _iota((SUBLANES, M_R), 0)
    ys = [_dot_nt(_pad_rows(cm[b * n:(b + 1) * n], BF16_ROWS), states[b])[:n] for b in range(nb)]
    y, ex, cum_tot = yield from _ssd_common(x, bm, cm, dtr, hp, causal, same, n,
                                            lambda _: jnp.where(row == 0, hp[2:3], 0.0))
    xt = ex[ntok:2 * ntok] * x
    upds = [_dot_tn(_pad_rows(xt[b * n:(b + 1) * n], BF16_ROWS), _pad_rows(bm[b * n:(b + 1) * n], BF16_ROWS))
            for b in range(nb)]
    yield
    etot = jnp.exp(cum_tot)
    new_states = []
    for b in range(nb):
        decay = jnp.concatenate([jnp.broadcast_to(etot[b * n:b * n + 1, r:r + 1], (M_HEADDIM, M_D_STATE))
                                 for r in range(M_R)], axis=0)
        new_states.append(states[b] * decay + upds[b])
    y = y + jnp.concatenate(ys, axis=0) * ex[:ntok]
    return _ssd_finish(y, x, zc, ex[2 * ntok:2 * ntok + 1], ng), new_states


def _ssd_body(*refs, t, nb, has_state):
    if has_state:
        (z_ref, x_ref, b_ref, c_ref, dt_ref, cwx_ref, cwb_ref, cwc_ref, cbx_ref, cbb_ref, cbc_ref,
         hp_ref, ng_ref, csx_ref, csb_ref, csc_ref, sin_ref, y_ref, sout_ref) = refs
    else:
        (z_ref, x_ref, b_ref, c_ref, dt_ref, cwx_ref, cwb_ref, cwc_ref, cbx_ref, cbb_ref, cbc_ref,
         hp_ref, ng_ref, y_ref, sout_ref) = refs

    def consts(gi):
        wide = slice(gi * M_GW, (gi + 1) * M_GW)
        nar = slice(gi * M_D_STATE, (gi + 1) * M_D_STATE)
        cw = (cwx_ref[:, wide], cwb_ref[:, nar], cwc_ref[:, nar])
        cb = (cbx_ref[:, wide], cbb_ref[:, nar], cbc_ref[:, nar])
        return wide, nar, cw, cb, hp_ref[gi], ng_ref[:, wide]

    if has_state:
        ntok = nb * t
        gens = []
        for gi in range(SSD_GPS):
            wide, nar, cw, cb, hp, ng = consts(gi)
            stt = (csx_ref[:, :, wide].reshape(ntok, M_GW), csb_ref[:, :, nar].reshape(ntok, M_D_STATE),
                   csc_ref[:, :, nar].reshape(ntok, M_D_STATE))
            states = [sin_ref[b, gi * M_R:(gi + 1) * M_R].reshape(M_GW, M_D_STATE) for b in range(nb)]
            gens.append(_ssd_sample_chunk(nb, z_ref[:, wide], x_ref[:, wide], b_ref[:, nar], c_ref[:, nar],
                                          dt_ref[gi], stt, states, cw, cb, hp, ng))
        for gi, (out, states) in enumerate(_lockstep(gens)):
            y_ref[:, gi * M_GW:(gi + 1) * M_GW] = out.astype(y_ref.dtype)
            for b in range(nb):
                sout_ref[b, gi * M_R:(gi + 1) * M_R] = states[b].reshape(M_R, M_HEADDIM, M_D_STATE)
        return

    def run(start, nv, carry):
        rows = pl.ds(pl.multiple_of(start, SUBLANES), nv)
        gens = []
        for gi in range(SSD_GPS):
            wide, nar, cw, cb, hp, ng = consts(gi)
            prev, st = carry[gi]
            gens.append(_ssd_prompt_chunk(nv, z_ref[rows, wide], x_ref[rows, wide], b_ref[rows, nar],
                                          c_ref[rows, nar], dt_ref[gi, rows, :], prev, st, cw, cb, hp, ng))
        new = []
        for gi, (out, prev, st) in enumerate(_lockstep(gens)):
            y_ref[rows, gi * M_GW:(gi + 1) * M_GW] = out.astype(y_ref.dtype)
            new.append((prev, st))
        return tuple(new)

    zero_prev = (jnp.zeros((SUBLANES, M_GW), F32), jnp.zeros((SUBLANES, M_D_STATE), F32),
                 jnp.zeros((SUBLANES, M_D_STATE), F32))
    carry = tuple((zero_prev, jnp.zeros((M_D_STATE, M_GW), F32)) for _ in range(SSD_GPS))
    n_full = t // SSD_CHUNK
    carry = lax.fori_loop(0, n_full, lambda c, cr: run(c * SSD_CHUNK, SSD_CHUNK, cr), carry)
    if t % SSD_CHUNK:
        carry = run(n_full * SSD_CHUNK, t % SSD_CHUNK, carry)
    for gi in range(SSD_GPS):
        sout_ref[0, gi * M_R:(gi + 1) * M_R] = carry[gi][1].T.reshape(M_R, M_HEADDIM, M_D_STATE)


def _ssd(j, z, xbc, dtg, m_conv_w, conv_b3, hp, m_norm_g3, group, conv_state=None, ssm_state=None,
         y_into=None, state_into=None):
    nseq, t, row0, nb = group
    rb = nb * t
    r0 = row0 // rb
    assert row0 % rb == 0 and nseq % nb == 0 and M_GROUPS % SSD_GPS == 0
    has_state = ssm_state is not None
    gw, gn, gr = SSD_GPS * M_GW, SSD_GPS * M_D_STATE, SSD_GPS * M_R
    b0 = M_D_INNER // gn
    c0 = b0 + M_GROUPS // SSD_GPS
    one = _single(not has_state)
    in_specs = [
        pl.BlockSpec((rb, gw), lambda s, g: (s + r0, g)),
        pl.BlockSpec((rb, gw), lambda s, g: (s + r0, g)),
        pl.BlockSpec((rb, gn), lambda s, g: (s + r0, b0 + g), **one),
        pl.BlockSpec((rb, gn), lambda s, g: (s + r0, c0 + g), **one),
        pl.BlockSpec((SSD_GPS, rb, M_R), lambda s, g: (g, s + r0, 0), **one),
        pl.BlockSpec((None, M_CONV, gw), lambda s, g: (j, 0, g)),
        pl.BlockSpec((None, M_CONV, gn), lambda s, g: (j, 0, b0 + g)),
        pl.BlockSpec((None, M_CONV, gn), lambda s, g: (j, 0, c0 + g)),
        pl.BlockSpec((None, 1, gw), lambda s, g: (j, 0, g)),
        pl.BlockSpec((None, 1, gn), lambda s, g: (j, 0, b0 + g)),
        pl.BlockSpec((None, 1, gn), lambda s, g: (j, 0, c0 + g)),
        pl.BlockSpec((None, SSD_GPS, 3, M_R), lambda s, g: (j, g, 0, 0)),
        pl.BlockSpec((None, 1, gw), lambda s, g: (j, 0, g)),
    ]
    args = [z, xbc, xbc, xbc, dtg, m_conv_w, m_conv_w, m_conv_w, conv_b3, conv_b3, conv_b3, hp, m_norm_g3]
    if has_state:
        in_specs += [
            pl.BlockSpec((None, nb, SUBLANES, gw), lambda s, g: (j, s, 0, g)),
            pl.BlockSpec((None, nb, SUBLANES, gn), lambda s, g: (j, s, 0, b0 + g)),
            pl.BlockSpec((None, nb, SUBLANES, gn), lambda s, g: (j, s, 0, c0 + g)),
            pl.BlockSpec((None, nb, gr, M_HEADDIM, M_D_STATE), lambda s, g: (j, s, g, 0, 0)),
        ]
        args += [conv_state, conv_state, conv_state, ssm_state]
    body, aliases = _write_into(functools.partial(_ssd_body, t=t, nb=nb, has_state=has_state),
                                in_specs, args, (y_into, state_into))
    return pl.pallas_call(
        body,
        grid=(nseq // nb, M_GROUPS // SSD_GPS),
        in_specs=in_specs,
        out_specs=[pl.BlockSpec((rb, gw), lambda s, g: (s + r0, g)),
                   pl.BlockSpec((None, nb, gr, M_HEADDIM, M_D_STATE), lambda s, g: (j, s, g, 0, 0))],
        out_shape=[jax.ShapeDtypeStruct((z.shape[0], M_D_INNER), BF16),
                   jax.ShapeDtypeStruct((hp.shape[0], nseq, M_HEADS, M_HEADDIM, M_D_STATE), F32)],
        input_output_aliases=aliases,
        compiler_params=_cparams(2),
        name="ssd_sample" if has_state else "ssd_prompt",
    )(*args)


def _mamba_layer(j, u, groups, m_w_in, m_conv_w, m_conv_b, m_dt_bias, m_a_log, m_d, m_norm_g, m_w_out,
                 state_conv_pad, state_ssm, ssm_into=(None, None)):
    m = u.shape[0]
    gp, gs = groups
    w_in_t = jnp.swapaxes(m_w_in, 1, 2)
    z, w_out_bf = _mm(u, w_in_t, (j,), 0, M_D_INNER, w_t=True, side=(m_w_out, (j,)), name="ssd_in_z")
    xbc = _mm(u, w_in_t, (j,), M_D_INNER, M_CONV_DIM, w_t=True, name="ssd_in_xbc")
    dt = _mm(u, w_in_t, (j,), M_D_INNER + M_CONV_DIM, LANES, tn=LANES, w_t=True, name="ssd_in_dt")[:, :M_HEADS]
    dtg = dt.reshape(m, M_GROUPS, M_R).transpose(1, 0, 2)
    hp = jnp.stack([m_dt_bias, m_a_log, m_d], axis=1).reshape(-1, 3, M_GROUPS, M_R).transpose(0, 2, 1, 3)
    conv_b3 = m_conv_b[:, None, :]
    ng3 = m_norm_g[:, None, :]
    y, sp = _ssd(j, z, xbc, dtg, m_conv_w, conv_b3, hp, ng3, gp, state_into=ssm_into[0])
    y, ss = _ssd(j, z, xbc, dtg, m_conv_w, conv_b3, hp, ng3, gs, state_conv_pad, state_ssm,
                 y_into=y, state_into=ssm_into[1])
    keep = M_CONV - 1
    xp = jnp.stack([xbc[(b + 1) * gp[1] - keep:(b + 1) * gp[1]] for b in range(gp[0])])
    xs = xbc[gp[0] * gp[1]:].reshape(gs[0], gs[1], M_CONV_DIM)[:, gs[1] - keep:]
    return (y, w_out_bf), (sp, xp), (ss, xs)


R_TOK = 64
R_PAIRS = 4


def _rwkv_chunk(nb, nv, toks, sbs, prm):
    n = R_TOK // nb
    ntok = R_TOK
    r, kr, v, wl, al, g = (_pad_rows(t_, ntok) for t_ in toks)
    w0, a0, k_k, k_a, r_k, ln_g, ln_b = prm
    m0 = _iota((1, LANES), 1) < R_HEADSIZE
    ones_blk = ((_iota((LANES, LANES), 0) < R_HEADSIZE) == (_iota((LANES, LANES), 1) < R_HEADSIZE)).astype(BF16)

    w = -_softplus(-(w0 + wl)) - 0.5
    lw = -jnp.exp(w)
    a = _sigmoid(a0 + al)
    kk = kr * k_k
    k = kr * (1.0 + (a - 1.0) * k_a)
    if nv < n:
        valid = _iota((ntok, 1), 0) < nv
        lw = jnp.where(valid, lw, 0.0)
        kk = jnp.where(valid, kk, 0.0)
        k = jnp.where(valid, k, 0.0)
        v = jnp.where(valid, v, 0.0)
    same, causal = _seq_masks(ntok, n)
    head_sums = _xdot(jnp.concatenate([kk * kk, r * k * r_k], axis=0), ones_blk, terms=2)
    lam = _xdot_left(causal.astype(BF16), lw)
    lam_tot = lam[ntok - 1:ntok] if nb == 1 else _xdot_left(same.astype(BF16), lw)
    yield
    bonus = head_sums[ntok:] * v
    kk = kk / jnp.maximum(jnp.sqrt(head_sums[:ntok]), 1e-12)
    beta = kk * a
    e_in = jnp.exp(lam)
    e_out = jnp.exp(-lam)
    e_end = jnp.exp(lam_tot - lam)

    def stack(x):
        x3 = x.reshape(nb, n, LANES)
        s = jnp.concatenate([jnp.where(m0, x3, 0.0), jnp.where(m0, 0.0, x3)], axis=1)
        return s.reshape(2 * ntok, LANES).astype(BF16)

    am, rm = stack(kk * jnp.exp(lam - lw)), stack(r * e_in)
    khm, bhm = stack(k * e_out), stack(beta * e_out)
    vst = stack(v)
    n2 = 2 * ntok
    rr, cc = _iota((n2, n2), 0), _iota((n2, n2), 1)
    same_blk = (rr >> _log2(n)) == (cc >> _log2(n))
    strict = same_blk & ((rr & (n - 1)) > (cc & (n - 1)))
    incl = same_blk & ((rr & (n - 1)) >= (cc & (n - 1)))
    qm = jnp.concatenate([am, rm], axis=0)
    pmat = _dot_nt(qm, jnp.concatenate([khm, bhm], axis=0))
    w2 = 2 * n
    if nb == 1:
        qs = [_dot_nt(qm, sbs[0])]
    else:
        qs = [_dot_nt(jnp.concatenate([am[b * w2:(b + 1) * w2], rm[b * w2:(b + 1) * w2]], axis=0), sbs[b])
              for b in range(nb)]
    ktm, btm = stack(k * e_end), stack(beta * e_end)
    yield
    a_ak = jnp.where(strict, pmat[:n2, :n2], 0.0)
    nmat = jnp.where(strict, -pmat[:n2, n2:], 0.0)
    a_rk = jnp.where(incl, pmat[n2:, :n2], 0.0)
    a_rb = jnp.where(incl, pmat[n2:, n2:], 0.0)
    rhs_s = qs[0][:w2] if nb == 1 else jnp.concatenate([q[:w2] for q in qs], axis=0)
    ys = qs[0][w2:] if nb == 1 else jnp.concatenate([q[w2:] for q in qs], axis=0)

    tm = (rr == cc).astype(F32) + nmat
    npow = _dot(nmat, nmat)
    av = _dot(jnp.concatenate([a_ak, a_rk], axis=0), vst)
    yield
    rhs = rhs_s + av[:n2]
    ys = ys + av[n2:]
    for i in range(_log2(n) - 1):
        if i + 2 < _log2(n):
            both = _dot(jnp.concatenate([tm, npow], axis=0), npow)
            prod, npow = both[:n2], both[n2:]
        else:
            prod = _dot(tm, npow)
        yield
        tm = tm + prod
    u = _dot(tm, rhs)
    yield
    yst = (ys - _dot(a_rb, u)).reshape(nb, w2, LANES)
    nu = (-u).astype(BF16)
    new_sbs = []
    for b in range(nb):
        rows = slice(b * w2, (b + 1) * w2)
        lt = lam_tot if nb == 1 else lam_tot[b * n:b * n + 1]
        new_sbs.append(sbs[b] * jnp.exp(lt)
                       + _dot_tn(jnp.concatenate([vst[rows], nu[rows]], axis=0),
                                 jnp.concatenate([ktm[rows], btm[rows]], axis=0)))
    yield
    y = (yst[:, :n] + yst[:, n:]).reshape(ntok, LANES)
    inv = 1.0 / R_HEADSIZE
    mean = _xdot(y, ones_blk, terms=2) * inv
    yield
    yc = y - mean
    var = _xdot(yc * yc, ones_blk, terms=2) * inv
    yield
    yn = yc * lax.rsqrt(var + R_LN_EPS) * ln_g + ln_b
    return ((yn + bonus) * g)[:nb * nv], new_sbs


def _rwkv_body(*refs, t, nb, has_state):
    if has_state:
        (r_ref, k_ref, v_ref, wl_ref, al_ref, g_ref, prm_ref, sin_ref, y_ref, sout_ref) = refs
    else:
        (r_ref, k_ref, v_ref, wl_ref, al_ref, g_ref, prm_ref, y_ref, sout_ref) = refs
    tok_refs = (r_ref, k_ref, v_ref, wl_ref, al_ref, g_ref)
    zeros = jnp.zeros((R_HEADSIZE, R_HEADSIZE), F32)

    def lanes(p):
        return slice(p * LANES, (p + 1) * LANES)

    def run(rows, nv, states):
        res = _lockstep([
            _rwkv_chunk(nb, nv, tuple(ref[rows, lanes(p)] for ref in tok_refs), states[p],
                        tuple(prm_ref[i:i + 1, lanes(p)] for i in range(7)))
            for p in range(R_PAIRS)])
        for p, (out, _) in enumerate(res):
            y_ref[rows, lanes(p)] = out.astype(y_ref.dtype)
        return tuple(tuple(sbs) for _, sbs in res)

    if has_state:
        states = tuple(
            tuple(jnp.concatenate([jnp.concatenate([sin_ref[b, 2 * p], zeros], axis=1),
                                   jnp.concatenate([zeros, sin_ref[b, 2 * p + 1]], axis=1)], axis=0)
                  for b in range(nb))
            for p in range(R_PAIRS))
    else:
        states = tuple(tuple(jnp.zeros((LANES, LANES), F32) for _ in range(nb)) for _ in range(R_PAIRS))

    if nb == 1:
        n_full = t // R_TOK
        states = lax.fori_loop(
            0, n_full, lambda c, st: run(pl.ds(pl.multiple_of(c * R_TOK, R_TOK), R_TOK), R_TOK, st), states)
        if t % R_TOK:
            states = run(pl.ds(n_full * R_TOK, t % R_TOK), t % R_TOK, states)
    else:
        assert nb * t == R_TOK
        states = run(slice(None), t, states)
    for p in range(R_PAIRS):
        for b in range(nb):
            sout_ref[b, 2 * p] = states[p][b][:R_HEADSIZE, :R_HEADSIZE]
            sout_ref[b, 2 * p + 1] = states[p][b][R_HEADSIZE:, R_HEADSIZE:]


def _rwkv(r, k, v, wl, al, g, prm, group, wkv_state=None, y_into=None):
    nseq, t, row0, nb = group
    rb = nb * t
    r0 = row0 // rb
    assert row0 % rb == 0 and nseq % nb == 0
    has_state = wkv_state is not None
    gl = R_PAIRS * LANES
    tok = pl.BlockSpec((rb, gl), lambda s, hp_: (s + r0, hp_))
    tok1 = pl.BlockSpec((rb, gl), lambda s, hp_: (s + r0, hp_), **_single(not has_state))
    in_specs = [tok] * 5 + [tok1] + [pl.BlockSpec((SUBLANES, gl), lambda s, hp_: (0, hp_))]
    args = [r, k, v, wl, al, g, prm]
    if has_state:
        in_specs.append(pl.BlockSpec((None, nb, 2 * R_PAIRS, R_HEADSIZE, R_HEADSIZE),
                                     lambda s, hp_: (0, s, hp_, 0, 0)))
        args.append(wkv_state)
    body, aliases = _write_into(functools.partial(_rwkv_body, t=t, nb=nb, has_state=has_state),
                                in_specs, args, (y_into,))
    return pl.pallas_call(
        body,
        grid=(nseq // nb, R_HEADS // (2 * R_PAIRS)),
        in_specs=in_specs,
        out_specs=[pl.BlockSpec((rb, gl), lambda s, hp_: (s + r0, hp_)),
                   pl.BlockSpec((nb, 2 * R_PAIRS, R_HEADSIZE, R_HEADSIZE), lambda s, hp_: (s, hp_, 0, 0))],
        out_shape=[jax.ShapeDtypeStruct((r.shape[0], D_MODEL), BF16),
                   jax.ShapeDtypeStruct((nseq, R_HEADS, R_HEADSIZE, R_HEADSIZE), F32)],
        input_output_aliases=aliases,
        compiler_params=_cparams(2),
        name="rwkv_sample" if has_state else "rwkv_prompt",
    )(*args)


MIX_COLS = 512


def _mix_body(*refs, t, nb, has_state):
    if has_state:
        u_ref, mu_ref, sh_ref = refs[:3]
    else:
        u_ref, mu_ref = refs[:2]
    o_refs = refs[-6:]
    u = u_ref[...]
    rows = nb * t
    tpos = _iota((rows, 1), 0) & (t - 1) if nb > 1 else _iota((rows, 1), 0)
    prev = pltpu.roll(u, 1, 0)
    if has_state:
        pick = (_iota((rows, BF16_ROWS), 0) >> _log2(t)) == _iota((rows, BF16_ROWS), 1)
        first = _xdot_left(pick.astype(BF16), _pad_rows(sh_ref[...], BF16_ROWS))
    else:
        first = 0.0
    dx = jnp.where(tpos == 0, first, prev) - u
    for i, o_ref in enumerate(o_refs):
        o_ref[...] = (u + dx * mu_ref[i:i + 1, :]).astype(o_ref.dtype)


def _rwkv_mix(u, r_mu, group, shift_state=None, into=None):
    nseq, t, row0, nb = group
    m, d = u.shape
    rb = nb * t
    r0 = row0 // rb
    has_state = shift_state is not None
    assert row0 % rb == 0 and nseq % nb == 0 and (nb == 1 or nb <= BF16_ROWS)
    blk = pl.BlockSpec((rb, MIX_COLS), lambda s, c: (s + r0, c))
    in_specs = [blk, pl.BlockSpec((None, 6, MIX_COLS), lambda s, c: (0, 0, c))]
    args = [u, r_mu]
    if has_state:
        in_specs.append(pl.BlockSpec((None, nb, MIX_COLS), lambda s, c: (0, s, c)))
        args.append(shift_state)
    body, aliases = _write_into(functools.partial(_mix_body, t=t, nb=nb, has_state=has_state),
                                in_specs, args, into or (None,) * 6)
    return pl.pallas_call(
        body,
        grid=(nseq // nb, d // MIX_COLS),
        in_specs=in_specs,
        out_specs=[blk] * 6,
        out_shape=[jax.ShapeDtypeStruct((m, d), BF16)] * 6,
        input_output_aliases=aliases,
        compiler_params=_cparams(2),
        name="rwkv_mix_sample" if has_state else "rwkv_mix_prompt",
    )(*args)


def _rwkv_layer(u, groups, state_wkv, state_shift, r_mu, r_w_r, r_w_k, r_w_v, r_w0, r_w1, r_w2, r_a0, r_a1,
                r_a2, r_g1, r_g2, r_k_k, r_k_a, r_r_k, r_ln_g, r_ln_b, r_w_o):
    gp, gs = groups
    mixed = _rwkv_mix(u, r_mu, gp)
    xr, xw, xk, xv, xa, xg = _rwkv_mix(u, r_mu, gs, state_shift, into=mixed)
    r, w_o_bf = _mm(xr, r_w_r, (0,), side=(r_w_o, (0,)), name="rwkv_r")
    k = _mm(xk, r_w_k, (0,), name="rwkv_k")
    v = _mm(xv, r_w_v, (0,), name="rwkv_v")
    wl = _mm(_mm(xw, r_w1, (0,), out_dtype=BF16, act="tanh", name="rwkv_w1"), r_w2, (0,), name="rwkv_w2")
    al = _mm(_mm(xa, r_a1, (0,), out_dtype=BF16, name="rwkv_a1"), r_a2, (0,), name="rwkv_a2")
    g = _mm(_mm(xg, r_g1, (0,), out_dtype=BF16, act="sigmoid", name="rwkv_g1"), r_g2, (0,), name="rwkv_g2")
    prm = jnp.concatenate([r_w0, r_a0, r_k_k, r_k_a, r_r_k.reshape(1, D_MODEL), r_ln_g, r_ln_b,
                           jnp.zeros((1, D_MODEL), F32)], axis=0)
    y, sp = _rwkv(r, k, v, wl, al, g, prm, gp)
    y, ss = _rwkv(r, k, v, wl, al, g, prm, gs, state_wkv, y_into=y)
    last_p = jnp.stack([u[(b + 1) * gp[1] - 1] for b in range(gp[0])])
    last_s = u[gp[0] * gp[1]:].reshape(gs[0], gs[1], D_MODEL)[:, -1]
    return (y, w_o_bf), (sp, last_p), (ss, last_s)


H_CHUNK = 128
H_SUB = 16
H_HPS = 4


def _gla_diag(q3, g3, k3, v3):
    sub = q3.shape[1]
    pieces = [slice(p, p + SUBLANES) for p in range(0, sub, SUBLANES)]
    tpos = _iota((1, SUBLANES, 1), 1)
    g3 = g3 * LOG2_E
    ys = [jnp.zeros(v3[:, p].shape, F32) for p in pieces]
    for s in range(sub):
        terms = []
        for i, p in enumerate(pieces):
            if p.stop <= s:
                continue
            arg = g3[:, p] - g3[:, s:s + 1, :]
            if p.start < s:
                arg = jnp.where(tpos + p.start >= s, arg, -jnp.inf)
            terms.append((i, jnp.sum(q3[:, p] * k3[:, s:s + 1, :] * jnp.exp2(arg), axis=-1, keepdims=True)))
        yield
        for i, a_s in terms:
            ys[i] = ys[i] + a_s * v3[:, s:s + 1, :]
    return ys[0] if len(ys) == 1 else jnp.concatenate(ys, axis=1)


def _gla_prompt_chunk(nv, q, logf, k, v, st):
    n = -(-nv // H_SUB) * H_SUB
    q, logf, k, v = (_pad_rows(t_, n) for t_ in (q, logf, k, v))
    causal = _iota((n, n), 0) >= _iota((n, n), 1)
    gcum = _xdot_left(causal.astype(BF16), logf)
    yield
    y = _dot_nt(q * jnp.exp(gcum), st)
    g_n = gcum[n - 1:n]
    upd = _dot_tn(v, k * jnp.exp(g_n - gcum))
    atts = []
    for i in range(1, n // H_SUB):
        lo = i * H_SUB
        gs = gcum[lo - 1:lo]
        atts.append(_dot_nt(q[lo:lo + H_SUB] * jnp.exp(gcum[lo:lo + H_SUB] - gs),
                            k[:lo] * jnp.exp(gs - gcum[:lo])))
    yield
    st = st * jnp.exp(g_n) + upd
    offs = [jnp.zeros((H_SUB, LANES), F32)] + [_dot(att, v[:(i + 1) * H_SUB]) for i, att in enumerate(atts)]
    shape3 = (n // H_SUB, H_SUB, LANES)
    y3 = yield from _gla_diag(q.reshape(shape3), gcum.reshape(shape3), k.reshape(shape3), v.reshape(shape3))
    off = offs[0] if len(offs) == 1 else jnp.concatenate(offs, axis=0)
    return (y + off + y3.reshape(n, LANES))[:nv], st


def _gla_sample_chunk(nb, q, logf, k, v, sts):
    n = SUBLANES
    ntok = nb * n
    _, causal = _seq_masks(ntok, n)
    gcum = _xdot_left(causal.astype(BF16), logf)
    yield
    shape3 = (nb, n, LANES)
    g3 = gcum.reshape(shape3)
    g_tot = g3[:, n - 1:n, :]
    qe = q * jnp.exp(gcum)
    kd = (k.reshape(shape3) * jnp.exp(g_tot - g3)).reshape(ntok, LANES)
    ys = [_dot_nt(_pad_rows(qe[b * n:(b + 1) * n], BF16_ROWS), sts[b])[:n] for b in range(nb)]
    upds = [_dot_tn(_pad_rows(v[b * n:(b + 1) * n], BF16_ROWS), _pad_rows(kd[b * n:(b + 1) * n], BF16_ROWS))
            for b in range(nb)]
    y3 = yield from _gla_diag(q.reshape(shape3), g3, k.reshape(shape3), v.reshape(shape3))
    new = [sts[b] * jnp.exp(g_tot[b]) + upds[b] for b in range(nb)]
    return y3.reshape(ntok, LANES) + jnp.concatenate(ys, axis=0), new


def _hgrn_body(*refs, t, nb, has_state, layer):
    if has_state:
        (q_ref, f_ref, i_ref, g_ref, lb_ref, ng_ref, sin_ref, y_ref, sout_ref) = refs
    else:
        (q_ref, f_ref, i_ref, g_ref, lb_ref, ng_ref, y_ref, sout_ref) = refs
    hl = lb_ref[...]
    e = jnp.exp(hl - jnp.max(hl, axis=0, keepdims=True))
    sm = e / jnp.sum(e, axis=0, keepdims=True)
    lb_all = (jnp.sum(sm[1:layer + 1], axis=0, keepdims=True) if layer > 0
              else jnp.zeros((1, H_HPS * LANES), F32))
    ng = ng_ref[...]

    def lanes(hi):
        return slice(hi * LANES, (hi + 1) * LANES)

    def gates(rows, hi):
        lb = lb_all[:, lanes(hi)]
        q = _silu(q_ref[rows, lanes(hi)])
        f = lb + (1.0 - lb) * _sigmoid(f_ref[rows, lanes(hi)])
        return q, jnp.log(f), 1.0 - f, i_ref[rows, lanes(hi)]

    def finish(rows, hi, yy):
        yy = yy * lax.rsqrt(jnp.mean(yy * yy, axis=-1, keepdims=True) + EPS) * ng
        y_ref[rows, lanes(hi)] = (yy * _silu(g_ref[rows, lanes(hi)])).astype(y_ref.dtype)

    if has_state:
        rows = slice(None)
        res = _lockstep([_gla_sample_chunk(nb, *gates(rows, hi), [sin_ref[b, hi].T for b in range(nb)])
                         for hi in range(H_HPS)])
        for hi, (yy, sts) in enumerate(res):
            finish(rows, hi, yy)
            for b in range(nb):
                sout_ref[b, hi] = sts[b].T
        return

    def run(start, nv, sts):
        rows = pl.ds(pl.multiple_of(start, SUBLANES), nv)
        res = _lockstep([_gla_prompt_chunk(nv, *gates(rows, hi), sts[hi]) for hi in range(H_HPS)])
        for hi, (yy, _) in enumerate(res):
            finish(rows, hi, yy)
        return tuple(st for _, st in res)

    sts = tuple(jnp.zeros((H_VDIM, H_EXPAND), F32) for _ in range(H_HPS))
    n_full = t // H_CHUNK
    sts = lax.fori_loop(0, n_full, lambda c, s: run(c * H_CHUNK, H_CHUNK, s), sts)
    if t % H_CHUNK:
        sts = run(n_full * H_CHUNK, t % H_CHUNK, sts)
    for hi in range(H_HPS):
        sout_ref[0, hi] = sts[hi].T


def _hgrn(layer, proj, h_lb, h_norm_g, group, state=None, y_into=None):
    nseq, t, row0, nb = group
    rb = nb * t
    r0 = row0 // rb
    assert row0 % rb == 0 and nseq % nb == 0
    has_state = state is not None
    nhb = H_HEADS // H_HPS
    hw = H_HPS * LANES

    def tok(section):
        return pl.BlockSpec((rb, hw), lambda s, h: (s + r0, h + section * nhb))

    in_specs = [tok(0), tok(1), tok(2), tok(3),
                pl.BlockSpec((DEPTH, hw), lambda s, h: (0, h)),
                pl.BlockSpec((1, LANES), lambda s, h: (0, 0))]
    args = [proj, proj, proj, proj, h_lb, h_norm_g]
    if has_state:
        in_specs.append(pl.BlockSpec((None, nb, H_HPS, H_EXPAND, H_VDIM), lambda s, h: (0, s, h, 0, 0)))
        args.append(state)
    body, aliases = _write_into(functools.partial(_hgrn_body, t=t, nb=nb, has_state=has_state, layer=layer),
                                in_specs, args, (y_into,))
    return pl.pallas_call(
        body,
        grid=(nseq // nb, nhb),
        in_specs=in_specs,
        out_specs=[pl.BlockSpec((rb, hw), lambda s, h: (s + r0, h)),
                   pl.BlockSpec((nb, H_HPS, H_EXPAND, H_VDIM), lambda s, h: (s, h, 0, 0))],
        out_shape=[jax.ShapeDtypeStruct((proj.shape[0], D_MODEL), BF16),
                   jax.ShapeDtypeStruct((nseq, H_HEADS, H_EXPAND, H_VDIM), F32)],
        input_output_aliases=aliases,
        compiler_params=_cparams(2),
        name="hgrn_sample" if has_state else "hgrn_prompt",
    )(*args)


def _hgrn_layer(layer, u, groups, state_hgrn, h_w_in, h_lb, h_norm_g, h_w_out):
    proj, w_out_bf = _mm(u, h_w_in, (0,), side=(h_w_out, (0,)), name="hgrn_in")
    y, sp = _hgrn(layer, proj, h_lb, h_norm_g, groups[0])
    y, ss = _hgrn(layer, proj, h_lb, h_norm_g, groups[1], state_hgrn, y_into=y)
    return (y, w_out_bf), sp, ss


def kernel(x_prompt, x_sample, state_ssm, state_conv, state_wkv, state_shift, state_hgrn, meta_tokens, norm_g, ffn_w_gate, ffn_w_up, ffn_w_down, m_w_in, m_conv_w, m_conv_b, m_dt_bias, m_a_log, m_d, m_norm_g, m_w_out, r_mu, r_w_r, r_w_k, r_w_v, r_w0, r_w1, r_w2, r_a0, r_a1, r_a2, r_g1, r_g2, r_k_k, r_k_a, r_r_k, r_ln_g, r_ln_b, r_w_o, h_w_in, h_lb, h_norm_g, h_w_out):
    pieces = [p for b in range(BATCH) for p in (meta_tokens, x_prompt[b])] + [x_sample.reshape(M_S, D_MODEL)]
    h = jnp.concatenate(pieces, axis=0)
    conv_pad = jnp.pad(state_conv, ((0, 0), (0, 0), (SUBLANES - (M_CONV - 1), 0), (0, 0)))

    p_conv, s_conv = [], []
    ssm = (None, None)
    u = _norm(h, norm_g, 0, 0)
    for l in range(DEPTH):
        kind, j = l % 3, l // 3
        act, w_down = _swiglu_up(u, ffn_w_gate, ffn_w_up, ffn_w_down, (l, 0))
        h, u = _down_resnorm(act, w_down, h, norm_g, 0.5, l, 1, l, 2, u_dtype=F32 if kind == 1 else BF16)
        if kind == 0:
            y, (sp, cp), (ss, cs) = _mamba_layer(j, u, GROUPS, m_w_in, m_conv_w, m_conv_b, m_dt_bias, m_a_log,
                                                 m_d, m_norm_g, m_w_out, conv_pad, state_ssm, ssm)
            ssm = (sp, ss)
            p_conv.append(cp)
            s_conv.append(cs)
        elif kind == 1:
            y, (p_wkv, p_shift), (s_wkv, s_shift) = _rwkv_layer(
                u, GROUPS, state_wkv, state_shift, r_mu, r_w_r, r_w_k, r_w_v, r_w0, r_w1, r_w2, r_a0, r_a1,
                r_a2, r_g1, r_g2, r_k_k, r_k_a, r_r_k, r_ln_g, r_ln_b, r_w_o)
        else:
            y, p_hgrn, s_hgrn = _hgrn_layer(l, u, GROUPS, state_hgrn, h_w_in, h_lb, h_norm_g, h_w_out)
        h, u = _down_resnorm(*y, h, norm_g, 1.0, l, 3, l, 4)
        act, w_down = _swiglu_up(u, ffn_w_gate, ffn_w_up, ffn_w_down, (l, 1))
        if l + 1 < DEPTH:
            h, u = _down_resnorm(act, w_down, h, norm_g, 0.5, l, 5, l + 1, 0)
        else:
            h, _ = _down_resnorm(act, w_down, h, norm_g, 0.5, l, 5)

    y_prompt = jnp.stack([h[b * T_P + N_META:(b + 1) * T_P] for b in range(BATCH)])
    y_sample = h[M_P:].reshape(DEC_BATCH, T_S, D_MODEL)
    return (y_prompt, y_sample,
            ssm[0], jnp.stack(p_conv), p_wkv[None], p_shift[None], p_hgrn[None],
            ssm[1], jnp.stack(s_conv), s_wkv[None], s_shift[None], s_hgrn[None])
```

```python
import functools

import jax
import jax.numpy as jnp
from jax import lax
from jax.experimental import pallas as pl
from jax.experimental.pallas import tpu as pltpu

F32 = jnp.float32
BF16 = jnp.bfloat16

D_MODEL = 2048
BATCH = 4
SEQ = 2048
DEPTH = 4
DEC_BATCH = 128
DEC_SEQ = 8
N_META = 16
EPS = 1e-6

M_D_INNER = 2 * D_MODEL
M_HEADDIM = 64
M_HEADS = M_D_INNER // M_HEADDIM
M_GROUPS = 8
M_D_STATE = 128
M_CONV = 4
M_CONV_DIM = M_D_INNER + 2 * M_GROUPS * M_D_STATE
M_NORM_EPS = 1e-5
M_GW = M_D_INNER // M_GROUPS
M_R = M_HEADS // M_GROUPS

R_HEADSIZE = 64
R_HEADS = D_MODEL // R_HEADSIZE
R_LN_EPS = 64e-5

H_HEADS = 16
H_EXPAND = 128
H_VDIM = D_MODEL // H_HEADS

T_P = N_META + SEQ
T_S = DEC_SEQ
M_P = BATCH * T_P
M_S = DEC_BATCH * T_S
M_TOK = M_P + M_S
S_BS = 8
LANES = 128
SUBLANES = 8
BF16_ROWS = 16
VMEM_LIMIT = 56 * 1024 * 1024
LOG2_E = 1.4426950408889634
GROUPS = ((BATCH, T_P, 0, 1), (DEC_BATCH, T_S, M_P, S_BS))


def _cparams(n_axes):
    return pltpu.CompilerParams(dimension_semantics=("arbitrary",) * n_axes,
                                vmem_limit_bytes=VMEM_LIMIT)


def _silu(x):
    h = 0.5 * x
    return h + h * jnp.tanh(h)


def _sigmoid(x):
    return 1.0 / (1.0 + jnp.exp(-x))


def _softplus(x):
    return jnp.maximum(x, 0.0) + jnp.log1p(jnp.exp(-jnp.abs(x)))


def _iota(shape, dim):
    return lax.broadcasted_iota(jnp.int32, shape, dim)


def _log2(n):
    assert n & (n - 1) == 0
    return n.bit_length() - 1


def _dot(a, b):
    return jnp.dot(a.astype(BF16), b.astype(BF16), preferred_element_type=F32)


def _dot_nt(a, b):
    return lax.dot_general(a.astype(BF16), b.astype(BF16), (((1,), (1,)), ((), ())),
                           preferred_element_type=F32)


def _dot_tn(a, b):
    return lax.dot_general(a.astype(BF16), b.astype(BF16), (((0,), (0,)), ((), ())),
                           preferred_element_type=F32)


def _split(x, terms):
    parts = []
    rem = x
    for _ in range(terms):
        hi = rem.astype(BF16)
        parts.append(hi)
        rem = rem - hi.astype(F32)
    return parts


def _xdot(x, sel, terms=3):
    return sum(jnp.dot(p, sel, preferred_element_type=F32) for p in _split(x, terms))


def _xdot_left(sel, x, terms=3):
    return sum(jnp.dot(sel, p, preferred_element_type=F32) for p in _split(x, terms))


def _xdot_nt(sel, x, terms=3):
    return sum(lax.dot_general(sel, p, (((1,), (1,)), ((), ())), preferred_element_type=F32)
               for p in _split(x, terms))


def _pad_rows(x, n):
    if x.shape[0] >= n:
        return x
    return jnp.concatenate([x, jnp.zeros((n - x.shape[0],) + x.shape[1:], x.dtype)], axis=0)


def _seq_masks(ntok, n):
    r, c = _iota((ntok, ntok), 0), _iota((ntok, ntok), 1)
    same = (r >> _log2(n)) == (c >> _log2(n))
    return same, same & (r >= c)


def _lockstep(gens):
    results = [None] * len(gens)
    live = list(enumerate(gens))
    while live:
        still = []
        for i, gen in live:
            try:
                next(gen)
                still.append((i, gen))
            except StopIteration as done:
                results[i] = done.value
        live = still
    return results


def _write_into(body, in_specs, args, targets):
    n_in = len(args)
    aliases = {}
    for k, arr in enumerate(targets):
        if arr is not None:
            aliases[len(args)] = k
            in_specs.append(pl.BlockSpec(memory_space=pl.ANY))
            args.append(arr)
    n_extra = len(args) - n_in

    def wrapped(*refs):
        return body(*refs[:n_in], *refs[n_in + n_extra:])

    return (wrapped if n_extra else body), aliases


def _row_tile(m, want):
    return next(d for d in range(want - want % BF16_ROWS, 0, -BF16_ROWS) if m % d == 0)


def _single(mode):
    return dict(pipeline_mode=pl.Buffered(1)) if mode else {}


SIDE_SLICES = 4


def _side_cast_specs(side, nj, ni):
    w, lead = side
    rows, d = w.shape[-2:]
    nsub = min(SIDE_SLICES, ni)
    sr = rows // (nj * nsub)
    assert rows % (nj * nsub) == 0 and sr % BF16_ROWS == 0

    def block(j, i):
        return j * nsub + jnp.minimum(i, nsub - 1)

    in_spec = pl.BlockSpec((None,) * len(lead) + (sr, d), lambda j, i: lead + (block(j, i), 0))
    out_spec = pl.BlockSpec((sr, d), lambda j, i: (block(j, i), 0))
    return in_spec, out_spec, jax.ShapeDtypeStruct((rows, d), BF16)


def _mm_body(x_ref, w_ref, *rest, act, valid_cols, w_t):
    if len(rest) == 4:
        side_ref, o_ref, side_bf_ref, wbf_ref = rest
    else:
        (o_ref, wbf_ref), side_ref = rest, None

    @pl.when(pl.program_id(1) == 0)
    def _():
        w = w_ref[...].T if w_t else w_ref[...]
        if valid_cols is not None:
            w = jnp.where(_iota(w.shape, 1) < valid_cols, w, 0.0)
        wbf_ref[...] = w.astype(BF16)

    acc = jnp.dot(x_ref[...].astype(BF16), wbf_ref[...], preferred_element_type=F32)
    if act == "tanh":
        acc = jnp.tanh(acc)
    elif act == "sigmoid":
        acc = _sigmoid(acc)
    o_ref[...] = acc.astype(o_ref.dtype)
    if side_ref is not None:
        side_bf_ref[...] = side_ref[...].astype(BF16)


def _mm(x, w, lead=(), col0=0, ncols=None, tn=512, tm=1856, out_dtype=F32, act=None, w_t=False, side=None,
        name="mm"):
    m, k = x.shape
    n_all = w.shape[-2] if w_t else w.shape[-1]
    ncols = n_all if ncols is None else ncols
    tn = min(tn, ncols)
    tm = _row_tile(m, tm)
    assert ncols % tn == 0 and col0 % tn == 0 and m % tm == 0 and w.shape[-1 if w_t else -2] == k
    valid_cols = None
    if col0 + ncols > n_all:
        assert ncols == tn
        valid_cols = n_all - col0
    jb = col0 // tn
    if w_t:
        wspec = pl.BlockSpec((None,) * len(lead) + (tn, k), lambda j, i: lead + (j + jb, 0))
    else:
        wspec = pl.BlockSpec((None,) * len(lead) + (k, tn), lambda j, i: lead + (0, j + jb))
    grid = (ncols // tn, m // tm)
    in_specs = [pl.BlockSpec((tm, k), lambda j, i: (i, 0)), wspec]
    out_specs = [pl.BlockSpec((tm, tn), lambda j, i: (i, j))]
    out_shape = [jax.ShapeDtypeStruct((m, ncols), out_dtype)]
    args = [x, w]
    if side is not None:
        side_in, side_out, side_shape = _side_cast_specs(side, *grid)
        in_specs.append(side_in)
        out_specs.append(side_out)
        out_shape.append(side_shape)
        args.append(side[0])
    res = pl.pallas_call(
        functools.partial(_mm_body, act=act, valid_cols=valid_cols, w_t=w_t),
        grid=grid,
        in_specs=in_specs,
        out_specs=out_specs,
        out_shape=out_shape,
        scratch_shapes=[pltpu.VMEM((k, tn), BF16)],
        compiler_params=_cparams(2),
        name=name,
    )(*args)
    return res[0] if side is None else res


def _swiglu_up_body(x_ref, wg_ref, wu_ref, wd_ref, o_ref, wd_bf_ref, wg_bf, wu_bf):
    @pl.when(pl.program_id(1) == 0)
    def _():
        wg_bf[...] = wg_ref[...].astype(BF16)
        wu_bf[...] = wu_ref[...].astype(BF16)

    x = x_ref[...]
    g = jnp.dot(x, wg_bf[...], preferred_element_type=F32)
    u = jnp.dot(x, wu_bf[...], preferred_element_type=F32)
    o_ref[...] = (g * _sigmoid(g) * u).astype(o_ref.dtype)
    wd_bf_ref[...] = wd_ref[...].astype(BF16)


def _swiglu_up(x, w_gate, w_up, w_down, lead, tn=512, tm=1856):
    m, k = x.shape
    n = w_gate.shape[-1]
    tm = _row_tile(m, tm)
    wspec = pl.BlockSpec((None,) * len(lead) + (k, tn), lambda j, i: lead + (0, j))
    grid = (n // tn, m // tm)
    side_in, side_out, side_shape = _side_cast_specs((w_down, lead), *grid)
    return pl.pallas_call(
        _swiglu_up_body,
        grid=grid,
        in_specs=[pl.BlockSpec((tm, k), lambda j, i: (i, 0)), wspec, wspec, side_in],
        out_specs=[pl.BlockSpec((tm, tn), lambda j, i: (i, j)), side_out],
        out_shape=[jax.ShapeDtypeStruct((m, n), BF16), side_shape],
        scratch_shapes=[pltpu.VMEM((k, tn), BF16), pltpu.VMEM((k, tn), BF16)],
        compiler_params=_cparams(2),
        name="swiglu_up",
    )(x, w_gate, w_up, w_down)


def _rms(x, g):
    return x * lax.rsqrt(jnp.mean(x * x, axis=-1, keepdims=True) + EPS) * g


def _norm_body(h_ref, g_ref, u_ref, *, gi):
    u_ref[...] = _rms(h_ref[...], g_ref[gi:gi + 1, :]).astype(u_ref.dtype)


def _norm(h, norm_g, layer, gi, out_dtype=BF16, tm=464):
    m, d = h.shape
    tm = _row_tile(m, tm)
    return pl.pallas_call(
        functools.partial(_norm_body, gi=gi),
        grid=(m // tm,),
        in_specs=[pl.BlockSpec((tm, d), lambda i: (i, 0)),
                  pl.BlockSpec((None, 6, d), lambda i: (layer, 0, 0))],
        out_specs=pl.BlockSpec((tm, d), lambda i: (i, 0)),
        out_shape=jax.ShapeDtypeStruct((m, d), out_dtype),
        compiler_params=_cparams(1),
        name="norm",
    )(h, norm_g)


def _down_resnorm_body(x_ref, w_ref, h_ref, ga_ref, gb_ref, hn_ref, *rest, scale, ia, ib):
    u_ref, y_scr = rest if len(rest) == 2 else (None, rest[0])
    j = pl.program_id(1)
    y_scr[j] = jnp.dot(x_ref[...], w_ref[...], preferred_element_type=F32)

    @pl.when(j == pl.num_programs(1) - 1)
    def _():
        y = jnp.concatenate([y_scr[c] for c in range(y_scr.shape[0])], axis=1)
        hn = h_ref[...] + scale * _rms(y, ga_ref[ia:ia + 1, :])
        hn_ref[...] = hn
        if u_ref is not None:
            u_ref[...] = _rms(hn, gb_ref[ib:ib + 1, :]).astype(u_ref.dtype)


RESIDENT_W_BYTES = 8 * 1024 * 1024


def _down_resnorm(x, w_bf, h, norm_g, scale, la, ia, lb=None, ib=None, u_dtype=BF16, tm=464):
    m, k = x.shape
    d = w_bf.shape[-1]
    tn = d if k * d * w_bf.dtype.itemsize <= RESIDENT_W_BYTES else 512
    tm = _row_tile(m, tm)
    row = pl.BlockSpec((tm, d), lambda i, j: (i, 0))
    with_u = lb is not None
    if not with_u:
        lb = la
    out_shape = [jax.ShapeDtypeStruct((m, d), F32)]
    out_specs = [row]
    if with_u:
        out_shape.append(jax.ShapeDtypeStruct((m, d), u_dtype))
        out_specs.append(row)
    res = pl.pallas_call(
        functools.partial(_down_resnorm_body, scale=scale, ia=ia, ib=ib),
        grid=(m // tm, d // tn),
        in_specs=[pl.BlockSpec((tm, k), lambda i, j: (i, 0)),
                  pl.BlockSpec((k, tn), lambda i, j: (0, j)),
                  row,
                  pl.BlockSpec((None, 6, d), lambda i, j: (la, 0, 0)),
                  pl.BlockSpec((None, 6, d), lambda i, j: (lb, 0, 0))],
        out_specs=out_specs,
        out_shape=out_shape,
        scratch_shapes=[pltpu.VMEM((d // tn, tm, tn), F32)],
        compiler_params=_cparams(2),
        name="down_resnorm",
    )(x, w_bf, h, norm_g, norm_g)
    return res if with_u else (res[0], None)


SSD_CHUNK = 128
SSD_GPS = 2


def _conv_taps(cur, shifted, w, b):
    out = b + cur * w[M_CONV - 1:M_CONV]
    for k in range(1, M_CONV):
        out = out + shifted(k) * w[M_CONV - 1 - k:M_CONV - k]
    return _silu(out)


def _conv_silu_seq(cur, prv, w, b):
    row = _iota((SUBLANES, cur.shape[1]), 0)

    def shifted(k):
        sh = pltpu.roll(cur, k, 0)
        top = jnp.where(row < k, pltpu.roll(prv, k, 0), sh[:SUBLANES])
        return jnp.concatenate([top, sh[SUBLANES:]], axis=0)

    return _conv_taps(cur, shifted, w, b)


def _conv_silu_batch(cur, stt, w, b):
    ntok = cur.shape[0]
    tpos = _iota((ntok, 1), 0) & (SUBLANES - 1)

    def shifted(k):
        return jnp.where(tpos < k, pltpu.roll(stt, (k - SUBLANES) % ntok, 0), pltpu.roll(cur, k, 0))

    return _conv_taps(cur, shifted, w, b)


def _expand_heads(rows):
    head_of_lane = _iota((M_R, M_GW), 1) >> _log2(M_HEADDIM)
    sel = (head_of_lane == _iota((M_R, M_GW), 0)).astype(BF16)
    return _xdot(rows, sel, terms=2)


def _ssd_common(x, bm, cm, dtr, hp, causal, same, nv, extra_rows):
    ntok = x.shape[0]
    bias, a = hp[0:1], -jnp.exp(hp[1:2])
    dt = _softplus(dtr + bias)
    if nv < ntok and same is None:
        dt = jnp.where(_iota((ntok, 1), 0) < nv, dt, 0.0)
    da = dt * a
    cum = _xdot_left(causal.astype(BF16), da)
    cum_tot = cum[ntok - 1:ntok] if same is None else _xdot_left(same.astype(BF16), da)
    cbm = _dot_nt(cm, bm)
    eye8 = (_iota((M_R, M_R), 0) == _iota((M_R, M_R), 1)).astype(BF16)
    dt_t = _xdot_nt(eye8, dt)
    even = (_iota((1, M_GW), 1) & M_HEADDIM) == 0
    x_e = jnp.where(even, x, 0.0).astype(BF16)
    x_o = jnp.where(even, 0.0, x).astype(BF16)
    yield
    cum2 = cum * LOG2_E
    cum2_t = _xdot_nt(eye8, cum2)
    tail = jnp.exp(cum_tot - cum) * dt
    ex = _expand_heads(jnp.concatenate([jnp.exp(cum), tail, extra_rows(cum_tot)], axis=0))
    yield
    parts = []
    for q in range(M_R // 2):
        acc = None
        for r, x_m in ((2 * q, x_e), (2 * q + 1, x_o)):
            dec = jnp.exp2(jnp.where(causal, cum2[:, r:r + 1] - cum2_t[r:r + 1, :], -jnp.inf))
            sc = (cbm * dec * dt_t[r:r + 1, :]).astype(BF16)
            yy = jnp.dot(sc, x_m[:, q * LANES:(q + 1) * LANES], preferred_element_type=F32)
            acc = yy if acc is None else acc + yy
        parts.append(acc)
    return jnp.concatenate(parts, axis=1), ex, cum_tot


def _ssd_finish(y, x, zc, d_row, ng):
    yv = (y + d_row * x) * _silu(zc)
    return yv * lax.rsqrt(jnp.mean(yv * yv, axis=-1, keepdims=True) + M_NORM_EPS) * ng


def _ssd_prompt_chunk(nv, zc, xr, br, cr, dtr, prev, st, cw, cb, hp, ng):
    n = SSD_CHUNK
    new_prev = (xr[nv - SUBLANES:], br[nv - SUBLANES:], cr[nv - SUBLANES:])
    zc, xr, br, cr, dtr = (_pad_rows(v, n) for v in (zc, xr, br, cr, dtr))
    x = _conv_silu_seq(xr, prev[0], cw[0], cb[0])
    bm = _conv_silu_seq(br, prev[1], cw[1], cb[1])
    cm = _conv_silu_seq(cr, prev[2], cw[2], cb[2])
    causal = _iota((n, n), 0) >= _iota((n, n), 1)
    row = _iota((SUBLANES, M_R), 0)

    def extra_rows(cum_tot):
        return jnp.where(row == 0, jnp.exp(cum_tot), jnp.where(row == 1, hp[2:3], 0.0))

    y_state = _dot(cm, st)
    y, ex, _ = yield from _ssd_common(x, bm, cm, dtr, hp, causal, None, nv, extra_rows)
    upd = _dot_tn(bm, ex[n:2 * n] * x)
    yield
    y = y + y_state * ex[:n]
    st = st * ex[2 * n:2 * n + 1] + upd
    return _ssd_finish(y, x, zc, ex[2 * n + 1:2 * n + 2], ng)[:nv], new_prev, st


def _ssd_sample_chunk(nb, zc, xr, br, cr, dtr, stt, states, cw, cb, hp, ng):
    n = SUBLANES
    ntok = nb * n
    x = _conv_silu_batch(xr, stt[0], cw[0], cb[0])
    bm = _conv_silu_batch(br, stt[1], cw[1], cb[1])
    cm = _conv_silu_batch(cr, stt[2], cw[2], cb[2])
    same, causal = _seq_masks(ntok, n)
    row = _iota((SUBLANES, M_R), 0)
    ys = [_dot_nt(_pad_rows(cm[b * n:(b + 1) * n], BF16_ROWS), states[b])[:n] for b in range(nb)]
    y, ex, cum_tot = yield from _ssd_common(x, bm, cm, dtr, hp, causal, same, n,
                                            lambda _: jnp.where(row == 0, hp[2:3], 0.0))
    xt = ex[ntok:2 * ntok] * x
    upds = [_dot_tn(_pad_rows(xt[b * n:(b + 1) * n], BF16_ROWS), _pad_rows(bm[b * n:(b + 1) * n], BF16_ROWS))
            for b in range(nb)]
    yield
    etot = jnp.exp(cum_tot)
    new_states = []
    for b in range(nb):
        decay = jnp.concatenate([jnp.broadcast_to(etot[b * n:b * n + 1, r:r + 1], (M_HEADDIM, M_D_STATE))
                                 for r in range(M_R)], axis=0)
        new_states.append(states[b] * decay + upds[b])
    y = y + jnp.concatenate(ys, axis=0) * ex[:ntok]
    return _ssd_finish(y, x, zc, ex[2 * ntok:2 * ntok + 1], ng), new_states


def _ssd_body(*refs, t, nb, has_state):
    if has_state:
        (z_ref, x_ref, b_ref, c_ref, dt_ref, cwx_ref, cwb_ref, cwc_ref, cbx_ref, cbb_ref, cbc_ref,
         hp_ref, ng_ref, csx_ref, csb_ref, csc_ref, sin_ref, y_ref, sout_ref) = refs
    else:
        (z_ref, x_ref, b_ref, c_ref, dt_ref, cwx_ref, cwb_ref, cwc_ref, cbx_ref, cbb_ref, cbc_ref,
         hp_ref, ng_ref, y_ref, sout_ref) = refs

    def consts(gi):
        wide = slice(gi * M_GW, (gi + 1) * M_GW)
        nar = slice(gi * M_D_STATE, (gi + 1) * M_D_STATE)
        cw = (cwx_ref[:, wide], cwb_ref[:, nar], cwc_ref[:, nar])
        cb = (cbx_ref[:, wide], cbb_ref[:, nar], cbc_ref[:, nar])
        return wide, nar, cw, cb, hp_ref[gi], ng_ref[:, wide]

    if has_state:
        ntok = nb * t
        gens = []
        for gi in range(SSD_GPS):
            wide, nar, cw, cb, hp, ng = consts(gi)
            stt = (csx_ref[:, :, wide].reshape(ntok, M_GW), csb_ref[:, :, nar].reshape(ntok, M_D_STATE),
                   csc_ref[:, :, nar].reshape(ntok, M_D_STATE))
            states = [sin_ref[b, gi * M_R:(gi + 1) * M_R].reshape(M_GW, M_D_STATE) for b in range(nb)]
            gens.append(_ssd_sample_chunk(nb, z_ref[:, wide], x_ref[:, wide], b_ref[:, nar], c_ref[:, nar],
                                          dt_ref[gi], stt, states, cw, cb, hp, ng))
        for gi, (out, states) in enumerate(_lockstep(gens)):
            y_ref[:, gi * M_GW:(gi + 1) * M_GW] = out.astype(y_ref.dtype)
            for b in range(nb):
                sout_ref[b, gi * M_R:(gi + 1) * M_R] = states[b].reshape(M_R, M_HEADDIM, M_D_STATE)
        return

    def run(start, nv, carry):
        rows = pl.ds(pl.multiple_of(start, SUBLANES), nv)
        gens = []
        for gi in range(SSD_GPS):
            wide, nar, cw, cb, hp, ng = consts(gi)
            prev, st = carry[gi]
            gens.append(_ssd_prompt_chunk(nv, z_ref[rows, wide], x_ref[rows, wide], b_ref[rows, nar],
                                          c_ref[rows, nar], dt_ref[gi, rows, :], prev, st, cw, cb, hp, ng))
        new = []
        for gi, (out, prev, st) in enumerate(_lockstep(gens)):
            y_ref[rows, gi * M_GW:(gi + 1) * M_GW] = out.astype(y_ref.dtype)
            new.append((prev, st))
        return tuple(new)

    zero_prev = (jnp.zeros((SUBLANES, M_GW), F32), jnp.zeros((SUBLANES, M_D_STATE), F32),
                 jnp.zeros((SUBLANES, M_D_STATE), F32))
    carry = tuple((zero_prev, jnp.zeros((M_D_STATE, M_GW), F32)) for _ in range(SSD_GPS))
    n_full = t // SSD_CHUNK
    carry = lax.fori_loop(0, n_full, lambda c, cr: run(c * SSD_CHUNK, SSD_CHUNK, cr), carry)
    if t % SSD_CHUNK:
        carry = run(n_full * SSD_CHUNK, t % SSD_CHUNK, carry)
    for gi in range(SSD_GPS):
        sout_ref[0, gi * M_R:(gi + 1) * M_R] = carry[gi][1].T.reshape(M_R, M_HEADDIM, M_D_STATE)


def _ssd(j, z, xbc, dtg, m_conv_w, conv_b3, hp, m_norm_g3, group, conv_state=None, ssm_state=None,
         y_into=None, state_into=None):
    nseq, t, row0, nb = group
    rb = nb * t
    r0 = row0 // rb
    assert row0 % rb == 0 and nseq % nb == 0 and M_GROUPS % SSD_GPS == 0
    has_state = ssm_state is not None
    gw, gn, gr = SSD_GPS * M_GW, SSD_GPS * M_D_STATE, SSD_GPS * M_R
    b0 = M_D_INNER // gn
    c0 = b0 + M_GROUPS // SSD_GPS
    one = _single(not has_state)
    in_specs = [
        pl.BlockSpec((rb, gw), lambda s, g: (s + r0, g)),
        pl.BlockSpec((rb, gw), lambda s, g: (s + r0, g)),
        pl.BlockSpec((rb, gn), lambda s, g: (s + r0, b0 + g), **one),
        pl.BlockSpec((rb, gn), lambda s, g: (s + r0, c0 + g), **one),
        pl.BlockSpec((SSD_GPS, rb, M_R), lambda s, g: (g, s + r0, 0), **one),
        pl.BlockSpec((None, M_CONV, gw), lambda s, g: (j, 0, g)),
        pl.BlockSpec((None, M_CONV, gn), lambda s, g: (j, 0, b0 + g)),
        pl.BlockSpec((None, M_CONV, gn), lambda s, g: (j, 0, c0 + g)),
        pl.BlockSpec((None, 1, gw), lambda s, g: (j, 0, g)),
        pl.BlockSpec((None, 1, gn), lambda s, g: (j, 0, b0 + g)),
        pl.BlockSpec((None, 1, gn), lambda s, g: (j, 0, c0 + g)),
        pl.BlockSpec((None, SSD_GPS, 3, M_R), lambda s, g: (j, g, 0, 0)),
        pl.BlockSpec((None, 1, gw), lambda s, g: (j, 0, g)),
    ]
    args = [z, xbc, xbc, xbc, dtg, m_conv_w, m_conv_w, m_conv_w, conv_b3, conv_b3, conv_b3, hp, m_norm_g3]
    if has_state:
        in_specs += [
            pl.BlockSpec((None, nb, SUBLANES, gw), lambda s, g: (j, s, 0, g)),
            pl.BlockSpec((None, nb, SUBLANES, gn), lambda s, g: (j, s, 0, b0 + g)),
            pl.BlockSpec((None, nb, SUBLANES, gn), lambda s, g: (j, s, 0, c0 + g)),
            pl.BlockSpec((None, nb, gr, M_HEADDIM, M_D_STATE), lambda s, g: (j, s, g, 0, 0)),
        ]
        args += [conv_state, conv_state, conv_state, ssm_state]
    body, aliases = _write_into(functools.partial(_ssd_body, t=t, nb=nb, has_state=has_state),
                                in_specs, args, (y_into, state_into))
    return pl.pallas_call(
        body,
        grid=(nseq // nb, M_GROUPS // SSD_GPS),
        in_specs=in_specs,
        out_specs=[pl.BlockSpec((rb, gw), lambda s, g: (s + r0, g)),
                   pl.BlockSpec((None, nb, gr, M_HEADDIM, M_D_STATE), lambda s, g: (j, s, g, 0, 0))],
        out_shape=[jax.ShapeDtypeStruct((z.shape[0], M_D_INNER), BF16),
                   jax.ShapeDtypeStruct((hp.shape[0], nseq, M_HEADS, M_HEADDIM, M_D_STATE), F32)],
        input_output_aliases=aliases,
        compiler_params=_cparams(2),
        name="ssd_sample" if has_state else "ssd_prompt",
    )(*args)


def _mamba_layer(j, u, groups, m_w_in, m_conv_w, m_conv_b, m_dt_bias, m_a_log, m_d, m_norm_g, m_w_out,
                 state_conv_pad, state_ssm, ssm_into=(None, None)):
    m = u.shape[0]
    gp, gs = groups
    w_in_t = jnp.swapaxes(m_w_in, 1, 2)
    z, w_out_bf = _mm(u, w_in_t, (j,), 0, M_D_INNER, w_t=True, side=(m_w_out, (j,)), name="ssd_in_z")
    xbc = _mm(u, w_in_t, (j,), M_D_INNER, M_CONV_DIM, w_t=True, name="ssd_in_xbc")
    dt = _mm(u, w_in_t, (j,), M_D_INNER + M_CONV_DIM, LANES, tn=LANES, w_t=True, name="ssd_in_dt")[:, :M_HEADS]
    dtg = dt.reshape(m, M_GROUPS, M_R).transpose(1, 0, 2)
    hp = jnp.stack([m_dt_bias, m_a_log, m_d], axis=1).reshape(-1, 3, M_GROUPS, M_R).transpose(0, 2, 1, 3)
    conv_b3 = m_conv_b[:, None, :]
    ng3 = m_norm_g[:, None, :]
    y, sp = _ssd(j, z, xbc, dtg, m_conv_w, conv_b3, hp, ng3, gp, state_into=ssm_into[0])
    y, ss = _ssd(j, z, xbc, dtg, m_conv_w, conv_b3, hp, ng3, gs, state_conv_pad, state_ssm,
                 y_into=y, state_into=ssm_into[1])
    keep = M_CONV - 1
    xp = jnp.stack([xbc[(b + 1) * gp[1] - keep:(b + 1) * gp[1]] for b in range(gp[0])])
    xs = xbc[gp[0] * gp[1]:].reshape(gs[0], gs[1], M_CONV_DIM)[:, gs[1] - keep:]
    return (y, w_out_bf), (sp, xp), (ss, xs)


R_TOK = 64
R_PAIRS = 8
R_SPLIT = 2


def _rwkv_chunk(nb, nv, toks, sbs, prm):
    n = R_TOK // nb
    ntok = R_TOK
    r, kr, v, wl, al, g = (_pad_rows(t_, ntok) for t_ in toks)
    w0, a0, k_k, k_a, r_k, ln_g, ln_b = prm
    m0 = _iota((1, LANES), 1) < R_HEADSIZE
    ones_blk = ((_iota((LANES, LANES), 0) < R_HEADSIZE) == (_iota((LANES, LANES), 1) < R_HEADSIZE)).astype(BF16)

    w = -_softplus(-(w0 + wl)) - 0.5
    lw = -jnp.exp(w)
    a = _sigmoid(a0 + al)
    kk = kr * k_k
    k = kr * (1.0 + (a - 1.0) * k_a)
    if nv < n:
        valid = _iota((ntok, 1), 0) < nv
        lw = jnp.where(valid, lw, 0.0)
        kk = jnp.where(valid, kk, 0.0)
        k = jnp.where(valid, k, 0.0)
        v = jnp.where(valid, v, 0.0)
    same, causal = _seq_masks(ntok, n)
    head_sums = _xdot(jnp.concatenate([kk * kk, r * k * r_k], axis=0), ones_blk, terms=2)
    lam = _xdot_left(causal.astype(BF16), lw)
    lam_tot = lam[ntok - 1:ntok] if nb == 1 else _xdot_left(same.astype(BF16), lw)
    yield
    bonus = head_sums[ntok:] * v
    kk = kk / jnp.maximum(jnp.sqrt(head_sums[:ntok]), 1e-12)
    beta = kk * a
    e_in = jnp.exp(lam)
    e_out = jnp.exp(-lam)
    e_end = jnp.exp(lam_tot - lam)

    def stack(x):
        x3 = x.reshape(nb, n, LANES)
        s = jnp.concatenate([jnp.where(m0, x3, 0.0), jnp.where(m0, 0.0, x3)], axis=1)
        return s.reshape(2 * ntok, LANES).astype(BF16)

    am, rm = stack(kk * jnp.exp(lam - lw)), stack(r * e_in)
    khm, bhm = stack(k * e_out), stack(beta * e_out)
    vst = stack(v)
    n2 = 2 * ntok
    rr, cc = _iota((n2, n2), 0), _iota((n2, n2), 1)
    same_blk = (rr >> _log2(n)) == (cc >> _log2(n))
    strict = same_blk & ((rr & (n - 1)) > (cc & (n - 1)))
    incl = same_blk & ((rr & (n - 1)) >= (cc & (n - 1)))
    qm = jnp.concatenate([am, rm], axis=0)
    pmat = _dot_nt(qm, jnp.concatenate([khm, bhm], axis=0))
    w2 = 2 * n
    if nb == 1:
        qs = [_dot_nt(qm, sbs[0])]
    else:
        qs = [_dot_nt(jnp.concatenate([am[b * w2:(b + 1) * w2], rm[b * w2:(b + 1) * w2]], axis=0), sbs[b])
              for b in range(nb)]
    ktm, btm = stack(k * e_end), stack(beta * e_end)
    yield
    a_ak = jnp.where(strict, pmat[:n2, :n2], 0.0)
    nmat = jnp.where(strict, -pmat[:n2, n2:], 0.0)
    a_rk = jnp.where(incl, pmat[n2:, :n2], 0.0)
    a_rb = jnp.where(incl, pmat[n2:, n2:], 0.0)
    rhs_s = qs[0][:w2] if nb == 1 else jnp.concatenate([q[:w2] for q in qs], axis=0)
    ys = qs[0][w2:] if nb == 1 else jnp.concatenate([q[w2:] for q in qs], axis=0)

    tm = (rr == cc).astype(F32) + nmat
    npow = _dot(nmat, nmat)
    av = _dot(jnp.concatenate([a_ak, a_rk], axis=0), vst)
    yield
    rhs = rhs_s + av[:n2]
    ys = ys + av[n2:]
    for i in range(_log2(n) - 1):
        if i + 2 < _log2(n):
            both = _dot(jnp.concatenate([tm, npow], axis=0), npow)
            prod, npow = both[:n2], both[n2:]
        else:
            prod = _dot(tm, npow)
        yield
        tm = tm + prod
    u = _dot(tm, rhs)
    yield
    yst = (ys - _dot(a_rb, u)).reshape(nb, w2, LANES)
    nu = (-u).astype(BF16)
    new_sbs = []
    for b in range(nb):
        rows = slice(b * w2, (b + 1) * w2)
        lt = lam_tot if nb == 1 else lam_tot[b * n:b * n + 1]
        new_sbs.append(sbs[b] * jnp.exp(lt)
                       + _dot_tn(jnp.concatenate([vst[rows], nu[rows]], axis=0),
                                 jnp.concatenate([ktm[rows], btm[rows]], axis=0)))
    yield
    y = (yst[:, :n] + yst[:, n:]).reshape(ntok, LANES)
    inv = 1.0 / R_HEADSIZE
    mean = _xdot(y, ones_blk, terms=2) * inv
    yield
    yc = y - mean
    var = _xdot(yc * yc, ones_blk, terms=2) * inv
    yield
    yn = yc * lax.rsqrt(var + R_LN_EPS) * ln_g + ln_b
    return ((yn + bonus) * g)[:nb * nv], new_sbs


def _rwkv_body(*refs, t, nb, has_state):
    if has_state:
        (r_ref, k_ref, v_ref, wl_ref, al_ref, g_ref, prm_ref, sin_ref, y_ref, sout_ref) = refs
    else:
        (r_ref, k_ref, v_ref, wl_ref, al_ref, g_ref, prm_ref, y_ref, sout_ref, st_scr) = refs
    tok_refs = (r_ref, k_ref, v_ref, wl_ref, al_ref, g_ref)
    zeros = jnp.zeros((R_HEADSIZE, R_HEADSIZE), F32)

    def lanes(p):
        return slice(p * LANES, (p + 1) * LANES)

    def run(rows, nv, states):
        res = _lockstep([
            _rwkv_chunk(nb, nv, tuple(ref[rows, lanes(p)] for ref in tok_refs), states[p],
                        tuple(prm_ref[i:i + 1, lanes(p)] for i in range(7)))
            for p in range(R_PAIRS)])
        for p, (out, _) in enumerate(res):
            y_ref[rows, lanes(p)] = out.astype(y_ref.dtype)
        return tuple(tuple(sbs) for _, sbs in res)

    if has_state:
        states = tuple(
            tuple(jnp.concatenate([jnp.concatenate([sin_ref[b, 2 * p], zeros], axis=1),
                                   jnp.concatenate([zeros, sin_ref[b, 2 * p + 1]], axis=1)], axis=0)
                  for b in range(nb))
            for p in range(R_PAIRS))
    else:
        @pl.when(pl.program_id(2) == 0)
        def _():
            st_scr[...] = jnp.zeros(st_scr.shape, F32)

        states = tuple((st_scr[p],) for p in range(R_PAIRS))

    if nb == 1:
        n_full = t // R_TOK
        states = lax.fori_loop(
            0, n_full, lambda c, st: run(pl.ds(pl.multiple_of(c * R_TOK, R_TOK), R_TOK), R_TOK, st), states)
        if t % R_TOK:
            states = run(pl.ds(n_full * R_TOK, t % R_TOK), t % R_TOK, states)
    else:
        assert nb * t == R_TOK
        states = run(slice(None), t, states)
    for p in range(R_PAIRS):
        if not has_state:
            st_scr[p] = states[p][0]
        for b in range(nb):
            sout_ref[b, 2 * p] = states[p][b][:R_HEADSIZE, :R_HEADSIZE]
            sout_ref[b, 2 * p + 1] = states[p][b][R_HEADSIZE:, R_HEADSIZE:]


def _rwkv(r, k, v, wl, al, g, prm, group, wkv_state=None, y_into=None):
    nseq, t, row0, nb = group
    has_state = wkv_state is not None
    split = 1 if has_state else R_SPLIT
    tb = t // split
    rb = nb * tb
    r0 = row0 // rb
    assert t % split == 0 and tb % SUBLANES == 0 and row0 % rb == 0 and nseq % nb == 0
    gl = R_PAIRS * LANES

    def rows(s, hp_, tt):
        return (s * split + tt + r0, hp_)

    tok = pl.BlockSpec((rb, gl), rows)
    tok1 = pl.BlockSpec((rb, gl), rows, **_single(not has_state))
    in_specs = [tok] * 4 + [tok1] * 2 + [pl.BlockSpec((SUBLANES, gl), lambda s, hp_, tt: (0, hp_))]
    args = [r, k, v, wl, al, g, prm]
    if has_state:
        in_specs.append(pl.BlockSpec((None, nb, 2 * R_PAIRS, R_HEADSIZE, R_HEADSIZE),
                                     lambda s, hp_, tt: (0, s, hp_, 0, 0)))
        args.append(wkv_state)
    body, aliases = _write_into(functools.partial(_rwkv_body, t=tb, nb=nb, has_state=has_state),
                                in_specs, args, (y_into,))
    return pl.pallas_call(
        body,
        grid=(nseq // nb, R_HEADS // (2 * R_PAIRS), split),
        in_specs=in_specs,
        out_specs=[pl.BlockSpec((rb, gl), rows),
                   pl.BlockSpec((nb, 2 * R_PAIRS, R_HEADSIZE, R_HEADSIZE), lambda s, hp_, tt: (s, hp_, 0, 0))],
        out_shape=[jax.ShapeDtypeStruct((r.shape[0], D_MODEL), BF16),
                   jax.ShapeDtypeStruct((nseq, R_HEADS, R_HEADSIZE, R_HEADSIZE), F32)],
        scratch_shapes=[] if has_state else [pltpu.VMEM((R_PAIRS, LANES, LANES), F32)],
        input_output_aliases=aliases,
        compiler_params=_cparams(3),
        name="rwkv_sample" if has_state else "rwkv_prompt",
    )(*args)


MIX_COLS = 512


def _mix_body(*refs, t, nb, has_state):
    if has_state:
        u_ref, mu_ref, sh_ref = refs[:3]
    else:
        u_ref, mu_ref = refs[:2]
    o_refs = refs[-6:]
    u = u_ref[...]
    rows = nb * t
    tpos = _iota((rows, 1), 0) & (t - 1) if nb > 1 else _iota((rows, 1), 0)
    prev = pltpu.roll(u, 1, 0)
    if has_state:
        pick = (_iota((rows, BF16_ROWS), 0) >> _log2(t)) == _iota((rows, BF16_ROWS), 1)
        first = _xdot_left(pick.astype(BF16), _pad_rows(sh_ref[...], BF16_ROWS))
    else:
        first = 0.0
    dx = jnp.where(tpos == 0, first, prev) - u
    for i, o_ref in enumerate(o_refs):
        o_ref[...] = (u + dx * mu_ref[i:i + 1, :]).astype(o_ref.dtype)


def _rwkv_mix(u, r_mu, group, shift_state=None, into=None):
    nseq, t, row0, nb = group
    m, d = u.shape
    rb = nb * t
    r0 = row0 // rb
    has_state = shift_state is not None
    assert row0 % rb == 0 and nseq % nb == 0 and (nb == 1 or nb <= BF16_ROWS)
    blk = pl.BlockSpec((rb, MIX_COLS), lambda s, c: (s + r0, c))
    in_specs = [blk, pl.BlockSpec((None, 6, MIX_COLS), lambda s, c: (0, 0, c))]
    args = [u, r_mu]
    if has_state:
        in_specs.append(pl.BlockSpec((None, nb, MIX_COLS), lambda s, c: (0, s, c)))
        args.append(shift_state)
    body, aliases = _write_into(functools.partial(_mix_body, t=t, nb=nb, has_state=has_state),
                                in_specs, args, into or (None,) * 6)
    return pl.pallas_call(
        body,
        grid=(nseq // nb, d // MIX_COLS),
        in_specs=in_specs,
        out_specs=[blk] * 6,
        out_shape=[jax.ShapeDtypeStruct((m, d), BF16)] * 6,
        input_output_aliases=aliases,
        compiler_params=_cparams(2),
        name="rwkv_mix_sample" if has_state else "rwkv_mix_prompt",
    )(*args)


def _rwkv_layer(u, groups, state_wkv, state_shift, r_mu, r_w_r, r_w_k, r_w_v, r_w0, r_w1, r_w2, r_a0, r_a1,
                r_a2, r_g1, r_g2, r_k_k, r_k_a, r_r_k, r_ln_g, r_ln_b, r_w_o):
    gp, gs = groups
    mixed = _rwkv_mix(u, r_mu, gp)
    xr, xw, xk, xv, xa, xg = _rwkv_mix(u, r_mu, gs, state_shift, into=mixed)
    r, w_o_bf = _mm(xr, r_w_r, (0,), side=(r_w_o, (0,)), name="rwkv_r")
    k = _mm(xk, r_w_k, (0,), name="rwkv_k")
    v = _mm(xv, r_w_v, (0,), name="rwkv_v")
    wl = _mm(_mm(xw, r_w1, (0,), out_dtype=BF16, act="tanh", name="rwkv_w1"), r_w2, (0,), name="rwkv_w2")
    al = _mm(_mm(xa, r_a1, (0,), out_dtype=BF16, name="rwkv_a1"), r_a2, (0,), name="rwkv_a2")
    g = _mm(_mm(xg, r_g1, (0,), out_dtype=BF16, act="sigmoid", name="rwkv_g1"), r_g2, (0,), name="rwkv_g2")
    prm = jnp.concatenate([r_w0, r_a0, r_k_k, r_k_a, r_r_k.reshape(1, D_MODEL), r_ln_g, r_ln_b,
                           jnp.zeros((1, D_MODEL), F32)], axis=0)
    y, sp = _rwkv(r, k, v, wl, al, g, prm, gp)
    y, ss = _rwkv(r, k, v, wl, al, g, prm, gs, state_wkv, y_into=y)
    last_p = jnp.stack([u[(b + 1) * gp[1] - 1] for b in range(gp[0])])
    last_s = u[gp[0] * gp[1]:].reshape(gs[0], gs[1], D_MODEL)[:, -1]
    return (y, w_o_bf), (sp, last_p), (ss, last_s)


H_CHUNK = 128
H_SUB = 16
H_HPS = 4


def _gla_diag(q3, g3, k3, v3):
    sub = q3.shape[1]
    pieces = [slice(p, p + SUBLANES) for p in range(0, sub, SUBLANES)]
    tpos = _iota((1, SUBLANES, 1), 1)
    g3 = g3 * LOG2_E
    ys = [jnp.zeros(v3[:, p].shape, F32) for p in pieces]
    for s in range(sub):
        terms = []
        for i, p in enumerate(pieces):
            if p.stop <= s:
                continue
            arg = g3[:, p] - g3[:, s:s + 1, :]
            if p.start < s:
                arg = jnp.where(tpos + p.start >= s, arg, -jnp.inf)
            terms.append((i, jnp.sum(q3[:, p] * k3[:, s:s + 1, :] * jnp.exp2(arg), axis=-1, keepdims=True)))
        yield
        for i, a_s in terms:
            ys[i] = ys[i] + a_s * v3[:, s:s + 1, :]
    return ys[0] if len(ys) == 1 else jnp.concatenate(ys, axis=1)


def _gla_prompt_chunk(nv, q, logf, k, v, st):
    n = -(-nv // H_SUB) * H_SUB
    q, logf, k, v = (_pad_rows(t_, n) for t_ in (q, logf, k, v))
    causal = _iota((n, n), 0) >= _iota((n, n), 1)
    gcum = _xdot_left(causal.astype(BF16), logf)
    yield
    y = _dot_nt(q * jnp.exp(gcum), st)
    g_n = gcum[n - 1:n]
    upd = _dot_tn(v, k * jnp.exp(g_n - gcum))
    atts = []
    for i in range(1, n // H_SUB):
        lo = i * H_SUB
        gs = gcum[lo - 1:lo]
        atts.append(_dot_nt(q[lo:lo + H_SUB] * jnp.exp(gcum[lo:lo + H_SUB] - gs),
                            k[:lo] * jnp.exp(gs - gcum[:lo])))
    yield
    st = st * jnp.exp(g_n) + upd
    offs = [jnp.zeros((H_SUB, LANES), F32)] + [_dot(att, v[:(i + 1) * H_SUB]) for i, att in enumerate(atts)]
    shape3 = (n // H_SUB, H_SUB, LANES)
    y3 = yield from _gla_diag(q.reshape(shape3), gcum.reshape(shape3), k.reshape(shape3), v.reshape(shape3))
    off = offs[0] if len(offs) == 1 else jnp.concatenate(offs, axis=0)
    return (y + off + y3.reshape(n, LANES))[:nv], st


def _gla_sample_chunk(nb, q, logf, k, v, sts):
    n = SUBLANES
    ntok = nb * n
    _, causal = _seq_masks(ntok, n)
    gcum = _xdot_left(causal.astype(BF16), logf)
    yield
    shape3 = (nb, n, LANES)
    g3 = gcum.reshape(shape3)
    g_tot = g3[:, n - 1:n, :]
    qe = q * jnp.exp(gcum)
    kd = (k.reshape(shape3) * jnp.exp(g_tot - g3)).reshape(ntok, LANES)
    ys = [_dot_nt(_pad_rows(qe[b * n:(b + 1) * n], BF16_ROWS), sts[b])[:n] for b in range(nb)]
    upds = [_dot_tn(_pad_rows(v[b * n:(b + 1) * n], BF16_ROWS), _pad_rows(kd[b * n:(b + 1) * n], BF16_ROWS))
            for b in range(nb)]
    y3 = yield from _gla_diag(q.reshape(shape3), g3, k.reshape(shape3), v.reshape(shape3))
    new = [sts[b] * jnp.exp(g_tot[b]) + upds[b] for b in range(nb)]
    return y3.reshape(ntok, LANES) + jnp.concatenate(ys, axis=0), new


def _hgrn_body(*refs, t, nb, has_state, layer):
    if has_state:
        (q_ref, f_ref, i_ref, g_ref, lb_ref, ng_ref, sin_ref, y_ref, sout_ref) = refs
    else:
        (q_ref, f_ref, i_ref, g_ref, lb_ref, ng_ref, y_ref, sout_ref) = refs
    hl = lb_ref[...]
    e = jnp.exp(hl - jnp.max(hl, axis=0, keepdims=True))
    sm = e / jnp.sum(e, axis=0, keepdims=True)
    lb_all = (jnp.sum(sm[1:layer + 1], axis=0, keepdims=True) if layer > 0
              else jnp.zeros((1, H_HPS * LANES), F32))
    ng = ng_ref[...]

    def lanes(hi):
        return slice(hi * LANES, (hi + 1) * LANES)

    def gates(rows, hi):
        lb = lb_all[:, lanes(hi)]
        q = _silu(q_ref[rows, lanes(hi)])
        f = lb + (1.0 - lb) * _sigmoid(f_ref[rows, lanes(hi)])
        return q, jnp.log(f), 1.0 - f, i_ref[rows, lanes(hi)]

    def finish(rows, hi, yy):
        yy = yy * lax.rsqrt(jnp.mean(yy * yy, axis=-1, keepdims=True) + EPS) * ng
        y_ref[rows, lanes(hi)] = (yy * _silu(g_ref[rows, lanes(hi)])).astype(y_ref.dtype)

    if has_state:
        rows = slice(None)
        res = _lockstep([_gla_sample_chunk(nb, *gates(rows, hi), [sin_ref[b, hi].T for b in range(nb)])
                         for hi in range(H_HPS)])
        for hi, (yy, sts) in enumerate(res):
            finish(rows, hi, yy)
            for b in range(nb):
                sout_ref[b, hi] = sts[b].T
        return

    def run(start, nv, sts):
        rows = pl.ds(pl.multiple_of(start, SUBLANES), nv)
        res = _lockstep([_gla_prompt_chunk(nv, *gates(rows, hi), sts[hi]) for hi in range(H_HPS)])
        for hi, (yy, _) in enumerate(res):
            finish(rows, hi, yy)
        return tuple(st for _, st in res)

    sts = tuple(jnp.zeros((H_VDIM, H_EXPAND), F32) for _ in range(H_HPS))
    n_full = t // H_CHUNK
    sts = lax.fori_loop(0, n_full, lambda c, s: run(c * H_CHUNK, H_CHUNK, s), sts)
    if t % H_CHUNK:
        sts = run(n_full * H_CHUNK, t % H_CHUNK, sts)
    for hi in range(H_HPS):
        sout_ref[0, hi] = sts[hi].T


def _hgrn(layer, proj, h_lb, h_norm_g, group, state=None, y_into=None):
    nseq, t, row0, nb = group
    rb = nb * t
    r0 = row0 // rb
    assert row0 % rb == 0 and nseq % nb == 0
    has_state = state is not None
    nhb = H_HEADS // H_HPS
    hw = H_HPS * LANES

    def tok(section):
        return pl.BlockSpec((rb, hw), lambda s, h: (s + r0, h + section * nhb))

    in_specs = [tok(0), tok(1), tok(2), tok(3),
                pl.BlockSpec((DEPTH, hw), lambda s, h: (0, h)),
                pl.BlockSpec((1, LANES), lambda s, h: (0, 0))]
    args = [proj, proj, proj, proj, h_lb, h_norm_g]
    if has_state:
        in_specs.append(pl.BlockSpec((None, nb, H_HPS, H_EXPAND, H_VDIM), lambda s, h: (0, s, h, 0, 0)))
        args.append(state)
    body, aliases = _write_into(functools.partial(_hgrn_body, t=t, nb=nb, has_state=has_state, layer=layer),
                                in_specs, args, (y_into,))
    return pl.pallas_call(
        body,
        grid=(nseq // nb, nhb),
        in_specs=in_specs,
        out_specs=[pl.BlockSpec((rb, hw), lambda s, h: (s + r0, h)),
                   pl.BlockSpec((nb, H_HPS, H_EXPAND, H_VDIM), lambda s, h: (s, h, 0, 0))],
        out_shape=[jax.ShapeDtypeStruct((proj.shape[0], D_MODEL), BF16),
                   jax.ShapeDtypeStruct((nseq, H_HEADS, H_EXPAND, H_VDIM), F32)],
        input_output_aliases=aliases,
        compiler_params=_cparams(2),
        name="hgrn_sample" if has_state else "hgrn_prompt",
    )(*args)


def _hgrn_layer(layer, u, groups, state_hgrn, h_w_in, h_lb, h_norm_g, h_w_out):
    proj, w_out_bf = _mm(u, h_w_in, (0,), side=(h_w_out, (0,)), name="hgrn_in")
    y, sp = _hgrn(layer, proj, h_lb, h_norm_g, groups[0])
    y, ss = _hgrn(layer, proj, h_lb, h_norm_g, groups[1], state_hgrn, y_into=y)
    return (y, w_out_bf), sp, ss


def kernel(x_prompt, x_sample, state_ssm, state_conv, state_wkv, state_shift, state_hgrn, meta_tokens, norm_g, ffn_w_gate, ffn_w_up, ffn_w_down, m_w_in, m_conv_w, m_conv_b, m_dt_bias, m_a_log, m_d, m_norm_g, m_w_out, r_mu, r_w_r, r_w_k, r_w_v, r_w0, r_w1, r_w2, r_a0, r_a1, r_a2, r_g1, r_g2, r_k_k, r_k_a, r_r_k, r_ln_g, r_ln_b, r_w_o, h_w_in, h_lb, h_norm_g, h_w_out):
    pieces = [p for b in range(BATCH) for p in (meta_tokens, x_prompt[b])] + [x_sample.reshape(M_S, D_MODEL)]
    h = jnp.concatenate(pieces, axis=0)
    conv_pad = jnp.pad(state_conv, ((0, 0), (0, 0), (SUBLANES - (M_CONV - 1), 0), (0, 0)))

    p_conv, s_conv = [], []
    ssm = (None, None)
    u = _norm(h, norm_g, 0, 0)
    for l in range(DEPTH):
        kind, j = l % 3, l // 3
        act, w_down = _swiglu_up(u, ffn_w_gate, ffn_w_up, ffn_w_down, (l, 0))
        h, u = _down_resnorm(act, w_down, h, norm_g, 0.5, l, 1, l, 2, u_dtype=F32 if kind == 1 else BF16)
        if kind == 0:
            y, (sp, cp), (ss, cs) = _mamba_layer(j, u, GROUPS, m_w_in, m_conv_w, m_conv_b, m_dt_bias, m_a_log,
                                                 m_d, m_norm_g, m_w_out, conv_pad, state_ssm, ssm)
            ssm = (sp, ss)
            p_conv.append(cp)
            s_conv.append(cs)
        elif kind == 1:
            y, (p_wkv, p_shift), (s_wkv, s_shift) = _rwkv_layer(
                u, GROUPS, state_wkv, state_shift, r_mu, r_w_r, r_w_k, r_w_v, r_w0, r_w1, r_w2, r_a0, r_a1,
                r_a2, r_g1, r_g2, r_k_k, r_k_a, r_r_k, r_ln_g, r_ln_b, r_w_o)
        else:
            y, p_hgrn, s_hgrn = _hgrn_layer(l, u, GROUPS, state_hgrn, h_w_in, h_lb, h_norm_g, h_w_out)
        h, u = _down_resnorm(*y, h, norm_g, 1.0, l, 3, l, 4)
        act, w_down = _swiglu_up(u, ffn_w_gate, ffn_w_up, ffn_w_down, (l, 1))
        if l + 1 < DEPTH:
            h, u = _down_resnorm(act, w_down, h, norm_g, 0.5, l, 5, l + 1, 0)
        else:
            h, _ = _down_resnorm(act, w_down, h, norm_g, 0.5, l, 5)

    y_prompt = jnp.stack([h[b * T_P + N_META:(b + 1) * T_P] for b in range(BATCH)])
    y_sample = h[M_P:].reshape(DEC_BATCH, T_S, D_MODEL)
    return (y_prompt, y_sample,
            ssm[0], jnp.stack(p_conv), p_wkv[None], p_shift[None], p_hgrn[None],
            ssm[1], jnp.stack(s_conv), s_wkv[None], s_shift[None], s_hgrn[None])
```

```python
import functools

import jax
import jax.numpy as jnp
from jax import lax
from jax.experimental import pallas as pl
from jax.experimental.pallas import tpu as pltpu

F32 = jnp.float32
BF16 = jnp.bfloat16

D_MODEL = 2048
BATCH = 4
SEQ = 2048
DEPTH = 4
DEC_BATCH = 128
DEC_SEQ = 8
N_META = 16
EPS = 1e-6

M_D_INNER = 2 * D_MODEL
M_HEADDIM = 64
M_HEADS = M_D_INNER // M_HEADDIM
M_GROUPS = 8
M_D_STATE = 128
M_CONV = 4
M_CONV_DIM = M_D_INNER + 2 * M_GROUPS * M_D_STATE
M_NORM_EPS = 1e-5
M_GW = M_D_INNER // M_GROUPS
M_R = M_HEADS // M_GROUPS

R_HEADSIZE = 64
R_HEADS = D_MODEL // R_HEADSIZE
R_LN_EPS = 64e-5

H_HEADS = 16
H_EXPAND = 128
H_VDIM = D_MODEL // H_HEADS

T_P = N_META + SEQ
T_S = DEC_SEQ
M_P = BATCH * T_P
M_S = DEC_BATCH * T_S
M_TOK = M_P + M_S
S_BS = 8
LANES = 128
SUBLANES = 8
BF16_ROWS = 16
VMEM_LIMIT = 56 * 1024 * 1024
LOG2_E = 1.4426950408889634
GROUPS = ((BATCH, T_P, 0, 1), (DEC_BATCH, T_S, M_P, S_BS))


def _cparams(n_axes):
    return pltpu.CompilerParams(dimension_semantics=("arbitrary",) * n_axes,
                                vmem_limit_bytes=VMEM_LIMIT)


def _silu(x):
    h = 0.5 * x
    return h + h * jnp.tanh(h)


def _sigmoid(x):
    return 1.0 / (1.0 + jnp.exp(-x))


def _softplus(x):
    return jnp.maximum(x, 0.0) + jnp.log1p(jnp.exp(-jnp.abs(x)))


def _iota(shape, dim):
    return lax.broadcasted_iota(jnp.int32, shape, dim)


def _log2(n):
    assert n & (n - 1) == 0
    return n.bit_length() - 1


def _dot(a, b):
    return jnp.dot(a.astype(BF16), b.astype(BF16), preferred_element_type=F32)


def _dot_nt(a, b):
    return lax.dot_general(a.astype(BF16), b.astype(BF16), (((1,), (1,)), ((), ())),
                           preferred_element_type=F32)


def _dot_tn(a, b):
    return lax.dot_general(a.astype(BF16), b.astype(BF16), (((0,), (0,)), ((), ())),
                           preferred_element_type=F32)


def _split(x, terms):
    parts = []
    rem = x
    for _ in range(terms):
        hi = rem.astype(BF16)
        parts.append(hi)
        rem = rem - hi.astype(F32)
    return parts


def _xdot(x, sel, terms=3):
    return sum(jnp.dot(p, sel, preferred_element_type=F32) for p in _split(x, terms))


def _xdot_left(sel, x, terms=3):
    return sum(jnp.dot(sel, p, preferred_element_type=F32) for p in _split(x, terms))


def _xdot_nt(sel, x, terms=3):
    return sum(lax.dot_general(sel, p, (((1,), (1,)), ((), ())), preferred_element_type=F32)
               for p in _split(x, terms))


def _pad_rows(x, n):
    if x.shape[0] >= n:
        return x
    return jnp.concatenate([x, jnp.zeros((n - x.shape[0],) + x.shape[1:], x.dtype)], axis=0)


def _seq_masks(ntok, n):
    r, c = _iota((ntok, ntok), 0), _iota((ntok, ntok), 1)
    same = (r >> _log2(n)) == (c >> _log2(n))
    return same, same & (r >= c)


def _lockstep(gens):
    results = [None] * len(gens)
    live = list(enumerate(gens))
    while live:
        still = []
        for i, gen in live:
            try:
                next(gen)
                still.append((i, gen))
            except StopIteration as done:
                results[i] = done.value
        live = still
    return results


def _write_into(body, in_specs, args, targets):
    n_in = len(args)
    aliases = {}
    for k, arr in enumerate(targets):
        if arr is not None:
            aliases[len(args)] = k
            in_specs.append(pl.BlockSpec(memory_space=pl.ANY))
            args.append(arr)
    n_extra = len(args) - n_in

    def wrapped(*refs):
        return body(*refs[:n_in], *refs[n_in + n_extra:])

    return (wrapped if n_extra else body), aliases


def _row_tile(m, want):
    return next(d for d in range(want - want % BF16_ROWS, 0, -BF16_ROWS) if m % d == 0)


def _single(mode):
    return dict(pipeline_mode=pl.Buffered(1)) if mode else {}


SIDE_SLICES = 4


def _side_cast_specs(side, nj, ni):
    w, lead = side
    rows, d = w.shape[-2:]
    nsub = min(SIDE_SLICES, ni)
    sr = rows // (nj * nsub)
    assert rows % (nj * nsub) == 0 and sr % BF16_ROWS == 0

    def block(j, i):
        return j * nsub + jnp.minimum(i, nsub - 1)

    in_spec = pl.BlockSpec((None,) * len(lead) + (sr, d), lambda j, i: lead + (block(j, i), 0))
    out_spec = pl.BlockSpec((sr, d), lambda j, i: (block(j, i), 0))
    return in_spec, out_spec, jax.ShapeDtypeStruct((rows, d), BF16)


MM_SUBTILES = 4


def _mm_body(x_ref, w_ref, *rest, act, valid_cols, w_t):
    if len(rest) == 4:
        side_ref, o_ref, side_bf_ref, wbf_ref = rest
    else:
        (o_ref, wbf_ref), side_ref = rest, None

    @pl.when(pl.program_id(1) == 0)
    def _():
        w = w_ref[...].T if w_t else w_ref[...]
        if valid_cols is not None:
            w = jnp.where(_iota(w.shape, 1) < valid_cols, w, 0.0)
        wbf_ref[...] = w.astype(BF16)

    tm = x_ref.shape[0]
    sub = tm // MM_SUBTILES if tm % (MM_SUBTILES * BF16_ROWS) == 0 else tm
    for r0 in range(0, tm, sub):
        acc = jnp.dot(x_ref[r0:r0 + sub, :].astype(BF16), wbf_ref[...], preferred_element_type=F32)
        if act == "tanh":
            acc = jnp.tanh(acc)
        elif act == "sigmoid":
            acc = _sigmoid(acc)
        o_ref[r0:r0 + sub, :] = acc.astype(o_ref.dtype)
    if side_ref is not None:
        side_bf_ref[...] = side_ref[...].astype(BF16)


def _mm(x, w, lead=(), col0=0, ncols=None, tn=512, tm=1856, out_dtype=F32, act=None, w_t=False, side=None,
        name="mm"):
    m, k = x.shape
    n_all = w.shape[-2] if w_t else w.shape[-1]
    ncols = n_all if ncols is None else ncols
    tn = min(tn, ncols)
    tm = _row_tile(m, tm)
    assert ncols % tn == 0 and col0 % tn == 0 and m % tm == 0 and w.shape[-1 if w_t else -2] == k
    valid_cols = None
    if col0 + ncols > n_all:
        assert ncols == tn
        valid_cols = n_all - col0
    jb = col0 // tn
    if w_t:
        wspec = pl.BlockSpec((None,) * len(lead) + (tn, k), lambda j, i: lead + (j + jb, 0))
    else:
        wspec = pl.BlockSpec((None,) * len(lead) + (k, tn), lambda j, i: lead + (0, j + jb))
    grid = (ncols // tn, m // tm)
    in_specs = [pl.BlockSpec((tm, k), lambda j, i: (i, 0)), wspec]
    out_specs = [pl.BlockSpec((tm, tn), lambda j, i: (i, j))]
    out_shape = [jax.ShapeDtypeStruct((m, ncols), out_dtype)]
    args = [x, w]
    if side is not None:
        side_in, side_out, side_shape = _side_cast_specs(side, *grid)
        in_specs.append(side_in)
        out_specs.append(side_out)
        out_shape.append(side_shape)
        args.append(side[0])
    res = pl.pallas_call(
        functools.partial(_mm_body, act=act, valid_cols=valid_cols, w_t=w_t),
        grid=grid,
        in_specs=in_specs,
        out_specs=out_specs,
        out_shape=out_shape,
        scratch_shapes=[pltpu.VMEM((k, tn), BF16)],
        compiler_params=_cparams(2),
        name=name,
    )(*args)
    return res[0] if side is None else res


SWIGLU_SUBTILES = 4


def _swiglu_up_body(x_ref, wg_ref, wu_ref, wd_ref, o_ref, wd_bf_ref, wg_bf, wu_bf):
    @pl.when(pl.program_id(1) == 0)
    def _():
        wg_bf[...] = wg_ref[...].astype(BF16)
        wu_bf[...] = wu_ref[...].astype(BF16)

    tm = x_ref.shape[0]
    sub = tm // SWIGLU_SUBTILES if tm % (SWIGLU_SUBTILES * BF16_ROWS) == 0 else tm
    for r0 in range(0, tm, sub):
        x = x_ref[r0:r0 + sub, :]
        g = jnp.dot(x, wg_bf[...], preferred_element_type=F32)
        u = jnp.dot(x, wu_bf[...], preferred_element_type=F32)
        o_ref[r0:r0 + sub, :] = (g * _sigmoid(g) * u).astype(o_ref.dtype)
    wd_bf_ref[...] = wd_ref[...].astype(BF16)


def _swiglu_up(x, w_gate, w_up, w_down, lead, tn=512, tm=1856):
    m, k = x.shape
    n = w_gate.shape[-1]
    tm = _row_tile(m, tm)
    wspec = pl.BlockSpec((None,) * len(lead) + (k, tn), lambda j, i: lead + (0, j))
    grid = (n // tn, m // tm)
    side_in, side_out, side_shape = _side_cast_specs((w_down, lead), *grid)
    return pl.pallas_call(
        _swiglu_up_body,
        grid=grid,
        in_specs=[pl.BlockSpec((tm, k), lambda j, i: (i, 0)), wspec, wspec, side_in],
        out_specs=[pl.BlockSpec((tm, tn), lambda j, i: (i, j)), side_out],
        out_shape=[jax.ShapeDtypeStruct((m, n), BF16), side_shape],
        scratch_shapes=[pltpu.VMEM((k, tn), BF16), pltpu.VMEM((k, tn), BF16)],
        compiler_params=_cparams(2),
        name="swiglu_up",
    )(x, w_gate, w_up, w_down)


def _rms(x, g):
    return x * lax.rsqrt(jnp.mean(x * x, axis=-1, keepdims=True) + EPS) * g


def _norm_body(h_ref, g_ref, u_ref, *, gi):
    u_ref[...] = _rms(h_ref[...], g_ref[gi:gi + 1, :]).astype(u_ref.dtype)


def _norm(h, norm_g, layer, gi, out_dtype=BF16, tm=464):
    m, d = h.shape
    tm = _row_tile(m, tm)
    return pl.pallas_call(
        functools.partial(_norm_body, gi=gi),
        grid=(m // tm,),
        in_specs=[pl.BlockSpec((tm, d), lambda i: (i, 0)),
                  pl.BlockSpec((None, 6, d), lambda i: (layer, 0, 0))],
        out_specs=pl.BlockSpec((tm, d), lambda i: (i, 0)),
        out_shape=jax.ShapeDtypeStruct((m, d), out_dtype),
        compiler_params=_cparams(1),
        name="norm",
    )(h, norm_g)


def _down_resnorm_body(x_ref, w_ref, h_ref, ga_ref, gb_ref, hn_ref, *rest, scale, ia, ib):
    u_ref, y_scr = rest if len(rest) == 2 else (None, rest[0])
    j = pl.program_id(1)
    y_scr[j] = jnp.dot(x_ref[...], w_ref[...], preferred_element_type=F32)

    @pl.when(j == pl.num_programs(1) - 1)
    def _():
        y = jnp.concatenate([y_scr[c] for c in range(y_scr.shape[0])], axis=1)
        hn = h_ref[...] + scale * _rms(y, ga_ref[ia:ia + 1, :])
        hn_ref[...] = hn
        if u_ref is not None:
            u_ref[...] = _rms(hn, gb_ref[ib:ib + 1, :]).astype(u_ref.dtype)


RESIDENT_W_BYTES = 8 * 1024 * 1024


def _down_resnorm(x, w_bf, h, norm_g, scale, la, ia, lb=None, ib=None, u_dtype=BF16, tm=464):
    m, k = x.shape
    d = w_bf.shape[-1]
    tn = d if k * d * w_bf.dtype.itemsize <= RESIDENT_W_BYTES else 512
    tm = _row_tile(m, tm)
    row = pl.BlockSpec((tm, d), lambda i, j: (i, 0))
    with_u = lb is not None
    if not with_u:
        lb = la
    out_shape = [jax.ShapeDtypeStruct((m, d), F32)]
    out_specs = [row]
    if with_u:
        out_shape.append(jax.ShapeDtypeStruct((m, d), u_dtype))
        out_specs.append(row)
    res = pl.pallas_call(
        functools.partial(_down_resnorm_body, scale=scale, ia=ia, ib=ib),
        grid=(m // tm, d // tn),
        in_specs=[pl.BlockSpec((tm, k), lambda i, j: (i, 0)),
                  pl.BlockSpec((k, tn), lambda i, j: (0, j)),
                  row,
                  pl.BlockSpec((None, 6, d), lambda i, j: (la, 0, 0)),
                  pl.BlockSpec((None, 6, d), lambda i, j: (lb, 0, 0))],
        out_specs=out_specs,
        out_shape=out_shape,
        scratch_shapes=[pltpu.VMEM((d // tn, tm, tn), F32)],
        compiler_params=_cparams(2),
        name="down_resnorm",
    )(x, w_bf, h, norm_g, norm_g)
    return res if with_u else (res[0], None)


SSD_CHUNK = 128
SSD_GPS = 2


def _conv_taps(cur, shifted, w, b):
    out = b + cur * w[M_CONV - 1:M_CONV]
    for k in range(1, M_CONV):
        out = out + shifted(k) * w[M_CONV - 1 - k:M_CONV - k]
    return _silu(out)


def _conv_silu_seq(cur, prv, w, b):
    row = _iota((SUBLANES, cur.shape[1]), 0)

    def shifted(k):
        sh = pltpu.roll(cur, k, 0)
        top = jnp.where(row < k, pltpu.roll(prv, k, 0), sh[:SUBLANES])
        return jnp.concatenate([top, sh[SUBLANES:]], axis=0)

    return _conv_taps(cur, shifted, w, b)


def _conv_silu_batch(cur, stt, w, b):
    ntok = cur.shape[0]
    tpos = _iota((ntok, 1), 0) & (SUBLANES - 1)

    def shifted(k):
        return jnp.where(tpos < k, pltpu.roll(stt, (k - SUBLANES) % ntok, 0), pltpu.roll(cur, k, 0))

    return _conv_taps(cur, shifted, w, b)


def _expand_heads(rows):
    head_of_lane = _iota((M_R, M_GW), 1) >> _log2(M_HEADDIM)
    sel = (head_of_lane == _iota((M_R, M_GW), 0)).astype(BF16)
    return _xdot(rows, sel, terms=2)


def _ssd_common(x, bm, cm, dtr, hp, causal, same, nv, extra_rows):
    ntok = x.shape[0]
    bias, a = hp[0:1], -jnp.exp(hp[1:2])
    dt = _softplus(dtr + bias)
    if nv < ntok and same is None:
        dt = jnp.where(_iota((ntok, 1), 0) < nv, dt, 0.0)
    da = dt * a
    cum = _xdot_left(causal.astype(BF16), da)
    cum_tot = cum[ntok - 1:ntok] if same is None else _xdot_left(same.astype(BF16), da)
    cbm = _dot_nt(cm, bm)
    eye8 = (_iota((M_R, M_R), 0) == _iota((M_R, M_R), 1)).astype(BF16)
    dt_t = _xdot_nt(eye8, dt)
    even = (_iota((1, M_GW), 1) & M_HEADDIM) == 0
    x_e = jnp.where(even, x, 0.0).astype(BF16)
    x_o = jnp.where(even, 0.0, x).astype(BF16)
    yield
    cum2 = cum * LOG2_E
    cum2_t = _xdot_nt(eye8, cum2)
    tail = jnp.exp(cum_tot - cum) * dt
    ex = _expand_heads(jnp.concatenate([jnp.exp(cum), tail, extra_rows(cum_tot)], axis=0))
    yield
    parts = []
    for q in range(M_R // 2):
        acc = None
        for r, x_m in ((2 * q, x_e), (2 * q + 1, x_o)):
            dec = jnp.exp2(jnp.where(causal, cum2[:, r:r + 1] - cum2_t[r:r + 1, :], -jnp.inf))
            sc = (cbm * dec * dt_t[r:r + 1, :]).astype(BF16)
            yy = jnp.dot(sc, x_m[:, q * LANES:(q + 1) * LANES], preferred_element_type=F32)
            acc = yy if acc is None else acc + yy
        parts.append(acc)
    return jnp.concatenate(parts, axis=1), ex, cum_tot


def _ssd_finish(y, x, zc, d_row, ng):
    yv = (y + d_row * x) * _silu(zc)
    return yv * lax.rsqrt(jnp.mean(yv * yv, axis=-1, keepdims=True) + M_NORM_EPS) * ng


def _ssd_prompt_chunk(nv, zc, xr, br, cr, dtr, prev, st, cw, cb, hp, ng):
    n = SSD_CHUNK
    new_prev = (xr[nv - SUBLANES:], br[nv - SUBLANES:], cr[nv - SUBLANES:])
    zc, xr, br, cr, dtr = (_pad_rows(v, n) for v in (zc, xr, br, cr, dtr))
    x = _conv_silu_seq(xr, prev[0], cw[0], cb[0])
    bm = _conv_silu_seq(br, prev[1], cw[1], cb[1])
    cm = _conv_silu_seq(cr, prev[2], cw[2], cb[2])
    causal = _iota((n, n), 0) >= _iota((n, n), 1)
    row = _iota((SUBLANES, M_R), 0)

    def extra_rows(cum_tot):
        return jnp.where(row == 0, jnp.exp(cum_tot), jnp.where(row == 1, hp[2:3], 0.0))

    y_state = _dot(cm, st)
    y, ex, _ = yield from _ssd_common(x, bm, cm, dtr, hp, causal, None, nv, extra_rows)
    upd = _dot_tn(bm, ex[n:2 * n] * x)
    yield
    y = y + y_state * ex[:n]
    st = st * ex[2 * n:2 * n + 1] + upd
    return _ssd_finish(y, x, zc, ex[2 * n + 1:2 * n + 2], ng)[:nv], new_prev, st


def _ssd_sample_chunk(nb, zc, xr, br, cr, dtr, stt, states, cw, cb, hp, ng):
    n = SUBLANES
    ntok = nb * n
    x = _conv_silu_batch(xr, stt[0], cw[0], cb[0])
    bm = _conv_silu_batch(br, stt[1], cw[1], cb[1])
    cm = _conv_silu_batch(cr, stt[2], cw[2], cb[2])
    same, causal = _seq_masks(ntok, n)
    row = _iota((SUBLANES, M_R), 0)
    ys = [_dot_nt(_pad_rows(cm[b * n:(b + 1) * n], BF16_ROWS), states[b])[:n] for b in range(nb)]
    y, ex, cum_tot = yield from _ssd_common(x, bm, cm, dtr, hp, causal, same, n,
                                            lambda _: jnp.where(row == 0, hp[2:3], 0.0))
    xt = ex[ntok:2 * ntok] * x
    upds = [_dot_tn(_pad_rows(xt[b * n:(b + 1) * n], BF16_ROWS), _pad_rows(bm[b * n:(b + 1) * n], BF16_ROWS))
            for b in range(nb)]
    yield
    etot = jnp.exp(cum_tot)
    new_states = []
    for b in range(nb):
        decay = jnp.concatenate([jnp.broadcast_to(etot[b * n:b * n + 1, r:r + 1], (M_HEADDIM, M_D_STATE))
                                 for r in range(M_R)], axis=0)
        new_states.append(states[b] * decay + upds[b])
    y = y + jnp.concatenate(ys, axis=0) * ex[:ntok]
    return _ssd_finish(y, x, zc, ex[2 * ntok:2 * ntok + 1], ng), new_states


def _ssd_body(*refs, t, nb, has_state):
    if has_state:
        (z_ref, x_ref, b_ref, c_ref, dt_ref, cwx_ref, cwb_ref, cwc_ref, cbx_ref, cbb_ref, cbc_ref,
         hp_ref, ng_ref, csx_ref, csb_ref, csc_ref, sin_ref, y_ref, sout_ref) = refs
    else:
        (z_ref, x_ref, b_ref, c_ref, dt_ref, cwx_ref, cwb_ref, cwc_ref, cbx_ref, cbb_ref, cbc_ref,
         hp_ref, ng_ref, y_ref, sout_ref) = refs

    def consts(gi):
        wide = slice(gi * M_GW, (gi + 1) * M_GW)
        nar = slice(gi * M_D_STATE, (gi + 1) * M_D_STATE)
        cw = (cwx_ref[:, wide], cwb_ref[:, nar], cwc_ref[:, nar])
        cb = (cbx_ref[:, wide], cbb_ref[:, nar], cbc_ref[:, nar])
        return wide, nar, cw, cb, hp_ref[gi], ng_ref[:, wide]

    if has_state:
        ntok = nb * t
        gens = []
        for gi in range(SSD_GPS):
            wide, nar, cw, cb, hp, ng = consts(gi)
            stt = (csx_ref[:, :, wide].reshape(ntok, M_GW), csb_ref[:, :, nar].reshape(ntok, M_D_STATE),
                   csc_ref[:, :, nar].reshape(ntok, M_D_STATE))
            states = [sin_ref[b, gi * M_R:(gi + 1) * M_R].reshape(M_GW, M_D_STATE) for b in range(nb)]
            gens.append(_ssd_sample_chunk(nb, z_ref[:, wide], x_ref[:, wide], b_ref[:, nar], c_ref[:, nar],
                                          dt_ref[gi], stt, states, cw, cb, hp, ng))
        for gi, (out, states) in enumerate(_lockstep(gens)):
            y_ref[:, gi * M_GW:(gi + 1) * M_GW] = out.astype(y_ref.dtype)
            for b in range(nb):
                sout_ref[b, gi * M_R:(gi + 1) * M_R] = states[b].reshape(M_R, M_HEADDIM, M_D_STATE)
        return

    def run(start, nv, carry):
        rows = pl.ds(pl.multiple_of(start, SUBLANES), nv)
        gens = []
        for gi in range(SSD_GPS):
            wide, nar, cw, cb, hp, ng = consts(gi)
            prev, st = carry[gi]
            gens.append(_ssd_prompt_chunk(nv, z_ref[rows, wide], x_ref[rows, wide], b_ref[rows, nar],
                                          c_ref[rows, nar], dt_ref[gi, rows, :], prev, st, cw, cb, hp, ng))
        new = []
        for gi, (out, prev, st) in enumerate(_lockstep(gens)):
            y_ref[rows, gi * M_GW:(gi + 1) * M_GW] = out.astype(y_ref.dtype)
            new.append((prev, st))
        return tuple(new)

    zero_prev = (jnp.zeros((SUBLANES, M_GW), F32), jnp.zeros((SUBLANES, M_D_STATE), F32),
                 jnp.zeros((SUBLANES, M_D_STATE), F32))
    carry = tuple((zero_prev, jnp.zeros((M_D_STATE, M_GW), F32)) for _ in range(SSD_GPS))
    n_full = t // SSD_CHUNK
    carry = lax.fori_loop(0, n_full, lambda c, cr: run(c * SSD_CHUNK, SSD_CHUNK, cr), carry)
    if t % SSD_CHUNK:
        carry = run(n_full * SSD_CHUNK, t % SSD_CHUNK, carry)
    for gi in range(SSD_GPS):
        sout_ref[0, gi * M_R:(gi + 1) * M_R] = carry[gi][1].T.reshape(M_R, M_HEADDIM, M_D_STATE)


def _ssd(j, z, xbc, dtg, m_conv_w, conv_b3, hp, m_norm_g3, group, conv_state=None, ssm_state=None,
         y_into=None, state_into=None):
    nseq, t, row0, nb = group
    rb = nb * t
    r0 = row0 // rb
    assert row0 % rb == 0 and nseq % nb == 0 and M_GROUPS % SSD_GPS == 0
    has_state = ssm_state is not None
    gw, gn, gr = SSD_GPS * M_GW, SSD_GPS * M_D_STATE, SSD_GPS * M_R
    b0 = M_D_INNER // gn
    c0 = b0 + M_GROUPS // SSD_GPS
    one = _single(not has_state)
    in_specs = [
        pl.BlockSpec((rb, gw), lambda s, g: (s + r0, g)),
        pl.BlockSpec((rb, gw), lambda s, g: (s + r0, g)),
        pl.BlockSpec((rb, gn), lambda s, g: (s + r0, b0 + g), **one),
        pl.BlockSpec((rb, gn), lambda s, g: (s + r0, c0 + g), **one),
        pl.BlockSpec((SSD_GPS, rb, M_R), lambda s, g: (g, s + r0, 0), **one),
        pl.BlockSpec((None, M_CONV, gw), lambda s, g: (j, 0, g)),
        pl.BlockSpec((None, M_CONV, gn), lambda s, g: (j, 0, b0 + g)),
        pl.BlockSpec((None, M_CONV, gn), lambda s, g: (j, 0, c0 + g)),
        pl.BlockSpec((None, 1, gw), lambda s, g: (j, 0, g)),
        pl.BlockSpec((None, 1, gn), lambda s, g: (j, 0, b0 + g)),
        pl.BlockSpec((None, 1, gn), lambda s, g: (j, 0, c0 + g)),
        pl.BlockSpec((None, SSD_GPS, 3, M_R), lambda s, g: (j, g, 0, 0)),
        pl.BlockSpec((None, 1, gw), lambda s, g: (j, 0, g)),
    ]
    args = [z, xbc, xbc, xbc, dtg, m_conv_w, m_conv_w, m_conv_w, conv_b3, conv_b3, conv_b3, hp, m_norm_g3]
    if has_state:
        in_specs += [
            pl.BlockSpec((None, nb, SUBLANES, gw), lambda s, g: (j, s, 0, g)),
            pl.BlockSpec((None, nb, SUBLANES, gn), lambda s, g: (j, s, 0, b0 + g)),
            pl.BlockSpec((None, nb, SUBLANES, gn), lambda s, g: (j, s, 0, c0 + g)),
            pl.BlockSpec((None, nb, gr, M_HEADDIM, M_D_STATE), lambda s, g: (j, s, g, 0, 0)),
        ]
        args += [conv_state, conv_state, conv_state, ssm_state]
    body, aliases = _write_into(functools.partial(_ssd_body, t=t, nb=nb, has_state=has_state),
                                in_specs, args, (y_into, state_into))
    return pl.pallas_call(
        body,
        grid=(nseq // nb, M_GROUPS // SSD_GPS),
        in_specs=in_specs,
        out_specs=[pl.BlockSpec((rb, gw), lambda s, g: (s + r0, g)),
                   pl.BlockSpec((None, nb, gr, M_HEADDIM, M_D_STATE), lambda s, g: (j, s, g, 0, 0))],
        out_shape=[jax.ShapeDtypeStruct((z.shape[0], M_D_INNER), BF16),
                   jax.ShapeDtypeStruct((hp.shape[0], nseq, M_HEADS, M_HEADDIM, M_D_STATE), F32)],
        input_output_aliases=aliases,
        compiler_params=_cparams(2),
        name="ssd_sample" if has_state else "ssd_prompt",
    )(*args)


def _mamba_layer(j, u, groups, m_w_in, m_conv_w, m_conv_b, m_dt_bias, m_a_log, m_d, m_norm_g, m_w_out,
                 state_conv_pad, state_ssm, ssm_into=(None, None)):
    m = u.shape[0]
    gp, gs = groups
    w_in_t = jnp.swapaxes(m_w_in, 1, 2)
    z, w_out_bf = _mm(u, w_in_t, (j,), 0, M_D_INNER, w_t=True, side=(m_w_out, (j,)), name="ssd_in_z")
    xbc = _mm(u, w_in_t, (j,), M_D_INNER, M_CONV_DIM, w_t=True, name="ssd_in_xbc")
    dt = _mm(u, w_in_t, (j,), M_D_INNER + M_CONV_DIM, LANES, tn=LANES, w_t=True, name="ssd_in_dt")[:, :M_HEADS]
    dtg = dt.reshape(m, M_GROUPS, M_R).transpose(1, 0, 2)
    hp = jnp.stack([m_dt_bias, m_a_log, m_d], axis=1).reshape(-1, 3, M_GROUPS, M_R).transpose(0, 2, 1, 3)
    conv_b3 = m_conv_b[:, None, :]
    ng3 = m_norm_g[:, None, :]
    y, sp = _ssd(j, z, xbc, dtg, m_conv_w, conv_b3, hp, ng3, gp, state_into=ssm_into[0])
    y, ss = _ssd(j, z, xbc, dtg, m_conv_w, conv_b3, hp, ng3, gs, state_conv_pad, state_ssm,
                 y_into=y, state_into=ssm_into[1])
    keep = M_CONV - 1
    xp = jnp.stack([xbc[(b + 1) * gp[1] - keep:(b + 1) * gp[1]] for b in range(gp[0])])
    xs = xbc[gp[0] * gp[1]:].reshape(gs[0], gs[1], M_CONV_DIM)[:, gs[1] - keep:]
    return (y, w_out_bf), (sp, xp), (ss, xs)


R_TOK = 64
R_PAIRS = 8
R_SPLIT = 2


def _rwkv_chunk(nb, nv, toks, sbs, prm):
    n = R_TOK // nb
    ntok = R_TOK
    r, kr, v, wl, al, g = (_pad_rows(t_, ntok) for t_ in toks)
    w0, a0, k_k, k_a, r_k, ln_g, ln_b = prm
    m0 = _iota((1, LANES), 1) < R_HEADSIZE
    ones_blk = ((_iota((LANES, LANES), 0) < R_HEADSIZE) == (_iota((LANES, LANES), 1) < R_HEADSIZE)).astype(BF16)

    w = -_softplus(-(w0 + wl)) - 0.5
    lw = -jnp.exp(w)
    a = _sigmoid(a0 + al)
    kk = kr * k_k
    k = kr * (1.0 + (a - 1.0) * k_a)
    if nv < n:
        valid = _iota((ntok, 1), 0) < nv
        lw = jnp.where(valid, lw, 0.0)
        kk = jnp.where(valid, kk, 0.0)
        k = jnp.where(valid, k, 0.0)
        v = jnp.where(valid, v, 0.0)
    same, causal = _seq_masks(ntok, n)
    head_sums = _xdot(jnp.concatenate([kk * kk, r * k * r_k], axis=0), ones_blk, terms=2)
    lam = _xdot_left(causal.astype(BF16), lw)
    lam_tot = lam[ntok - 1:ntok] if nb == 1 else _xdot_left(same.astype(BF16), lw)
    yield
    bonus = head_sums[ntok:] * v
    kk = kk / jnp.maximum(jnp.sqrt(head_sums[:ntok]), 1e-12)
    beta = kk * a
    e_in = jnp.exp(lam)
    e_out = jnp.exp(-lam)
    e_end = jnp.exp(lam_tot - lam)

    def stack(x):
        x3 = x.reshape(nb, n, LANES)
        s = jnp.concatenate([jnp.where(m0, x3, 0.0), jnp.where(m0, 0.0, x3)], axis=1)
        return s.reshape(2 * ntok, LANES).astype(BF16)

    am, rm = stack(kk * jnp.exp(lam - lw)), stack(r * e_in)
    khm, bhm = stack(k * e_out), stack(beta * e_out)
    vst = stack(v)
    n2 = 2 * ntok
    rr, cc = _iota((n2, n2), 0), _iota((n2, n2), 1)
    same_blk = (rr >> _log2(n)) == (cc >> _log2(n))
    strict = same_blk & ((rr & (n - 1)) > (cc & (n - 1)))
    incl = same_blk & ((rr & (n - 1)) >= (cc & (n - 1)))
    qm = jnp.concatenate([am, rm], axis=0)
    pmat = _dot_nt(qm, jnp.concatenate([khm, bhm], axis=0))
    w2 = 2 * n
    if nb == 1:
        qs = [_dot_nt(qm, sbs[0])]
    else:
        qs = [_dot_nt(jnp.concatenate([am[b * w2:(b + 1) * w2], rm[b * w2:(b + 1) * w2]], axis=0), sbs[b])
              for b in range(nb)]
    ktm, btm = stack(k * e_end), stack(beta * e_end)
    yield
    a_ak = jnp.where(strict, pmat[:n2, :n2], 0.0)
    nmat = jnp.where(strict, -pmat[:n2, n2:], 0.0)
    a_rk = jnp.where(incl, pmat[n2:, :n2], 0.0)
    a_rb = jnp.where(incl, pmat[n2:, n2:], 0.0)
    rhs_s = qs[0][:w2] if nb == 1 else jnp.concatenate([q[:w2] for q in qs], axis=0)
    ys = qs[0][w2:] if nb == 1 else jnp.concatenate([q[w2:] for q in qs], axis=0)

    tm = (rr == cc).astype(F32) + nmat
    npow = _dot(nmat, nmat)
    av = _dot(jnp.concatenate([a_ak, a_rk], axis=0), vst)
    yield
    rhs = rhs_s + av[:n2]
    ys = ys + av[n2:]
    for i in range(_log2(n) - 1):
        if i + 2 < _log2(n):
            both = _dot(jnp.concatenate([tm, npow], axis=0), npow)
            prod, npow = both[:n2], both[n2:]
        else:
            prod = _dot(tm, npow)
        yield
        tm = tm + prod
    u = _dot(tm, rhs)
    yield
    yst = (ys - _dot(a_rb, u)).reshape(nb, w2, LANES)
    nu = (-u).astype(BF16)
    new_sbs = []
    for b in range(nb):
        rows = slice(b * w2, (b + 1) * w2)
        lt = lam_tot if nb == 1 else lam_tot[b * n:b * n + 1]
        new_sbs.append(sbs[b] * jnp.exp(lt)
                       + _dot_tn(jnp.concatenate([vst[rows], nu[rows]], axis=0),
                                 jnp.concatenate([ktm[rows], btm[rows]], axis=0)))
    yield
    y = (yst[:, :n] + yst[:, n:]).reshape(ntok, LANES)
    inv = 1.0 / R_HEADSIZE
    mean = _xdot(y, ones_blk, terms=2) * inv
    yield
    yc = y - mean
    var = _xdot(yc * yc, ones_blk, terms=2) * inv
    yield
    yn = yc * lax.rsqrt(var + R_LN_EPS) * ln_g + ln_b
    return ((yn + bonus) * g)[:nb * nv], new_sbs


def _rwkv_body(*refs, t, nb, has_state):
    if has_state:
        (r_ref, k_ref, v_ref, wl_ref, al_ref, g_ref, prm_ref, sin_ref, y_ref, sout_ref) = refs
    else:
        (r_ref, k_ref, v_ref, wl_ref, al_ref, g_ref, prm_ref, y_ref, sout_ref, st_scr) = refs
    tok_refs = (r_ref, k_ref, v_ref, wl_ref, al_ref, g_ref)
    zeros = jnp.zeros((R_HEADSIZE, R_HEADSIZE), F32)

    def lanes(p):
        return slice(p * LANES, (p + 1) * LANES)

    def run(rows, nv, states):
        res = _lockstep([
            _rwkv_chunk(nb, nv, tuple(ref[rows, lanes(p)] for ref in tok_refs), states[p],
                        tuple(prm_ref[i:i + 1, lanes(p)] for i in range(7)))
            for p in range(R_PAIRS)])
        for p, (out, _) in enumerate(res):
            y_ref[rows, lanes(p)] = out.astype(y_ref.dtype)
        return tuple(tuple(sbs) for _, sbs in res)

    if has_state:
        states = tuple(
            tuple(jnp.concatenate([jnp.concatenate([sin_ref[b, 2 * p], zeros], axis=1),
                                   jnp.concatenate([zeros, sin_ref[b, 2 * p + 1]], axis=1)], axis=0)
                  for b in range(nb))
            for p in range(R_PAIRS))
    else:
        @pl.when(pl.program_id(2) == 0)
        def _():
            st_scr[...] = jnp.zeros(st_scr.shape, F32)

        states = tuple((st_scr[p],) for p in range(R_PAIRS))

    if nb == 1:
        n_full = t // R_TOK
        states = lax.fori_loop(
            0, n_full, lambda c, st: run(pl.ds(pl.multiple_of(c * R_TOK, R_TOK), R_TOK), R_TOK, st), states)
        if t % R_TOK:
            states = run(pl.ds(n_full * R_TOK, t % R_TOK), t % R_TOK, states)
    else:
        assert nb * t == R_TOK
        states = run(slice(None), t, states)
    for p in range(R_PAIRS):
        if not has_state:
            st_scr[p] = states[p][0]
        for b in range(nb):
            sout_ref[b, 2 * p] = states[p][b][:R_HEADSIZE, :R_HEADSIZE]
            sout_ref[b, 2 * p + 1] = states[p][b][R_HEADSIZE:, R_HEADSIZE:]


def _rwkv(r, k, v, wl, al, g, prm, group, wkv_state=None, y_into=None):
    nseq, t, row0, nb = group
    has_state = wkv_state is not None
    split = 1 if has_state else R_SPLIT
    tb = t // split
    rb = nb * tb
    r0 = row0 // rb
    assert t % split == 0 and tb % SUBLANES == 0 and row0 % rb == 0 and nseq % nb == 0
    gl = R_PAIRS * LANES

    def rows(s, hp_, tt):
        return (s * split + tt + r0, hp_)

    tok = pl.BlockSpec((rb, gl), rows)
    tok1 = pl.BlockSpec((rb, gl), rows, **_single(not has_state))
    in_specs = [tok] * 4 + [tok1] * 2 + [pl.BlockSpec((SUBLANES, gl), lambda s, hp_, tt: (0, hp_))]
    args = [r, k, v, wl, al, g, prm]
    if has_state:
        in_specs.append(pl.BlockSpec((None, nb, 2 * R_PAIRS, R_HEADSIZE, R_HEADSIZE),
                                     lambda s, hp_, tt: (0, s, hp_, 0, 0)))
        args.append(wkv_state)
    body, aliases = _write_into(functools.partial(_rwkv_body, t=tb, nb=nb, has_state=has_state),
                                in_specs, args, (y_into,))
    return pl.pallas_call(
        body,
        grid=(nseq // nb, R_HEADS // (2 * R_PAIRS), split),
        in_specs=in_specs,
        out_specs=[pl.BlockSpec((rb, gl), rows),
                   pl.BlockSpec((nb, 2 * R_PAIRS, R_HEADSIZE, R_HEADSIZE), lambda s, hp_, tt: (s, hp_, 0, 0))],
        out_shape=[jax.ShapeDtypeStruct((r.shape[0], D_MODEL), BF16),
                   jax.ShapeDtypeStruct((nseq, R_HEADS, R_HEADSIZE, R_HEADSIZE), F32)],
        scratch_shapes=[] if has_state else [pltpu.VMEM((R_PAIRS, LANES, LANES), F32)],
        input_output_aliases=aliases,
        compiler_params=_cparams(3),
        name="rwkv_sample" if has_state else "rwkv_prompt",
    )(*args)


MIX_COLS = 512


def _mix_body(*refs, t, nb, has_state):
    if has_state:
        u_ref, mu_ref, sh_ref = refs[:3]
    else:
        u_ref, mu_ref = refs[:2]
    o_refs = refs[-6:]
    u = u_ref[...]
    rows = nb * t
    tpos = _iota((rows, 1), 0) & (t - 1) if nb > 1 else _iota((rows, 1), 0)
    prev = pltpu.roll(u, 1, 0)
    if has_state:
        pick = (_iota((rows, BF16_ROWS), 0) >> _log2(t)) == _iota((rows, BF16_ROWS), 1)
        first = _xdot_left(pick.astype(BF16), _pad_rows(sh_ref[...], BF16_ROWS))
    else:
        first = 0.0
    dx = jnp.where(tpos == 0, first, prev) - u
    for i, o_ref in enumerate(o_refs):
        o_ref[...] = (u + dx * mu_ref[i:i + 1, :]).astype(o_ref.dtype)


def _rwkv_mix(u, r_mu, group, shift_state=None, into=None):
    nseq, t, row0, nb = group
    m, d = u.shape
    rb = nb * t
    r0 = row0 // rb
    has_state = shift_state is not None
    assert row0 % rb == 0 and nseq % nb == 0 and (nb == 1 or nb <= BF16_ROWS)
    blk = pl.BlockSpec((rb, MIX_COLS), lambda s, c: (s + r0, c))
    in_specs = [blk, pl.BlockSpec((None, 6, MIX_COLS), lambda s, c: (0, 0, c))]
    args = [u, r_mu]
    if has_state:
        in_specs.append(pl.BlockSpec((None, nb, MIX_COLS), lambda s, c: (0, s, c)))
        args.append(shift_state)
    body, aliases = _write_into(functools.partial(_mix_body, t=t, nb=nb, has_state=has_state),
                                in_specs, args, into or (None,) * 6)
    return pl.pallas_call(
        body,
        grid=(nseq // nb, d // MIX_COLS),
        in_specs=in_specs,
        out_specs=[blk] * 6,
        out_shape=[jax.ShapeDtypeStruct((m, d), BF16)] * 6,
        input_output_aliases=aliases,
        compiler_params=_cparams(2),
        name="rwkv_mix_sample" if has_state else "rwkv_mix_prompt",
    )(*args)


def _rwkv_layer(u, groups, state_wkv, state_shift, r_mu, r_w_r, r_w_k, r_w_v, r_w0, r_w1, r_w2, r_a0, r_a1,
                r_a2, r_g1, r_g2, r_k_k, r_k_a, r_r_k, r_ln_g, r_ln_b, r_w_o):
    gp, gs = groups
    mixed = _rwkv_mix(u, r_mu, gp)
    xr, xw, xk, xv, xa, xg = _rwkv_mix(u, r_mu, gs, state_shift, into=mixed)
    r, w_o_bf = _mm(xr, r_w_r, (0,), side=(r_w_o, (0,)), name="rwkv_r")
    k = _mm(xk, r_w_k, (0,), name="rwkv_k")
    v = _mm(xv, r_w_v, (0,), name="rwkv_v")
    wl = _mm(_mm(xw, r_w1, (0,), out_dtype=BF16, act="tanh", name="rwkv_w1"), r_w2, (0,), name="rwkv_w2")
    al = _mm(_mm(xa, r_a1, (0,), out_dtype=BF16, name="rwkv_a1"), r_a2, (0,), name="rwkv_a2")
    g = _mm(_mm(xg, r_g1, (0,), out_dtype=BF16, act="sigmoid", name="rwkv_g1"), r_g2, (0,), name="rwkv_g2")
    prm = jnp.concatenate([r_w0, r_a0, r_k_k, r_k_a, r_r_k.reshape(1, D_MODEL), r_ln_g, r_ln_b,
                           jnp.zeros((1, D_MODEL), F32)], axis=0)
    y, sp = _rwkv(r, k, v, wl, al, g, prm, gp)
    y, ss = _rwkv(r, k, v, wl, al, g, prm, gs, state_wkv, y_into=y)
    last_p = jnp.stack([u[(b + 1) * gp[1] - 1] for b in range(gp[0])])
    last_s = u[gp[0] * gp[1]:].reshape(gs[0], gs[1], D_MODEL)[:, -1]
    return (y, w_o_bf), (sp, last_p), (ss, last_s)


H_CHUNK = 128
H_SUB = 16
H_HPS = 4


def _gla_diag(q3, g3, k3, v3):
    sub = q3.shape[1]
    pieces = [slice(p, p + SUBLANES) for p in range(0, sub, SUBLANES)]
    tpos = _iota((1, SUBLANES, 1), 1)
    g3 = g3 * LOG2_E
    ys = [jnp.zeros(v3[:, p].shape, F32) for p in pieces]
    for s in range(sub):
        terms = []
        for i, p in enumerate(pieces):
            if p.stop <= s:
                continue
            arg = g3[:, p] - g3[:, s:s + 1, :]
            if p.start < s:
                arg = jnp.where(tpos + p.start >= s, arg, -jnp.inf)
            terms.append((i, jnp.sum(q3[:, p] * k3[:, s:s + 1, :] * jnp.exp2(arg), axis=-1, keepdims=True)))
        yield
        for i, a_s in terms:
            ys[i] = ys[i] + a_s * v3[:, s:s + 1, :]
    return ys[0] if len(ys) == 1 else jnp.concatenate(ys, axis=1)


def _gla_prompt_chunk(nv, q, logf, k, v, st):
    n = -(-nv // H_SUB) * H_SUB
    q, logf, k, v = (_pad_rows(t_, n) for t_ in (q, logf, k, v))
    causal = _iota((n, n), 0) >= _iota((n, n), 1)
    gcum = _xdot_left(causal.astype(BF16), logf)
    yield
    y = _dot_nt(q * jnp.exp(gcum), st)
    g_n = gcum[n - 1:n]
    upd = _dot_tn(v, k * jnp.exp(g_n - gcum))
    atts = []
    for i in range(1, n // H_SUB):
        lo = i * H_SUB
        gs = gcum[lo - 1:lo]
        atts.append(_dot_nt(q[lo:lo + H_SUB] * jnp.exp(gcum[lo:lo + H_SUB] - gs),
                            k[:lo] * jnp.exp(gs - gcum[:lo])))
    yield
    st = st * jnp.exp(g_n) + upd
    offs = [jnp.zeros((H_SUB, LANES), F32)] + [_dot(att, v[:(i + 1) * H_SUB]) for i, att in enumerate(atts)]
    shape3 = (n // H_SUB, H_SUB, LANES)
    y3 = yield from _gla_diag(q.reshape(shape3), gcum.reshape(shape3), k.reshape(shape3), v.reshape(shape3))
    off = offs[0] if len(offs) == 1 else jnp.concatenate(offs, axis=0)
    return (y + off + y3.reshape(n, LANES))[:nv], st


def _gla_sample_chunk(nb, q, logf, k, v, sts):
    n = SUBLANES
    ntok = nb * n
    _, causal = _seq_masks(ntok, n)
    gcum = _xdot_left(causal.astype(BF16), logf)
    yield
    shape3 = (nb, n, LANES)
    g3 = gcum.reshape(shape3)
    g_tot = g3[:, n - 1:n, :]
    qe = q * jnp.exp(gcum)
    kd = (k.reshape(shape3) * jnp.exp(g_tot - g3)).reshape(ntok, LANES)
    ys = [_dot_nt(_pad_rows(qe[b * n:(b + 1) * n], BF16_ROWS), sts[b])[:n] for b in range(nb)]
    upds = [_dot_tn(_pad_rows(v[b * n:(b + 1) * n], BF16_ROWS), _pad_rows(kd[b * n:(b + 1) * n], BF16_ROWS))
            for b in range(nb)]
    y3 = yield from _gla_diag(q.reshape(shape3), g3, k.reshape(shape3), v.reshape(shape3))
    new = [sts[b] * jnp.exp(g_tot[b]) + upds[b] for b in range(nb)]
    return y3.reshape(ntok, LANES) + jnp.concatenate(ys, axis=0), new


def _hgrn_body(*refs, t, nb, has_state, layer):
    if has_state:
        (q_ref, f_ref, i_ref, g_ref, lb_ref, ng_ref, sin_ref, y_ref, sout_ref) = refs
    else:
        (q_ref, f_ref, i_ref, g_ref, lb_ref, ng_ref, y_ref, sout_ref) = refs
    hl = lb_ref[...]
    e = jnp.exp(hl - jnp.max(hl, axis=0, keepdims=True))
    sm = e / jnp.sum(e, axis=0, keepdims=True)
    lb_all = (jnp.sum(sm[1:layer + 1], axis=0, keepdims=True) if layer > 0
              else jnp.zeros((1, H_HPS * LANES), F32))
    ng = ng_ref[...]

    def lanes(hi):
        return slice(hi * LANES, (hi + 1) * LANES)

    def gates(rows, hi):
        lb = lb_all[:, lanes(hi)]
        q = _silu(q_ref[rows, lanes(hi)])
        f = lb + (1.0 - lb) * _sigmoid(f_ref[rows, lanes(hi)])
        return q, jnp.log(f), 1.0 - f, i_ref[rows, lanes(hi)]

    def finish(rows, hi, yy):
        yy = yy * lax.rsqrt(jnp.mean(yy * yy, axis=-1, keepdims=True) + EPS) * ng
        y_ref[rows, lanes(hi)] = (yy * _silu(g_ref[rows, lanes(hi)])).astype(y_ref.dtype)

    if has_state:
        rows = slice(None)
        res = _lockstep([_gla_sample_chunk(nb, *gates(rows, hi), [sin_ref[b, hi].T for b in range(nb)])
                         for hi in range(H_HPS)])
        for hi, (yy, sts) in enumerate(res):
            finish(rows, hi, yy)
            for b in range(nb):
                sout_ref[b, hi] = sts[b].T
        return

    def run(start, nv, sts):
        rows = pl.ds(pl.multiple_of(start, SUBLANES), nv)
        res = _lockstep([_gla_prompt_chunk(nv, *gates(rows, hi), sts[hi]) for hi in range(H_HPS)])
        for hi, (yy, _) in enumerate(res):
            finish(rows, hi, yy)
        return tuple(st for _, st in res)

    sts = tuple(jnp.zeros((H_VDIM, H_EXPAND), F32) for _ in range(H_HPS))
    n_full = t // H_CHUNK
    sts = lax.fori_loop(0, n_full, lambda c, s: run(c * H_CHUNK, H_CHUNK, s), sts)
    if t % H_CHUNK:
        sts = run(n_full * H_CHUNK, t % H_CHUNK, sts)
    for hi in range(H_HPS):
        sout_ref[0, hi] = sts[hi].T


def _hgrn(layer, proj, h_lb, h_norm_g, group, state=None, y_into=None):
    nseq, t, row0, nb = group
    rb = nb * t
    r0 = row0 // rb
    assert row0 % rb == 0 and nseq % nb == 0
    has_state = state is not None
    nhb = H_HEADS // H_HPS
    hw = H_HPS * LANES

    def tok(section):
        return pl.BlockSpec((rb, hw), lambda s, h: (s + r0, h + section * nhb))

    in_specs = [tok(0), tok(1), tok(2), tok(3),
                pl.BlockSpec((DEPTH, hw), lambda s, h: (0, h)),
                pl.BlockSpec((1, LANES), lambda s, h: (0, 0))]
    args = [proj, proj, proj, proj, h_lb, h_norm_g]
    if has_state:
        in_specs.append(pl.BlockSpec((None, nb, H_HPS, H_EXPAND, H_VDIM), lambda s, h: (0, s, h, 0, 0)))
        args.append(state)
    body, aliases = _write_into(functools.partial(_hgrn_body, t=t, nb=nb, has_state=has_state, layer=layer),
                                in_specs, args, (y_into,))
    return pl.pallas_call(
        body,
        grid=(nseq // nb, nhb),
        in_specs=in_specs,
        out_specs=[pl.BlockSpec((rb, hw), lambda s, h: (s + r0, h)),
                   pl.BlockSpec((nb, H_HPS, H_EXPAND, H_VDIM), lambda s, h: (s, h, 0, 0))],
        out_shape=[jax.ShapeDtypeStruct((proj.shape[0], D_MODEL), BF16),
                   jax.ShapeDtypeStruct((nseq, H_HEADS, H_EXPAND, H_VDIM), F32)],
        input_output_aliases=aliases,
        compiler_params=_cparams(2),
        name="hgrn_sample" if has_state else "hgrn_prompt",
    )(*args)


def _hgrn_layer(layer, u, groups, state_hgrn, h_w_in, h_lb, h_norm_g, h_w_out):
    proj, w_out_bf = _mm(u, h_w_in, (0,), side=(h_w_out, (0,)), name="hgrn_in")
    y, sp = _hgrn(layer, proj, h_lb, h_norm_g, groups[0])
    y, ss = _hgrn(layer, proj, h_lb, h_norm_g, groups[1], state_hgrn, y_into=y)
    return (y, w_out_bf), sp, ss


def kernel(x_prompt, x_sample, state_ssm, state_conv, state_wkv, state_shift, state_hgrn, meta_tokens, norm_g, ffn_w_gate, ffn_w_up, ffn_w_down, m_w_in, m_conv_w, m_conv_b, m_dt_bias, m_a_log, m_d, m_norm_g, m_w_out, r_mu, r_w_r, r_w_k, r_w_v, r_w0, r_w1, r_w2, r_a0, r_a1, r_a2, r_g1, r_g2, r_k_k, r_k_a, r_r_k, r_ln_g, r_ln_b, r_w_o, h_w_in, h_lb, h_norm_g, h_w_out):
    pieces = [p for b in range(BATCH) for p in (meta_tokens, x_prompt[b])] + [x_sample.reshape(M_S, D_MODEL)]
    h = jnp.concatenate(pieces, axis=0)
    conv_pad = jnp.pad(state_conv, ((0, 0), (0, 0), (SUBLANES - (M_CONV - 1), 0), (0, 0)))

    p_conv, s_conv = [], []
    ssm = (None, None)
    u = _norm(h, norm_g, 0, 0)
    for l in range(DEPTH):
        kind, j = l % 3, l // 3
        act, w_down = _swiglu_up(u, ffn_w_gate, ffn_w_up, ffn_w_down, (l, 0))
        h, u = _down_resnorm(act, w_down, h, norm_g, 0.5, l, 1, l, 2, u_dtype=F32 if kind == 1 else BF16)
        if kind == 0:
            y, (sp, cp), (ss, cs) = _mamba_layer(j, u, GROUPS, m_w_in, m_conv_w, m_conv_b, m_dt_bias, m_a_log,
                                                 m_d, m_norm_g, m_w_out, conv_pad, state_ssm, ssm)
            ssm = (sp, ss)
            p_conv.append(cp)
            s_conv.append(cs)
        elif kind == 1:
            y, (p_wkv, p_shift), (s_wkv, s_shift) = _rwkv_layer(
                u, GROUPS, state_wkv, state_shift, r_mu, r_w_r, r_w_k, r_w_v, r_w0, r_w1, r_w2, r_a0, r_a1,
                r_a2, r_g1, r_g2, r_k_k, r_k_a, r_r_k, r_ln_g, r_ln_b, r_w_o)
        else:
            y, p_hgrn, s_hgrn = _hgrn_layer(l, u, GROUPS, state_hgrn, h_w_in, h_lb, h_norm_g, h_w_out)
        h, u = _down_resnorm(*y, h, norm_g, 1.0, l, 3, l, 4)
        act, w_down = _swiglu_up(u, ffn_w_gate, ffn_w_up, ffn_w_down, (l, 1))
        if l + 1 < DEPTH:
            h, u = _down_resnorm(act, w_down, h, norm_g, 0.5, l, 5, l + 1, 0)
        else:
            h, _ = _down_resnorm(act, w_down, h, norm_g, 0.5, l, 5)

    y_prompt = jnp.stack([h[b * T_P + N_META:(b + 1) * T_P] for b in range(BATCH)])
    y_sample = h[M_P:].reshape(DEC_BATCH, T_S, D_MODEL)
    return (y_prompt, y_sample,
            ssm[0], jnp.stack(p_conv), p_wkv[None], p_shift[None], p_hgrn[None],
            ssm[1], jnp.stack(s_conv), s_wkv[None], s_shift[None], s_hgrn[None])
```
